```python
import jax, jax.numpy as jnp
from jax import lax
import numpy as np

D_MODEL = 1024
BATCH = 8
SEQ = 4096
DEPTH = 1

GLA_HEADS = 4
GLA_DK = 128
GLA_DV = 256
GLA_QK_W = GLA_HEADS * GLA_DK
GLA_V_W = GLA_HEADS * GLA_DV
GATE_RANK = 16
GATE_NORM = 16.0
CHUNK = 64
CONV_W = 1024
CONV_WIDTH = 3
MIX_W = GLA_V_W + CONV_W
SPLITS = [GLA_QK_W, GLA_QK_W, GLA_V_W, GLA_V_W, GATE_RANK, GATE_RANK,
          CONV_W, CONV_W, CONV_W, CONV_W]
IN_W = sum(SPLITS)
EPS = 1e-6

kernel_name = "hybrid_gla_shortconv_parallel_heads"


def rmsnorm(x, g):
    xf = x.astype(jnp.float32)
    y = xf * lax.rsqrt(jnp.mean(xf * xf, axis=-1, keepdims=True) + EPS)
    return (y * g.astype(jnp.float32)).astype(x.dtype)


def gla_direction(q, k, v, g, strict):
    bsz, nh, s, dk = q.shape
    dv = v.shape[-1]
    n = s // CHUNK
    q = q.reshape(bsz, nh, n, CHUNK, dk)
    k = k.reshape(bsz, nh, n, CHUNK, dk)
    v = v.reshape(bsz, nh, n, CHUNK, dv)
    g = g.reshape(bsz, nh, n, CHUNK, dk)
    b = jnp.cumsum(g, axis=3)
    b_ref = b[:, :, :, CHUNK // 2:CHUNK // 2 + 1, :]
    att = jnp.einsum('bhncd,bhnjd->bhncj', q * jnp.exp(b - b_ref), k * jnp.exp(b_ref - b))
    mask = jnp.tril(jnp.ones((CHUNK, CHUNK), dtype=bool), k=-1 if strict else 0)
    att = jnp.where(mask, att, 0.0)
    o_intra = jnp.einsum('bhncj,bhnjv->bhncv', att, v)
    b_last = b[:, :, :, -1:, :]
    q_in = q * jnp.exp(b)
    k_out = k * jnp.exp(b_last - b)
    decay_chunk = jnp.exp(b_last[:, :, :, 0, :])
    xs = (jnp.moveaxis(q_in, 2, 0), jnp.moveaxis(k_out, 2, 0),
          jnp.moveaxis(v, 2, 0), jnp.moveaxis(decay_chunk, 2, 0))

    def step(state, inp):
        qc, kc, vc, dc = inp
        o = jnp.einsum('bhcd,bhdv->bhcv', qc, state)
        state = dc[..., None] * state + jnp.einsum('bhcd,bhcv->bhdv', kc, vc)
        return state, o

    s0 = jnp.zeros((bsz, nh, dk, dv), jnp.float32)
    _, o_inter = lax.scan(step, s0, xs)
    o = o_intra + jnp.moveaxis(o_inter, 0, 2)
    return o.reshape(bsz, nh, s, dv)


def to_heads(t, d):
    bsz, s, _ = t.shape
    return t.reshape(bsz, s, -1, d).transpose(0, 2, 1, 3)


def hybrid_mixer(h, w_in, w_gk_f, b_gk_f, w_gk_b, b_gk_b, gla_norm_g, conv_w, conv_b, w_out):
    bsz, s, _ = h.shape
    proj = jnp.einsum('bsd,de->bse', h, w_in)
    idx = np.cumsum(SPLITS)[:-1].tolist()
    (q, k, v, z_a, lr_f, lr_b, b_gate, c_gate, h_c, z_c) = jnp.split(proj, idx, axis=-1)
    f32 = jnp.float32
    q = to_heads(q.astype(f32), GLA_DK) * (GLA_DK ** -0.5)
    k = to_heads(k.astype(f32), GLA_DK)
    v = to_heads(v.astype(f32), GLA_DV)
    g_f = jax.nn.log_sigmoid(jnp.einsum('bsr,re->bse', lr_f.astype(f32), w_gk_f.astype(f32))
                             + b_gk_f.astype(f32)) / GATE_NORM
    g_b = jax.nn.log_sigmoid(jnp.einsum('bsr,re->bse', lr_b.astype(f32), w_gk_b.astype(f32))
                             + b_gk_b.astype(f32)) / GATE_NORM
    g_f = to_heads(g_f, GLA_DK)
    g_b = to_heads(g_b, GLA_DK)
    o_fwd = gla_direction(q, k, v, g_f, strict=False)
    flip = lambda t: jnp.flip(t, axis=2)
    o_bwd = flip(gla_direction(flip(q), flip(k), flip(v), flip(g_b), strict=True))
    o = o_fwd + o_bwd
    o = o * lax.rsqrt(jnp.mean(o * o, axis=-1, keepdims=True) + EPS) * gla_norm_g.astype(f32)
    y_a = o.transpose(0, 2, 1, 3).reshape(bsz, s, GLA_V_W)
    y_a = (y_a * jax.nn.silu(z_a.astype(f32))).astype(h.dtype)
    u = c_gate * h_c
    up = jnp.pad(u, ((0, 0), (1, 1), (0, 0)))
    conv = (conv_w[0] * up[:, :-2] + conv_w[1] * up[:, 1:-1] + conv_w[2] * up[:, 2:]) + conv_b
    y_c = b_gate * conv * jax.nn.silu(z_c)
    y = jnp.concatenate([y_a, y_c.astype(h.dtype)], axis=-1)
    return jnp.einsum('bse,ed->bsd', y, w_out)


def setup_inputs(seed: int = 0) -> dict:
    key = jax.random.key(seed)
    ks = jax.random.split(key, 14)
    nrm = lambda k_, shp, sc: jax.random.normal(k_, shp, jnp.float32) * sc
    return {
        "x": nrm(ks[0], (BATCH, SEQ, D_MODEL), 1.0),
        "norm_g": 1.0 + nrm(ks[1], (DEPTH, D_MODEL), 0.02),
        "w_in": nrm(ks[2], (DEPTH, D_MODEL, IN_W), D_MODEL ** -0.5),
        "w_gk_f": nrm(ks[3], (DEPTH, GATE_RANK, GLA_QK_W), GATE_RANK ** -0.5),
        "b_gk_f": nrm(ks[4], (DEPTH, GLA_QK_W), 0.1),
        "w_gk_b": nrm(ks[5], (DEPTH, GATE_RANK, GLA_QK_W), GATE_RANK ** -0.5),
        "b_gk_b": nrm(ks[6], (DEPTH, GLA_QK_W), 0.1),
        "gla_norm_g": 1.0 + nrm(ks[7], (DEPTH, GLA_DV), 0.02),
        "conv_w": nrm(ks[8], (DEPTH, CONV_WIDTH, CONV_W), CONV_WIDTH ** -0.5),
        "conv_b": nrm(ks[9], (DEPTH, CONV_W), 0.02),
        "w_out": nrm(ks[10], (DEPTH, MIX_W, D_MODEL), MIX_W ** -0.5),
        "final_g": 1.0 + nrm(ks[11], (D_MODEL,), 0.02),
    }


def reference(x, norm_g, w_in, w_gk_f, b_gk_f, w_gk_b, b_gk_b, gla_norm_g, conv_w, conv_b, w_out, final_g):
    for layer in range(DEPTH):
        h = rmsnorm(x, norm_g[layer])
        x = x + hybrid_mixer(h, w_in[layer], w_gk_f[layer], b_gk_f[layer], w_gk_b[layer],
                             b_gk_b[layer], gla_norm_g[layer], conv_w[layer], conv_b[layer],
                             w_out[layer])
    return rmsnorm(x, final_g)
```

```python
import functools

import jax
import jax.numpy as jnp
from jax import lax
from jax.experimental import pallas as pl
from jax.experimental.pallas import tpu as pltpu

D_MODEL = 1024
GLA_HEADS = 4
GLA_DK = 128
GLA_DV = 256
GLA_QK_W = GLA_HEADS * GLA_DK
GLA_V_W = GLA_HEADS * GLA_DV
GATE_RANK = 16
GATE_NORM = 16.0
CHUNK = 64
CONV_W = 1024
EPS = 1e-6

LANES = 128
BF16_SUBLANES = 16
LR_W = LANES

_OFF_Q = 0
_OFF_K = _OFF_Q + GLA_QK_W
_OFF_V = _OFF_K + GLA_QK_W
_OFF_ZA = _OFF_V + GLA_V_W
_OFF_BG = _OFF_ZA + GLA_V_W
_OFF_CG = _OFF_BG + CONV_W
_OFF_HC = _OFF_CG + CONV_W
_OFF_ZC = _OFF_HC + CONV_W
_OFF_LR = _OFF_ZC + CONV_W
W_ALL = _OFF_LR + LR_W

TM_PROJ = 512
TS_GLA = 256
TM_OUT = 512
VMEM_LIMIT = 56 * 1024 * 1024


def _silu(z):
    return z * (1.0 / (1.0 + jnp.exp(-z)))


def _inproj_kernel(x_ref, g_ref, w_ref, q_ref, k_ref, v_ref, sz_ref, u_ref, gc_ref, lr_ref):
    x = x_ref[...]
    ms = jnp.mean(x * x, axis=-1, keepdims=True)
    h = (x * lax.rsqrt(ms + EPS) * g_ref[...]).astype(jnp.bfloat16)

    def proj(lo, hi):
        return jnp.dot(h, w_ref[:, lo:hi], preferred_element_type=jnp.float32)

    q_ref[...] = (proj(_OFF_Q, _OFF_K) * (GLA_DK ** -0.5)).astype(q_ref.dtype)
    k_ref[...] = proj(_OFF_K, _OFF_V).astype(k_ref.dtype)
    v_ref[...] = proj(_OFF_V, _OFF_ZA).astype(v_ref.dtype)
    sz_ref[...] = _silu(proj(_OFF_ZA, _OFF_BG)).astype(sz_ref.dtype)
    u_ref[...] = (proj(_OFF_CG, _OFF_HC) * proj(_OFF_HC, _OFF_ZC)).astype(u_ref.dtype)
    gc_ref[...] = (proj(_OFF_BG, _OFF_CG) * _silu(proj(_OFF_ZC, _OFF_LR))).astype(gc_ref.dtype)
    lr_ref[...] = proj(_OFF_LR, W_ALL)


def _inproj(x2, norm_g, w_all):
    m = x2.shape[0]
    tm = TM_PROJ
    row = lambda w: pl.BlockSpec((tm, w), lambda i: (i, 0))
    bf = jnp.bfloat16
    return pl.pallas_call(
        _inproj_kernel,
        grid=(m // tm,),
        in_specs=[
            row(D_MODEL),
            pl.BlockSpec((1, D_MODEL), lambda i: (0, 0)),
            pl.BlockSpec((D_MODEL, W_ALL), lambda i: (0, 0), pipeline_mode=pl.Buffered(1)),
        ],
        out_specs=[row(GLA_QK_W), row(GLA_QK_W), row(GLA_V_W), row(GLA_V_W),
                   row(CONV_W), row(CONV_W), row(LR_W)],
        out_shape=[
            jax.ShapeDtypeStruct((m, GLA_QK_W), bf),
            jax.ShapeDtypeStruct((m, GLA_QK_W), bf),
            jax.ShapeDtypeStruct((m, GLA_V_W), bf),
            jax.ShapeDtypeStruct((m, GLA_V_W), bf),
            jax.ShapeDtypeStruct((m, CONV_W), bf),
            jax.ShapeDtypeStruct((m, CONV_W), bf),
            jax.ShapeDtypeStruct((m, LR_W), jnp.float32),
        ],
        compiler_params=pltpu.CompilerParams(
            dimension_semantics=("arbitrary",), vmem_limit_bytes=VMEM_LIMIT),
        name="inproj",
    )(x2, norm_g, w_all)


def _gla_kernel(q_ref, k_ref, v_ref, lr_ref, wg_ref, bg_ref, o_ref, s_ref, *, rev, nblk):
    ts = q_ref.shape[0]
    nc = ts // CHUNK

    @pl.when(pl.program_id(1) == 0)
    def _():
        s_ref[...] = jnp.zeros_like(s_ref)

    logits = jnp.dot(lr_ref[...].astype(jnp.bfloat16), wg_ref[...],
                     preferred_element_type=jnp.float32) + bg_ref[...]
    g = (jnp.minimum(logits, 0.0) - jnp.log1p(jnp.exp(-jnp.abs(logits)))) * (1.0 / GATE_NORM)

    r_i = lax.broadcasted_iota(jnp.int32, (ts, ts), 0)
    c_i = lax.broadcasted_iota(jnp.int32, (ts, ts), 1)
    same = (r_i // CHUNK) == (c_i // CHUNK)
    tri = jnp.where(same & ((c_i >= r_i) if rev else (c_i <= r_i)), 1.0, 0.0).astype(jnp.bfloat16)
    g_hi = g.astype(jnp.bfloat16)
    g_lo = (g - g_hi.astype(jnp.float32)).astype(jnp.bfloat16)
    b_all = (jnp.dot(tri, g_hi, preferred_element_type=jnp.float32)
             + jnp.dot(tri, g_lo, preferred_element_type=jnp.float32))

    ri = lax.broadcasted_iota(jnp.int32, (CHUNK, CHUNK), 0)
    ci = lax.broadcasted_iota(jnp.int32, (CHUNK, CHUNK), 1)
    amask = (ci > ri) if rev else (ci <= ri)
    ref_row = CHUNK - 1 - CHUNK // 2 if rev else CHUNK // 2
    last_row = 0 if rev else CHUNK - 1
    order = range(nc - 1, -1, -1) if rev else range(nc)

    for h in range(GLA_HEADS):
        b_h = b_all[:, h * GLA_DK:(h + 1) * GLA_DK]
        b_t = jnp.transpose(b_h)
        q_h = q_ref[:, h * GLA_DK:(h + 1) * GLA_DK].astype(jnp.float32)
        k_h = k_ref[:, h * GLA_DK:(h + 1) * GLA_DK].astype(jnp.float32)
        for c in order:
            lo = c * CHUNK
            b = b_h[lo:lo + CHUNK]
            b_mid = b[ref_row:ref_row + 1]
            b_last = b[last_row:last_row + 1]
            q = q_h[lo:lo + CHUNK]
            k = k_h[lo:lo + CHUNK]
            v = v_ref[lo:lo + CHUNK, h * GLA_DV:(h + 1) * GLA_DV]
            qe = (q * jnp.exp(b - b_mid)).astype(jnp.bfloat16)
            ke = (k * jnp.exp(b_mid - b)).astype(jnp.bfloat16)
            q_in = (q * jnp.exp(b)).astype(jnp.bfloat16)
            k_out = (k * jnp.exp(b_last - b)).astype(jnp.bfloat16)
            att = lax.dot_general(qe, ke, (((1,), (1,)), ((), ())),
                                  preferred_element_type=jnp.float32)
            att = jnp.where(amask, att, 0.0).astype(jnp.bfloat16)
            state = s_ref[h]
            o = (jnp.dot(att, v, preferred_element_type=jnp.float32)
                 + jnp.dot(q_in, state.astype(jnp.bfloat16), preferred_element_type=jnp.float32))
            upd = lax.dot_general(k_out, v, (((0,), (0,)), ((), ())),
                                  preferred_element_type=jnp.float32)
            col = lo + last_row
            decay = jnp.exp(b_t[:, col:col + 1])
            s_ref[h] = decay * state + upd
            o_ref[lo:lo + CHUNK, h * GLA_DV:(h + 1) * GLA_DV] = o.astype(o_ref.dtype)


def _gla(q, k, v, lr, wg, bg, *, rev, batch, seq):
    ts = TS_GLA
    nblk = seq // ts
    if rev:
        idx = lambda b, s: (b * nblk + (nblk - 1 - s), 0)
    else:
        idx = lambda b, s: (b * nblk + s, 0)
    row = lambda w: pl.BlockSpec((ts, w), idx)
    const = lambda shp: pl.BlockSpec(shp, lambda b, s: (0, 0))
    return pl.pallas_call(
        functools.partial(_gla_kernel, rev=rev, nblk=nblk),
        grid=(batch, nblk),
        in_specs=[row(GLA_QK_W), row(GLA_QK_W), row(GLA_V_W), row(LR_W),
                  const((LR_W, GLA_QK_W)), const((1, GLA_QK_W))],
        out_specs=row(GLA_V_W),
        out_shape=jax.ShapeDtypeStruct((batch * seq, GLA_V_W), jnp.bfloat16),
        scratch_shapes=[pltpu.VMEM((GLA_HEADS, GLA_DK, GLA_DV), jnp.float32)],
        compiler_params=pltpu.CompilerParams(
            dimension_semantics=("arbitrary", "arbitrary"), vmem_limit_bytes=VMEM_LIMIT),
        name="gla_bwd" if rev else "gla_fwd",
    )(q, k, v, lr, wg, bg)


def _out_kernel(x_ref, of_ref, ob_ref, sz_ref, u_ref, up_ref, un_ref, gc_ref,
                gn_ref, cw_ref, cb_ref, wo_ref, fg_ref, out_ref, *, tiles_per_seq):
    tm = x_ref.shape[0]
    pos = pl.program_id(0) % tiles_per_seq

    o = of_ref[...].astype(jnp.float32) + ob_ref[...].astype(jnp.float32)
    acc = None
    for h in range(GLA_HEADS):
        sl = slice(h * GLA_DV, (h + 1) * GLA_DV)
        o_h = o[:, sl]
        ms = jnp.mean(o_h * o_h, axis=-1, keepdims=True)
        y_h = o_h * lax.rsqrt(ms + EPS) * gn_ref[...] * sz_ref[:, sl].astype(jnp.float32)
        part = jnp.dot(y_h.astype(jnp.bfloat16), wo_ref[sl, :], preferred_element_type=jnp.float32)
        acc = part if acc is None else acc + part

    u = u_ref[...].astype(jnp.float32)
    prev_row = jnp.where(pos == 0, 0.0, up_ref[BF16_SUBLANES - 1:BF16_SUBLANES, :].astype(jnp.float32))
    next_row = jnp.where(pos == tiles_per_seq - 1, 0.0, un_ref[0:1, :].astype(jnp.float32))
    t_i = lax.broadcasted_iota(jnp.int32, (tm, 1), 0)
    u_prev = jnp.where(t_i == 0, prev_row, pltpu.roll(u, 1, 0))
    u_next = jnp.where(t_i == tm - 1, next_row, pltpu.roll(u, tm - 1, 0))
    conv = cw_ref[0:1, :] * u_prev + cw_ref[1:2, :] * u + cw_ref[2:3, :] * u_next + cb_ref[...]
    y_c = (gc_ref[...].astype(jnp.float32) * conv).astype(jnp.bfloat16)
    acc = acc + jnp.dot(y_c, wo_ref[GLA_V_W:, :], preferred_element_type=jnp.float32)

    xo = x_ref[...] + acc
    ms = jnp.mean(xo * xo, axis=-1, keepdims=True)
    out_ref[...] = xo * lax.rsqrt(ms + EPS) * fg_ref[...]


def _outproj(x2, o_f, o_b, sz, u, gc, gn, cw, cb, wo, fg, *, seq):
    m = x2.shape[0]
    tm = TM_OUT
    tiles_per_seq = seq // tm
    sub = tm // BF16_SUBLANES
    nsub = m // BF16_SUBLANES
    row = lambda w: pl.BlockSpec((tm, w), lambda i: (i, 0))
    const = lambda shp: pl.BlockSpec(shp, lambda i: (0, 0))
    halo_prev = pl.BlockSpec((BF16_SUBLANES, CONV_W), lambda i: (jnp.maximum(i * sub - 1, 0), 0))
    halo_next = pl.BlockSpec((BF16_SUBLANES, CONV_W),
                             lambda i: (jnp.minimum((i + 1) * sub, nsub - 1), 0))
    return pl.pallas_call(
        functools.partial(_out_kernel, tiles_per_seq=tiles_per_seq),
        grid=(m // tm,),
        in_specs=[row(D_MODEL), row(GLA_V_W), row(GLA_V_W), row(GLA_V_W),
                  row(CONV_W), halo_prev, halo_next, row(CONV_W),
                  const((1, GLA_DV)), const((3, CONV_W)), const((1, CONV_W)),
                  const((GLA_V_W + CONV_W, D_MODEL)), const((1, D_MODEL))],
        out_specs=row(D_MODEL),
        out_shape=jax.ShapeDtypeStruct((m, D_MODEL), jnp.float32),
        compiler_params=pltpu.CompilerParams(
            dimension_semantics=("arbitrary",), vmem_limit_bytes=VMEM_LIMIT),
        name="outproj",
    )(x2, o_f, o_b, sz, u, u, u, gc, gn, cw, cb, wo, fg)


def _regroup_w_in(w):
    q, k, v, za, lrf, lrb, bg, cg, hc, zc = jnp.split(
        w, [512, 1024, 2048, 3072, 3088, 3104, 4128, 5152, 6176], axis=1)
    pad = jnp.zeros((w.shape[0], LR_W - 2 * GATE_RANK), w.dtype)
    return jnp.concatenate([q, k, v, za, bg, cg, hc, zc, lrf, lrb, pad], axis=1).astype(jnp.bfloat16)


def _gate_weight(w_gk, first_row):
    full = jnp.zeros((LR_W, GLA_QK_W), jnp.float32)
    return lax.dynamic_update_slice(full, w_gk, (first_row, 0)).astype(jnp.bfloat16)


def kernel(x, norm_g, w_in, w_gk_f, b_gk_f, w_gk_b, b_gk_b, gla_norm_g, conv_w, conv_b, w_out, final_g):
    batch, seq, d = x.shape
    depth = w_in.shape[0]
    assert depth == 1 and d == D_MODEL and seq % TM_OUT == 0 and seq % TS_GLA == 0
    x2 = x.reshape(batch * seq, d)
    q, k, v, sz, u, gc, lr = _inproj(x2, norm_g[0][None, :], _regroup_w_in(w_in[0]))
    o_f = _gla(q, k, v, lr, _gate_weight(w_gk_f[0], 0), b_gk_f[0][None, :],
               rev=False, batch=batch, seq=seq)
    o_b = _gla(q, k, v, lr, _gate_weight(w_gk_b[0], GATE_RANK), b_gk_b[0][None, :],
               rev=True, batch=batch, seq=seq)
    out = _outproj(x2, o_f, o_b, sz, u, gc, gla_norm_g[0][None, :], conv_w[0], conv_b[0][None, :],
                   w_out[0].astype(jnp.bfloat16), final_g[None, :], seq=seq)
    return out.reshape(batch, seq, d)
```

```python
import functools

import jax
import jax.numpy as jnp
import numpy as np
from jax import lax
from jax.experimental import pallas as pl
from jax.experimental.pallas import tpu as pltpu

D_MODEL = 1024
GLA_HEADS = 4
GLA_DK = 128
GLA_DV = 256
GLA_QK_W = GLA_HEADS * GLA_DK
GLA_V_W = GLA_HEADS * GLA_DV
GATE_RANK = 16
GATE_NORM = 16.0
CHUNK = 64
CONV_W = 1024
EPS = 1e-6

LANES = 128
BF16_SUBLANES = 16
LR_W = LANES

_OFF_Q = 0
_OFF_K = _OFF_Q + GLA_QK_W
_OFF_V = _OFF_K + GLA_QK_W
_OFF_ZA = _OFF_V + GLA_V_W
_OFF_BG = _OFF_ZA + GLA_V_W
_OFF_CG = _OFF_BG + CONV_W
_OFF_HC = _OFF_CG + CONV_W
_OFF_ZC = _OFF_HC + CONV_W
_OFF_LR = _OFF_ZC + CONV_W
W_ALL = _OFF_LR + LR_W

TM_PROJ = 512
TS_GLA = 256
TM_OUT = 512
VMEM_LIMIT = 56 * 1024 * 1024


def _silu(z):
    return z * (1.0 / (1.0 + jnp.exp(-z)))


def _inproj_kernel(x_ref, g_ref, w_ref, q_ref, k_ref, v_ref, sz_ref, u_ref, gc_ref, lr_ref):
    x = x_ref[...]
    ms = jnp.mean(x * x, axis=-1, keepdims=True)
    h = (x * lax.rsqrt(ms + EPS) * g_ref[...]).astype(jnp.bfloat16)

    def proj(lo, hi):
        return jnp.dot(h, w_ref[:, lo:hi], preferred_element_type=jnp.float32)

    q_ref[...] = (proj(_OFF_Q, _OFF_K) * (GLA_DK ** -0.5)).astype(q_ref.dtype)
    k_ref[...] = proj(_OFF_K, _OFF_V).astype(k_ref.dtype)
    v_ref[...] = proj(_OFF_V, _OFF_ZA).astype(v_ref.dtype)
    sz_ref[...] = _silu(proj(_OFF_ZA, _OFF_BG)).astype(sz_ref.dtype)
    u_ref[...] = (proj(_OFF_CG, _OFF_HC) * proj(_OFF_HC, _OFF_ZC)).astype(u_ref.dtype)
    gc_ref[...] = (proj(_OFF_BG, _OFF_CG) * _silu(proj(_OFF_ZC, _OFF_LR))).astype(gc_ref.dtype)
    lr_ref[...] = proj(_OFF_LR, W_ALL)


def _inproj(x2, norm_g, w_all):
    m = x2.shape[0]
    tm = TM_PROJ
    row = lambda w: pl.BlockSpec((tm, w), lambda i: (i, 0))
    bf = jnp.bfloat16
    return pl.pallas_call(
        _inproj_kernel,
        grid=(m // tm,),
        in_specs=[
            row(D_MODEL),
            pl.BlockSpec((1, D_MODEL), lambda i: (0, 0)),
            pl.BlockSpec((D_MODEL, W_ALL), lambda i: (0, 0), pipeline_mode=pl.Buffered(1)),
        ],
        out_specs=[row(GLA_QK_W), row(GLA_QK_W), row(GLA_V_W), row(GLA_V_W),
                   row(CONV_W), row(CONV_W), row(LR_W)],
        out_shape=[
            jax.ShapeDtypeStruct((m, GLA_QK_W), bf),
            jax.ShapeDtypeStruct((m, GLA_QK_W), bf),
            jax.ShapeDtypeStruct((m, GLA_V_W), bf),
            jax.ShapeDtypeStruct((m, GLA_V_W), bf),
            jax.ShapeDtypeStruct((m, CONV_W), bf),
            jax.ShapeDtypeStruct((m, CONV_W), bf),
            jax.ShapeDtypeStruct((m, LR_W), jnp.float32),
        ],
        compiler_params=pltpu.CompilerParams(
            dimension_semantics=("arbitrary",), vmem_limit_bytes=VMEM_LIMIT),
        name="inproj",
    )(x2, norm_g, w_all)


def _block_tri(ts, rev):
    r = np.arange(ts)[:, None]
    c = np.arange(ts)[None, :]
    keep = (r // CHUNK == c // CHUNK) & ((c >= r) if rev else (c <= r))
    return jnp.asarray(keep, dtype=jnp.bfloat16)


def _decay_terms(lr_ref, wg_ref, bg_ref, tri_ref):
    logits = jnp.dot(lr_ref[...].astype(jnp.bfloat16), wg_ref[...],
                     preferred_element_type=jnp.float32) + bg_ref[...]
    g = (jnp.minimum(logits, 0.0) - jnp.log(1.0 + jnp.exp(-jnp.abs(logits)))) * (1.0 / GATE_NORM)
    tri = tri_ref[...]
    g_hi = g.astype(jnp.bfloat16)
    g_lo = (g - g_hi.astype(jnp.float32)).astype(jnp.bfloat16)
    return (jnp.dot(tri, g_hi, preferred_element_type=jnp.float32)
            + jnp.dot(tri, g_lo, preferred_element_type=jnp.float32))


def _gla_kernel(qf_ref, kf_ref, vf_ref, lrf_ref, qb_ref, kb_ref, vb_ref, lrb_ref,
                wgf_ref, bgf_ref, wgb_ref, bgb_ref, trif_ref, trib_ref,
                of_ref, ob_ref, s_ref):
    ts = qf_ref.shape[0]
    nc = ts // CHUNK

    @pl.when(pl.program_id(1) == 0)
    def _():
        s_ref[...] = jnp.zeros_like(s_ref)

    ri = lax.broadcasted_iota(jnp.int32, (CHUNK, CHUNK), 0)
    ci = lax.broadcasted_iota(jnp.int32, (CHUNK, CHUNK), 1)
    dirs = (
        dict(q=qf_ref, k=kf_ref, v=vf_ref, o=of_ref, s=s_ref.at[0], amask=ci <= ri,
             ref_row=CHUNK // 2, last_row=CHUNK - 1, order=tuple(range(nc)),
             b=_decay_terms(lrf_ref, wgf_ref, bgf_ref, trif_ref)),
        dict(q=qb_ref, k=kb_ref, v=vb_ref, o=ob_ref, s=s_ref.at[1], amask=ci > ri,
             ref_row=CHUNK - 1 - CHUNK // 2, last_row=0, order=tuple(range(nc - 1, -1, -1)),
             b=_decay_terms(lrb_ref, wgb_ref, bgb_ref, trib_ref)),
    )

    streams = []
    for d in dirs:
        for h in range(GLA_HEADS):
            ksl = slice(h * GLA_DK, (h + 1) * GLA_DK)
            b3 = d["b"][:, ksl].reshape(nc, CHUNK, GLA_DK)
            b_mid = b3[:, d["ref_row"]:d["ref_row"] + 1, :]
            b_last = b3[:, d["last_row"]:d["last_row"] + 1, :]
            q3 = d["q"][:, ksl].astype(jnp.float32).reshape(nc, CHUNK, GLA_DK)
            k3 = d["k"][:, ksl].astype(jnp.float32).reshape(nc, CHUNK, GLA_DK)
            streams.append(dict(
                d=d, h=h, vsl=slice(h * GLA_DV, (h + 1) * GLA_DV),
                qe=(q3 * jnp.exp(b3 - b_mid)).astype(jnp.bfloat16),
                ke=(k3 * jnp.exp(b_mid - b3)).astype(jnp.bfloat16),
                q_in=(q3 * jnp.exp(b3)).astype(jnp.bfloat16),
                k_out=(k3 * jnp.exp(b_last - b3)).astype(jnp.bfloat16),
                b_t=jnp.transpose(d["b"][:, ksl]),
            ))

    for st in streams:
        d = st["d"]
        st["att"] = [
            jnp.where(d["amask"],
                      lax.dot_general(st["qe"][c], st["ke"][c], (((1,), (1,)), ((), ())),
                                      preferred_element_type=jnp.float32),
                      0.0).astype(jnp.bfloat16)
            for c in range(nc)]
    for st in streams:
        d = st["d"]
        st["upd"] = [
            lax.dot_general(st["k_out"][c], d["v"][c * CHUNK:(c + 1) * CHUNK, st["vsl"]],
                            (((0,), (0,)), ((), ())), preferred_element_type=jnp.float32)
            for c in range(nc)]

    for step in range(nc):
        for st in streams:
            d = st["d"]
            c = d["order"][step]
            rows = slice(c * CHUNK, (c + 1) * CHUNK)
            state = d["s"][st["h"]]
            o = (jnp.dot(st["att"][c], d["v"][rows, st["vsl"]], preferred_element_type=jnp.float32)
                 + jnp.dot(st["q_in"][c], state.astype(jnp.bfloat16),
                           preferred_element_type=jnp.float32))
            d["o"][rows, st["vsl"]] = o.astype(d["o"].dtype)
            col = c * CHUNK + d["last_row"]
            decay = jnp.exp(st["b_t"][:, col:col + 1])
            d["s"][st["h"]] = decay * state + st["upd"][c]


def _gla(q, k, v, lr, wg_f, bg_f, wg_b, bg_b, *, batch, seq):
    ts = TS_GLA
    nblk = seq // ts
    fwd = lambda b, s: (b * nblk + s, 0)
    bwd = lambda b, s: (b * nblk + (nblk - 1 - s), 0)
    rows = lambda idx: [pl.BlockSpec((ts, w), idx) for w in (GLA_QK_W, GLA_QK_W, GLA_V_W, LR_W)]
    const = lambda shp: pl.BlockSpec(shp, lambda b, s: (0, 0))
    gate = [const((LR_W, GLA_QK_W)), const((1, GLA_QK_W))]
    o_shape = jax.ShapeDtypeStruct((batch * seq, GLA_V_W), jnp.bfloat16)
    return pl.pallas_call(
        _gla_kernel,
        grid=(batch, nblk),
        in_specs=rows(fwd) + rows(bwd) + gate + gate + [const((ts, ts)), const((ts, ts))],
        out_specs=[pl.BlockSpec((ts, GLA_V_W), fwd), pl.BlockSpec((ts, GLA_V_W), bwd)],
        out_shape=[o_shape, o_shape],
        scratch_shapes=[pltpu.VMEM((2, GLA_HEADS, GLA_DK, GLA_DV), jnp.float32)],
        compiler_params=pltpu.CompilerParams(
            dimension_semantics=("arbitrary", "arbitrary"), vmem_limit_bytes=VMEM_LIMIT),
        name="gla",
    )(q, k, v, lr, q, k, v, lr, wg_f, bg_f, wg_b, bg_b,
      _block_tri(ts, False), _block_tri(ts, True))


def _out_kernel(x_ref, of_ref, ob_ref, sz_ref, u_ref, up_ref, un_ref, gc_ref,
                gn_ref, cw_ref, cb_ref, wo_ref, fg_ref, out_ref, *, tiles_per_seq):
    tm = x_ref.shape[0]
    pos = pl.program_id(0) % tiles_per_seq

    o = of_ref[...].astype(jnp.float32) + ob_ref[...].astype(jnp.float32)
    acc = None
    for h in range(GLA_HEADS):
        sl = slice(h * GLA_DV, (h + 1) * GLA_DV)
        o_h = o[:, sl]
        ms = jnp.mean(o_h * o_h, axis=-1, keepdims=True)
        y_h = o_h * lax.rsqrt(ms + EPS) * gn_ref[...] * sz_ref[:, sl].astype(jnp.float32)
        part = jnp.dot(y_h.astype(jnp.bfloat16), wo_ref[sl, :], preferred_element_type=jnp.float32)
        acc = part if acc is None else acc + part

    u = u_ref[...].astype(jnp.float32)
    prev_row = jnp.where(pos == 0, 0.0, up_ref[BF16_SUBLANES - 1:BF16_SUBLANES, :].astype(jnp.float32))
    next_row = jnp.where(pos == tiles_per_seq - 1, 0.0, un_ref[0:1, :].astype(jnp.float32))
    t_i = lax.broadcasted_iota(jnp.int32, (tm, 1), 0)
    u_prev = jnp.where(t_i == 0, prev_row, pltpu.roll(u, 1, 0))
    u_next = jnp.where(t_i == tm - 1, next_row, pltpu.roll(u, tm - 1, 0))
    conv = cw_ref[0:1, :] * u_prev + cw_ref[1:2, :] * u + cw_ref[2:3, :] * u_next + cb_ref[...]
    y_c = (gc_ref[...].astype(jnp.float32) * conv).astype(jnp.bfloat16)
    acc = acc + jnp.dot(y_c, wo_ref[GLA_V_W:, :], preferred_element_type=jnp.float32)

    xo = x_ref[...] + acc
    ms = jnp.mean(xo * xo, axis=-1, keepdims=True)
    out_ref[...] = xo * lax.rsqrt(ms + EPS) * fg_ref[...]


def _outproj(x2, o_f, o_b, sz, u, gc, gn, cw, cb, wo, fg, *, seq):
    m = x2.shape[0]
    tm = TM_OUT
    tiles_per_seq = seq // tm
    sub = tm // BF16_SUBLANES
    nsub = m // BF16_SUBLANES
    row = lambda w: pl.BlockSpec((tm, w), lambda i: (i, 0))
    const = lambda shp: pl.BlockSpec(shp, lambda i: (0, 0))
    halo_prev = pl.BlockSpec((BF16_SUBLANES, CONV_W), lambda i: (jnp.maximum(i * sub - 1, 0), 0))
    halo_next = pl.BlockSpec((BF16_SUBLANES, CONV_W),
                             lambda i: (jnp.minimum((i + 1) * sub, nsub - 1), 0))
    return pl.pallas_call(
        functools.partial(_out_kernel, tiles_per_seq=tiles_per_seq),
        grid=(m // tm,),
        in_specs=[row(D_MODEL), row(GLA_V_W), row(GLA_V_W), row(GLA_V_W),
                  row(CONV_W), halo_prev, halo_next, row(CONV_W),
                  const((1, GLA_DV)), const((3, CONV_W)), const((1, CONV_W)),
                  const((GLA_V_W + CONV_W, D_MODEL)), const((1, D_MODEL))],
        out_specs=row(D_MODEL),
        out_shape=jax.ShapeDtypeStruct((m, D_MODEL), jnp.float32),
        compiler_params=pltpu.CompilerParams(
            dimension_semantics=("arbitrary",), vmem_limit_bytes=VMEM_LIMIT),
        name="outproj",
    )(x2, o_f, o_b, sz, u, u, u, gc, gn, cw, cb, wo, fg)


def _regroup_w_in(w):
    q, k, v, za, lrf, lrb, bg, cg, hc, zc = jnp.split(
        w, [512, 1024, 2048, 3072, 3088, 3104, 4128, 5152, 6176], axis=1)
    pad = jnp.zeros((w.shape[0], LR_W - 2 * GATE_RANK), w.dtype)
    return jnp.concatenate([q, k, v, za, bg, cg, hc, zc, lrf, lrb, pad], axis=1).astype(jnp.bfloat16)


def _gate_weight(w_gk, first_row):
    full = jnp.zeros((LR_W, GLA_QK_W), jnp.float32)
    return lax.dynamic_update_slice(full, w_gk, (first_row, 0)).astype(jnp.bfloat16)


def kernel(x, norm_g, w_in, w_gk_f, b_gk_f, w_gk_b, b_gk_b, gla_norm_g, conv_w, conv_b, w_out, final_g):
    batch, seq, d = x.shape
    depth = w_in.shape[0]
    assert depth == 1 and d == D_MODEL and seq % TM_OUT == 0 and seq % TS_GLA == 0
    x2 = x.reshape(batch * seq, d)
    q, k, v, sz, u, gc, lr = _inproj(x2, norm_g[0][None, :], _regroup_w_in(w_in[0]))
    o_f, o_b = _gla(q, k, v, lr, _gate_weight(w_gk_f[0], 0), b_gk_f[0][None, :],
                    _gate_weight(w_gk_b[0], GATE_RANK), b_gk_b[0][None, :], batch=batch, seq=seq)
    out = _outproj(x2, o_f, o_b, sz, u, gc, gla_norm_g[0][None, :], conv_w[0], conv_b[0][None, :],
                   w_out[0].astype(jnp.bfloat16), final_g[None, :], seq=seq)
    return out.reshape(batch, seq, d)
```

```python
import functools

import jax
import jax.numpy as jnp
import numpy as np
from jax import lax
from jax.experimental import pallas as pl
from jax.experimental.pallas import tpu as pltpu

D_MODEL = 1024
GLA_HEADS = 4
GLA_DK = 128
GLA_DV = 256
GLA_QK_W = GLA_HEADS * GLA_DK
GLA_V_W = GLA_HEADS * GLA_DV
GATE_RANK = 16
GATE_NORM = 16.0
CHUNK = 64
CONV_W = 1024
EPS = 1e-6

LANES = 128
BF16_SUBLANES = 16
LR_W = LANES

_OFF_Q = 0
_OFF_K = _OFF_Q + GLA_QK_W
_OFF_V = _OFF_K + GLA_QK_W
_OFF_ZA = _OFF_V + GLA_V_W
_OFF_BG = _OFF_ZA + GLA_V_W
_OFF_CG = _OFF_BG + CONV_W
_OFF_HC = _OFF_CG + CONV_W
_OFF_ZC = _OFF_HC + CONV_W
_OFF_LR = _OFF_ZC + CONV_W
W_ALL = _OFF_LR + LR_W

TM_PROJ = 512
TS_GLA = 512
TM_OUT = 512
VMEM_LIMIT = 56 * 1024 * 1024


def _silu(z):
    return z * (1.0 / (1.0 + jnp.exp(-z)))


def _inproj_kernel(x_ref, g_ref, w_ref, q_ref, k_ref, v_ref, sz_ref, u_ref, gc_ref, lr_ref):
    x = x_ref[...]
    ms = jnp.mean(x * x, axis=-1, keepdims=True)
    h = (x * lax.rsqrt(ms + EPS) * g_ref[...]).astype(jnp.bfloat16)

    def proj(lo, hi):
        return jnp.dot(h, w_ref[:, lo:hi], preferred_element_type=jnp.float32)

    q_ref[...] = (proj(_OFF_Q, _OFF_K) * (GLA_DK ** -0.5)).astype(q_ref.dtype)
    k_ref[...] = proj(_OFF_K, _OFF_V).astype(k_ref.dtype)
    v_ref[...] = proj(_OFF_V, _OFF_ZA).astype(v_ref.dtype)
    sz_ref[...] = _silu(proj(_OFF_ZA, _OFF_BG)).astype(sz_ref.dtype)
    u_ref[...] = (proj(_OFF_CG, _OFF_HC) * proj(_OFF_HC, _OFF_ZC)).astype(u_ref.dtype)
    gc_ref[...] = (proj(_OFF_BG, _OFF_CG) * _silu(proj(_OFF_ZC, _OFF_LR))).astype(gc_ref.dtype)
    lr_ref[...] = proj(_OFF_LR, W_ALL)


def _inproj(x2, norm_g, w_all):
    m = x2.shape[0]
    tm = TM_PROJ
    row = lambda w: pl.BlockSpec((tm, w), lambda i: (i, 0))
    bf = jnp.bfloat16
    return pl.pallas_call(
        _inproj_kernel,
        grid=(m // tm,),
        in_specs=[
            row(D_MODEL),
            pl.BlockSpec((1, D_MODEL), lambda i: (0, 0)),
            pl.BlockSpec((D_MODEL, W_ALL), lambda i: (0, 0), pipeline_mode=pl.Buffered(1)),
        ],
        out_specs=[row(GLA_QK_W), row(GLA_QK_W), row(GLA_V_W), row(GLA_V_W),
                   row(CONV_W), row(CONV_W), row(LR_W)],
        out_shape=[
            jax.ShapeDtypeStruct((m, GLA_QK_W), bf),
            jax.ShapeDtypeStruct((m, GLA_QK_W), bf),
            jax.ShapeDtypeStruct((m, GLA_V_W), bf),
            jax.ShapeDtypeStruct((m, GLA_V_W), bf),
            jax.ShapeDtypeStruct((m, CONV_W), bf),
            jax.ShapeDtypeStruct((m, CONV_W), bf),
            jax.ShapeDtypeStruct((m, LR_W), jnp.float32),
        ],
        compiler_params=pltpu.CompilerParams(
            dimension_semantics=("arbitrary",), vmem_limit_bytes=VMEM_LIMIT),
        name="inproj",
    )(x2, norm_g, w_all)


def _pair_tri(rev):
    t = np.arange(CHUNK)
    tri = (t[None, :] >= t[:, None]) if rev else (t[None, :] <= t[:, None])
    zero = np.zeros_like(tri)
    keep = np.block([[tri, tri, zero, zero], [zero, zero, tri, tri]])
    return jnp.asarray(keep, dtype=jnp.bfloat16)


def _decay_terms(lr_ref, wg_ref, bg_ref, tri_ref):
    ts = lr_ref.shape[0]
    logits = jnp.dot(lr_ref[...].astype(jnp.bfloat16), wg_ref[...],
                     preferred_element_type=jnp.float32) + bg_ref[...]
    g = (jnp.minimum(logits, 0.0) - jnp.log(1.0 + jnp.exp(-jnp.abs(logits)))) * (1.0 / GATE_NORM)
    tri = tri_ref[...]
    g_hi = g.astype(jnp.bfloat16)
    g_lo = (g - g_hi.astype(jnp.float32)).astype(jnp.bfloat16)
    out = []
    for p in range(ts // (2 * CHUNK)):
        c0 = slice(2 * p * CHUNK, (2 * p + 1) * CHUNK)
        c1 = slice((2 * p + 1) * CHUNK, (2 * p + 2) * CHUNK)
        rhs = jnp.concatenate([g_hi[c0], g_lo[c0], g_hi[c1], g_lo[c1]], axis=0)
        out.append(jnp.dot(tri, rhs, preferred_element_type=jnp.float32))
    return jnp.concatenate(out, axis=0)


def _gla_kernel(qf_ref, kf_ref, vf_ref, lrf_ref, qb_ref, kb_ref, vb_ref, lrb_ref,
                wgf_ref, bgf_ref, wgb_ref, bgb_ref, trif_ref, trib_ref,
                of_ref, ob_ref, s_ref):
    ts = qf_ref.shape[0]
    nc = ts // CHUNK

    @pl.when(pl.program_id(1) == 0)
    def _():
        s_ref[...] = jnp.zeros_like(s_ref)

    ri = lax.broadcasted_iota(jnp.int32, (CHUNK, CHUNK), 0)
    ci = lax.broadcasted_iota(jnp.int32, (CHUNK, CHUNK), 1)
    dirs = (
        dict(q=qf_ref, k=kf_ref, v=vf_ref, o=of_ref, s=s_ref.at[0], amask=ci <= ri,
             ref_row=CHUNK // 2, last_row=CHUNK - 1, order=tuple(range(nc)),
             b=_decay_terms(lrf_ref, wgf_ref, bgf_ref, trif_ref)),
        dict(q=qb_ref, k=kb_ref, v=vb_ref, o=ob_ref, s=s_ref.at[1], amask=ci > ri,
             ref_row=CHUNK - 1 - CHUNK // 2, last_row=0, order=tuple(range(nc - 1, -1, -1)),
             b=_decay_terms(lrb_ref, wgb_ref, bgb_ref, trib_ref)),
    )

    streams = []
    for d in dirs:
        b_last_rows = d["b"].reshape(nc, CHUNK, GLA_QK_W)[:, d["last_row"], :]
        for h in range(GLA_HEADS):
            ksl = slice(h * GLA_DK, (h + 1) * GLA_DK)
            streams.append(dict(
                d=d, h=h, ksl=ksl, vsl=slice(h * GLA_DV, (h + 1) * GLA_DV),
                decay_t=jnp.transpose(jnp.exp(b_last_rows[:, ksl])),
            ))

    def intra(st, step):
        d = st["d"]
        c = d["order"][step]
        rows = slice(c * CHUNK, (c + 1) * CHUNK)
        b = d["b"][rows, st["ksl"]]
        b_mid = b[d["ref_row"]:d["ref_row"] + 1]
        b_last = b[d["last_row"]:d["last_row"] + 1]
        qe = d["q"][rows, st["ksl"]].astype(jnp.float32) * jnp.exp(b - b_mid)
        ke = d["k"][rows, st["ksl"]].astype(jnp.float32) * jnp.exp(b_mid - b)
        att = lax.dot_general(qe.astype(jnp.bfloat16), ke.astype(jnp.bfloat16),
                              (((1,), (1,)), ((), ())), preferred_element_type=jnp.float32)
        return dict(c=c, rows=rows, att=att,
                    q_in=(qe * jnp.exp(b_mid)).astype(jnp.bfloat16),
                    k_out=(ke * jnp.exp(b_last - b_mid)).astype(jnp.bfloat16))

    ahead = {id(st): intra(st, 0) for st in streams}
    for step in range(nc):
        for st in streams:
            d = st["d"]
            cur = ahead[id(st)]
            if step + 1 < nc:
                ahead[id(st)] = intra(st, step + 1)
            v = d["v"][cur["rows"], st["vsl"]]
            state = d["s"][st["h"]]
            att = jnp.where(d["amask"], cur["att"], 0.0).astype(jnp.bfloat16)
            lhs = jnp.concatenate([cur["q_in"], att], axis=1)
            rhs = jnp.concatenate([state.astype(jnp.bfloat16), v], axis=0)
            o = jnp.dot(lhs, rhs, preferred_element_type=jnp.float32)
            d["o"][cur["rows"], st["vsl"]] = o.astype(d["o"].dtype)
            upd = lax.dot_general(cur["k_out"], v, (((0,), (0,)), ((), ())),
                                  preferred_element_type=jnp.float32)
            decay = st["decay_t"][:, cur["c"]:cur["c"] + 1]
            d["s"][st["h"]] = decay * state + upd


def _gla(q, k, v, lr, wg_f, bg_f, wg_b, bg_b, *, batch, seq):
    ts = TS_GLA
    nblk = seq // ts
    fwd = lambda b, s: (b * nblk + s, 0)
    bwd = lambda b, s: (b * nblk + (nblk - 1 - s), 0)
    rows = lambda idx: [pl.BlockSpec((ts, w), idx) for w in (GLA_QK_W, GLA_QK_W, GLA_V_W, LR_W)]
    const = lambda shp: pl.BlockSpec(shp, lambda b, s: (0, 0))
    gate = [const((LR_W, GLA_QK_W)), const((1, GLA_QK_W))]
    o_shape = jax.ShapeDtypeStruct((batch * seq, GLA_V_W), jnp.bfloat16)
    return pl.pallas_call(
        _gla_kernel,
        grid=(batch, nblk),
        in_specs=rows(fwd) + rows(bwd) + gate + gate + [const((2 * CHUNK, 4 * CHUNK))] * 2,
        out_specs=[pl.BlockSpec((ts, GLA_V_W), fwd), pl.BlockSpec((ts, GLA_V_W), bwd)],
        out_shape=[o_shape, o_shape],
        scratch_shapes=[pltpu.VMEM((2, GLA_HEADS, GLA_DK, GLA_DV), jnp.float32)],
        compiler_params=pltpu.CompilerParams(
            dimension_semantics=("arbitrary", "arbitrary"), vmem_limit_bytes=VMEM_LIMIT),
        name="gla",
    )(q, k, v, lr, q, k, v, lr, wg_f, bg_f, wg_b, bg_b,
      _pair_tri(False), _pair_tri(True))


def _out_kernel(x_ref, of_ref, ob_ref, sz_ref, u_ref, up_ref, un_ref, gc_ref,
                gn_ref, cw_ref, cb_ref, wo_ref, fg_ref, out_ref, *, tiles_per_seq):
    tm = x_ref.shape[0]
    pos = pl.program_id(0) % tiles_per_seq

    o = of_ref[...].astype(jnp.float32) + ob_ref[...].astype(jnp.float32)
    acc = None
    for h in range(GLA_HEADS):
        sl = slice(h * GLA_DV, (h + 1) * GLA_DV)
        o_h = o[:, sl]
        ms = jnp.mean(o_h * o_h, axis=-1, keepdims=True)
        y_h = o_h * lax.rsqrt(ms + EPS) * gn_ref[...] * sz_ref[:, sl].astype(jnp.float32)
        part = jnp.dot(y_h.astype(jnp.bfloat16), wo_ref[sl, :], preferred_element_type=jnp.float32)
        acc = part if acc is None else acc + part

    u = u_ref[...].astype(jnp.float32)
    prev_row = jnp.where(pos == 0, 0.0, up_ref[BF16_SUBLANES - 1:BF16_SUBLANES, :].astype(jnp.float32))
    next_row = jnp.where(pos == tiles_per_seq - 1, 0.0, un_ref[0:1, :].astype(jnp.float32))
    t_i = lax.broadcasted_iota(jnp.int32, (tm, 1), 0)
    u_prev = jnp.where(t_i == 0, prev_row, pltpu.roll(u, 1, 0))
    u_next = jnp.where(t_i == tm - 1, next_row, pltpu.roll(u, tm - 1, 0))
    conv = cw_ref[0:1, :] * u_prev + cw_ref[1:2, :] * u + cw_ref[2:3, :] * u_next + cb_ref[...]
    y_c = (gc_ref[...].astype(jnp.float32) * conv).astype(jnp.bfloat16)
    acc = acc + jnp.dot(y_c, wo_ref[GLA_V_W:, :], preferred_element_type=jnp.float32)

    xo = x_ref[...] + acc
    ms = jnp.mean(xo * xo, axis=-1, keepdims=True)
    out_ref[...] = xo * lax.rsqrt(ms + EPS) * fg_ref[...]


def _outproj(x2, o_f, o_b, sz, u, gc, gn, cw, cb, wo, fg, *, seq):
    m = x2.shape[0]
    tm = TM_OUT
    tiles_per_seq = seq // tm
    sub = tm // BF16_SUBLANES
    nsub = m // BF16_SUBLANES
    row = lambda w: pl.BlockSpec((tm, w), lambda i: (i, 0))
    const = lambda shp: pl.BlockSpec(shp, lambda i: (0, 0))
    halo_prev = pl.BlockSpec((BF16_SUBLANES, CONV_W), lambda i: (jnp.maximum(i * sub - 1, 0), 0))
    halo_next = pl.BlockSpec((BF16_SUBLANES, CONV_W),
                             lambda i: (jnp.minimum((i + 1) * sub, nsub - 1), 0))
    return pl.pallas_call(
        functools.partial(_out_kernel, tiles_per_seq=tiles_per_seq),
        grid=(m // tm,),
        in_specs=[row(D_MODEL), row(GLA_V_W), row(GLA_V_W), row(GLA_V_W),
                  row(CONV_W), halo_prev, halo_next, row(CONV_W),
                  const((1, GLA_DV)), const((3, CONV_W)), const((1, CONV_W)),
                  const((GLA_V_W + CONV_W, D_MODEL)), const((1, D_MODEL))],
        out_specs=row(D_MODEL),
        out_shape=jax.ShapeDtypeStruct((m, D_MODEL), jnp.float32),
        compiler_params=pltpu.CompilerParams(
            dimension_semantics=("arbitrary",), vmem_limit_bytes=VMEM_LIMIT),
        name="outproj",
    )(x2, o_f, o_b, sz, u, u, u, gc, gn, cw, cb, wo, fg)


def _regroup_w_in(w):
    q, k, v, za, lrf, lrb, bg, cg, hc, zc = jnp.split(
        w, [512, 1024, 2048, 3072, 3088, 3104, 4128, 5152, 6176], axis=1)
    pad = jnp.zeros((w.shape[0], LR_W - 2 * GATE_RANK), w.dtype)
    return jnp.concatenate([q, k, v, za, bg, cg, hc, zc, lrf, lrb, pad], axis=1).astype(jnp.bfloat16)


def _gate_weight(w_gk, first_row):
    full = jnp.zeros((LR_W, GLA_QK_W), jnp.float32)
    return lax.dynamic_update_slice(full, w_gk, (first_row, 0)).astype(jnp.bfloat16)


def kernel(x, norm_g, w_in, w_gk_f, b_gk_f, w_gk_b, b_gk_b, gla_norm_g, conv_w, conv_b, w_out, final_g):
    batch, seq, d = x.shape
    depth = w_in.shape[0]
    assert depth == 1 and d == D_MODEL and seq % TM_OUT == 0 and seq % TS_GLA == 0
    x2 = x.reshape(batch * seq, d)
    q, k, v, sz, u, gc, lr = _inproj(x2, norm_g[0][None, :], _regroup_w_in(w_in[0]))
    o_f, o_b = _gla(q, k, v, lr, _gate_weight(w_gk_f[0], 0), b_gk_f[0][None, :],
                    _gate_weight(w_gk_b[0], GATE_RANK), b_gk_b[0][None, :], batch=batch, seq=seq)
    out = _outproj(x2, o_f, o_b, sz, u, gc, gla_norm_g[0][None, :], conv_w[0], conv_b[0][None, :],
                   w_out[0].astype(jnp.bfloat16), final_g[None, :], seq=seq)
    return out.reshape(batch, seq, d)
```

```python
import functools

import jax
import jax.numpy as jnp
import numpy as np
from jax import lax
from jax.experimental import pallas as pl
from jax.experimental.pallas import tpu as pltpu

D_MODEL = 1024
GLA_HEADS = 4
GLA_DK = 128
GLA_DV = 256
GLA_QK_W = GLA_HEADS * GLA_DK
GLA_V_W = GLA_HEADS * GLA_DV
GATE_RANK = 16
GATE_NORM = 16.0
CHUNK = 64
CONV_W = 1024
EPS = 1e-6

LANES = 128
HALO = 16
LR_W = LANES

_OFF_Q = 0
_OFF_K = _OFF_Q + GLA_QK_W
_OFF_V = _OFF_K + GLA_QK_W
_OFF_ZA = _OFF_V + GLA_V_W
_OFF_BG = _OFF_ZA + GLA_V_W
_OFF_CG = _OFF_BG + CONV_W
_OFF_HC = _OFF_CG + CONV_W
_OFF_ZC = _OFF_HC + CONV_W
_OFF_LR = _OFF_ZC + CONV_W
W_ALL = _OFF_LR + LR_W

TM_PROJ = 512
TS_GLA = 512
TM_OUT = 512
VMEM_LIMIT = 56 * 1024 * 1024


def _silu(z):
    return z * (1.0 / (1.0 + jnp.exp(-z)))


def _inproj_kernel(x_ref, xp_ref, xn_ref, g_ref, w_ref, cw_ref, cb_ref,
                   q_ref, k_ref, v_ref, sz_ref, yc_ref, lr_ref, *, tiles_per_seq):
    tm = x_ref.shape[0]
    pos = pl.program_id(0) % tiles_per_seq

    def normed(x):
        ms = jnp.mean(x * x, axis=-1, keepdims=True)
        return (x * lax.rsqrt(ms + EPS) * g_ref[...]).astype(jnp.bfloat16)

    h = normed(x_ref[...])
    h_ext = jnp.concatenate([h, normed(xp_ref[...]), normed(xn_ref[...])], axis=0)

    def proj(lo, hi, lhs=h):
        return jnp.dot(lhs, w_ref[:, lo:hi], preferred_element_type=jnp.float32)

    u_ext = proj(_OFF_CG, _OFF_HC, h_ext) * proj(_OFF_HC, _OFF_ZC, h_ext)
    u = u_ext[:tm]
    prev_row = jnp.where(pos == 0, 0.0, u_ext[tm + HALO - 1:tm + HALO])
    next_row = jnp.where(pos == tiles_per_seq - 1, 0.0, u_ext[tm + HALO:tm + HALO + 1])
    t_i = lax.broadcasted_iota(jnp.int32, (tm, 1), 0)
    u_prev = jnp.where(t_i == 0, prev_row, pltpu.roll(u, 1, 0))
    u_next = jnp.where(t_i == tm - 1, next_row, pltpu.roll(u, tm - 1, 0))
    conv = cw_ref[0:1, :] * u_prev + cw_ref[1:2, :] * u + cw_ref[2:3, :] * u_next + cb_ref[...]
    yc_ref[...] = (proj(_OFF_BG, _OFF_CG) * conv * _silu(proj(_OFF_ZC, _OFF_LR))).astype(yc_ref.dtype)

    sz_ref[...] = _silu(proj(_OFF_ZA, _OFF_BG)).astype(sz_ref.dtype)
    q_ref[...] = (proj(_OFF_Q, _OFF_K) * (GLA_DK ** -0.5)).astype(q_ref.dtype)
    k_ref[...] = proj(_OFF_K, _OFF_V).astype(k_ref.dtype)
    v_ref[...] = proj(_OFF_V, _OFF_ZA).astype(v_ref.dtype)
    lr_ref[...] = proj(_OFF_LR, W_ALL)


def _inproj(x2, norm_g, w_all, conv_w, conv_b, *, seq):
    m = x2.shape[0]
    tm = TM_PROJ
    sub = tm // HALO
    nsub = m // HALO
    row = lambda w: pl.BlockSpec((tm, w), lambda i: (i, 0))
    const = lambda shp: pl.BlockSpec(shp, lambda i: (0, 0))
    halo_prev = pl.BlockSpec((HALO, D_MODEL), lambda i: (jnp.maximum(i * sub - 1, 0), 0))
    halo_next = pl.BlockSpec((HALO, D_MODEL), lambda i: (jnp.minimum((i + 1) * sub, nsub - 1), 0))
    bf = jnp.bfloat16
    return pl.pallas_call(
        functools.partial(_inproj_kernel, tiles_per_seq=seq // tm),
        grid=(m // tm,),
        in_specs=[
            row(D_MODEL), halo_prev, halo_next,
            const((1, D_MODEL)),
            pl.BlockSpec((D_MODEL, W_ALL), lambda i: (0, 0), pipeline_mode=pl.Buffered(1)),
            const((3, CONV_W)), const((1, CONV_W)),
        ],
        out_specs=[row(GLA_QK_W), row(GLA_QK_W), row(GLA_V_W), row(GLA_V_W),
                   row(CONV_W), row(LR_W)],
        out_shape=[
            jax.ShapeDtypeStruct((m, GLA_QK_W), bf),
            jax.ShapeDtypeStruct((m, GLA_QK_W), bf),
            jax.ShapeDtypeStruct((m, GLA_V_W), bf),
            jax.ShapeDtypeStruct((m, GLA_V_W), bf),
            jax.ShapeDtypeStruct((m, CONV_W), bf),
            jax.ShapeDtypeStruct((m, LR_W), jnp.float32),
        ],
        compiler_params=pltpu.CompilerParams(
            dimension_semantics=("arbitrary",), vmem_limit_bytes=VMEM_LIMIT),
        name="inproj",
    )(x2, x2, x2, norm_g, w_all, conv_w, conv_b)


def _pair_tri(rev):
    t = np.arange(CHUNK)
    tri = (t[None, :] >= t[:, None]) if rev else (t[None, :] <= t[:, None])
    zero = np.zeros_like(tri)
    keep = np.block([[tri, tri, zero, zero], [zero, zero, tri, tri]])
    return jnp.asarray(keep, dtype=jnp.bfloat16)


def _decay_terms(lr_ref, wg_ref, bg_ref, tri_ref):
    ts = lr_ref.shape[0]
    logits = jnp.dot(lr_ref[...].astype(jnp.bfloat16), wg_ref[...],
                     preferred_element_type=jnp.float32) + bg_ref[...]
    g = (jnp.minimum(logits, 0.0) - jnp.log(1.0 + jnp.exp(-jnp.abs(logits)))) * (1.0 / GATE_NORM)
    tri = tri_ref[...]
    g_hi = g.astype(jnp.bfloat16)
    g_lo = (g - g_hi.astype(jnp.float32)).astype(jnp.bfloat16)
    out = []
    for p in range(ts // (2 * CHUNK)):
        c0 = slice(2 * p * CHUNK, (2 * p + 1) * CHUNK)
        c1 = slice((2 * p + 1) * CHUNK, (2 * p + 2) * CHUNK)
        rhs = jnp.concatenate([g_hi[c0], g_lo[c0], g_hi[c1], g_lo[c1]], axis=0)
        out.append(jnp.dot(tri, rhs, preferred_element_type=jnp.float32))
    return jnp.concatenate(out, axis=0)


def _gla_kernel(qf_ref, kf_ref, vf_ref, lrf_ref, qb_ref, kb_ref, vb_ref, lrb_ref,
                wgf_ref, bgf_ref, wgb_ref, bgb_ref, trif_ref, trib_ref,
                of_ref, ob_ref, s_ref):
    ts = qf_ref.shape[0]
    nc = ts // CHUNK

    @pl.when(pl.program_id(1) == 0)
    def _():
        s_ref[...] = jnp.zeros_like(s_ref)

    ri = lax.broadcasted_iota(jnp.int32, (CHUNK, CHUNK), 0)
    ci = lax.broadcasted_iota(jnp.int32, (CHUNK, CHUNK), 1)
    dirs = (
        dict(q=qf_ref, k=kf_ref, v=vf_ref, o=of_ref, s=s_ref.at[0], amask=ci <= ri,
             ref_row=CHUNK // 2, last_row=CHUNK - 1, order=tuple(range(nc)),
             b=_decay_terms(lrf_ref, wgf_ref, bgf_ref, trif_ref)),
        dict(q=qb_ref, k=kb_ref, v=vb_ref, o=ob_ref, s=s_ref.at[1], amask=ci > ri,
             ref_row=CHUNK - 1 - CHUNK // 2, last_row=0, order=tuple(range(nc - 1, -1, -1)),
             b=_decay_terms(lrb_ref, wgb_ref, bgb_ref, trib_ref)),
    )

    streams = []
    for d in dirs:
        b_last_rows = d["b"].reshape(nc, CHUNK, GLA_QK_W)[:, d["last_row"], :]
        for h in range(GLA_HEADS):
            ksl = slice(h * GLA_DK, (h + 1) * GLA_DK)
            streams.append(dict(
                d=d, h=h, ksl=ksl, vsl=slice(h * GLA_DV, (h + 1) * GLA_DV),
                decay_t=jnp.transpose(jnp.exp(b_last_rows[:, ksl])),
            ))

    def intra(st, step):
        d = st["d"]
        c = d["order"][step]
        rows = slice(c * CHUNK, (c + 1) * CHUNK)
        b = d["b"][rows, st["ksl"]]
        b_mid = b[d["ref_row"]:d["ref_row"] + 1]
        b_last = b[d["last_row"]:d["last_row"] + 1]
        qe = d["q"][rows, st["ksl"]].astype(jnp.float32) * jnp.exp(b - b_mid)
        ke = d["k"][rows, st["ksl"]].astype(jnp.float32) * jnp.exp(b_mid - b)
        att = lax.dot_general(qe.astype(jnp.bfloat16), ke.astype(jnp.bfloat16),
                              (((1,), (1,)), ((), ())), preferred_element_type=jnp.float32)
        return dict(c=c, rows=rows, att=att,
                    q_in=(qe * jnp.exp(b_mid)).astype(jnp.bfloat16),
                    k_out=(ke * jnp.exp(b_last - b_mid)).astype(jnp.bfloat16))

    ahead = {id(st): intra(st, 0) for st in streams}
    for step in range(nc):
        for st in streams:
            d = st["d"]
            cur = ahead[id(st)]
            if step + 1 < nc:
                ahead[id(st)] = intra(st, step + 1)
            v = d["v"][cur["rows"], st["vsl"]]
            state = d["s"][st["h"]]
            att = jnp.where(d["amask"], cur["att"], 0.0).astype(jnp.bfloat16)
            lhs = jnp.concatenate([cur["q_in"], att], axis=1)
            rhs = jnp.concatenate([state.astype(jnp.bfloat16), v], axis=0)
            o = jnp.dot(lhs, rhs, preferred_element_type=jnp.float32)
            d["o"][cur["rows"], st["vsl"]] = o.astype(d["o"].dtype)
            upd = lax.dot_general(cur["k_out"], v, (((0,), (0,)), ((), ())),
                                  preferred_element_type=jnp.float32)
            decay = st["decay_t"][:, cur["c"]:cur["c"] + 1]
            d["s"][st["h"]] = decay * state + upd


def _gla(q, k, v, lr, wg_f, bg_f, wg_b, bg_b, *, batch, seq):
    ts = TS_GLA
    nblk = seq // ts
    fwd = lambda b, s: (b * nblk + s, 0)
    bwd = lambda b, s: (b * nblk + (nblk - 1 - s), 0)
    rows = lambda idx: [pl.BlockSpec((ts, w), idx) for w in (GLA_QK_W, GLA_QK_W, GLA_V_W, LR_W)]
    const = lambda shp: pl.BlockSpec(shp, lambda b, s: (0, 0))
    gate = [const((LR_W, GLA_QK_W)), const((1, GLA_QK_W))]
    o_shape = jax.ShapeDtypeStruct((batch * seq, GLA_V_W), jnp.bfloat16)
    return pl.pallas_call(
        _gla_kernel,
        grid=(batch, nblk),
        in_specs=rows(fwd) + rows(bwd) + gate + gate + [const((2 * CHUNK, 4 * CHUNK))] * 2,
        out_specs=[pl.BlockSpec((ts, GLA_V_W), fwd), pl.BlockSpec((ts, GLA_V_W), bwd)],
        out_shape=[o_shape, o_shape],
        scratch_shapes=[pltpu.VMEM((2, GLA_HEADS, GLA_DK, GLA_DV), jnp.float32)],
        compiler_params=pltpu.CompilerParams(
            dimension_semantics=("arbitrary", "arbitrary"), vmem_limit_bytes=VMEM_LIMIT),
        name="gla",
    )(q, k, v, lr, q, k, v, lr, wg_f, bg_f, wg_b, bg_b,
      _pair_tri(False), _pair_tri(True))


def _out_kernel(x_ref, of_ref, ob_ref, sz_ref, yc_ref, gn_ref, wo_ref, fg_ref, out_ref):
    acc = jnp.dot(yc_ref[...], wo_ref[GLA_V_W:, :], preferred_element_type=jnp.float32)
    for h in range(GLA_HEADS):
        sl = slice(h * GLA_DV, (h + 1) * GLA_DV)
        o_h = of_ref[:, sl].astype(jnp.float32) + ob_ref[:, sl].astype(jnp.float32)
        ms = jnp.mean(o_h * o_h, axis=-1, keepdims=True)
        y_h = o_h * lax.rsqrt(ms + EPS) * gn_ref[...] * sz_ref[:, sl].astype(jnp.float32)
        acc = acc + jnp.dot(y_h.astype(jnp.bfloat16), wo_ref[sl, :],
                            preferred_element_type=jnp.float32)
    xo = x_ref[...] + acc
    ms = jnp.mean(xo * xo, axis=-1, keepdims=True)
    out_ref[...] = xo * lax.rsqrt(ms + EPS) * fg_ref[...]


def _outproj(x2, o_f, o_b, sz, yc, gn, wo, fg):
    m = x2.shape[0]
    tm = TM_OUT
    row = lambda w: pl.BlockSpec((tm, w), lambda i: (i, 0))
    const = lambda shp: pl.BlockSpec(shp, lambda i: (0, 0))
    return pl.pallas_call(
        _out_kernel,
        grid=(m // tm,),
        in_specs=[row(D_MODEL), row(GLA_V_W), row(GLA_V_W), row(GLA_V_W), row(CONV_W),
                  const((1, GLA_DV)), const((GLA_V_W + CONV_W, D_MODEL)), const((1, D_MODEL))],
        out_specs=row(D_MODEL),
        out_shape=jax.ShapeDtypeStruct((m, D_MODEL), jnp.float32),
        compiler_params=pltpu.CompilerParams(
            dimension_semantics=("arbitrary",), vmem_limit_bytes=VMEM_LIMIT),
        name="outproj",
    )(x2, o_f, o_b, sz, yc, gn, wo, fg)


def _regroup_w_in(w):
    lr0 = GLA_QK_W * 2 + GLA_V_W * 2
    lr1 = lr0 + 2 * GATE_RANK
    pad = jnp.zeros((w.shape[0], LR_W - 2 * GATE_RANK), w.dtype)
    return jnp.concatenate([w[:, :lr0], w[:, lr1:], w[:, lr0:lr1], pad], axis=1).astype(jnp.bfloat16)


def _gate_weight(w_gk, first_row):
    full = jnp.zeros((LR_W, GLA_QK_W), jnp.float32)
    return lax.dynamic_update_slice(full, w_gk, (first_row, 0)).astype(jnp.bfloat16)


def kernel(x, norm_g, w_in, w_gk_f, b_gk_f, w_gk_b, b_gk_b, gla_norm_g, conv_w, conv_b, w_out, final_g):
    batch, seq, d = x.shape
    depth = w_in.shape[0]
    assert depth == 1 and d == D_MODEL
    assert seq % TM_PROJ == 0 and seq % TS_GLA == 0 and seq % TM_OUT == 0
    x2 = x.reshape(batch * seq, d)
    q, k, v, sz, yc, lr = _inproj(x2, norm_g[0][None, :], _regroup_w_in(w_in[0]),
                                  conv_w[0], conv_b[0][None, :], seq=seq)
    o_f, o_b = _gla(q, k, v, lr, _gate_weight(w_gk_f[0], 0), b_gk_f[0][None, :],
                    _gate_weight(w_gk_b[0], GATE_RANK), b_gk_b[0][None, :], batch=batch, seq=seq)
    out = _outproj(x2, o_f, o_b, sz, yc, gla_norm_g[0][None, :],
                   w_out[0].astype(jnp.bfloat16), final_g[None, :])
    return out.reshape(batch, seq, d)
```

```python
import functools

import jax
import jax.numpy as jnp
import numpy as np
from jax import lax
from jax.experimental import pallas as pl
from jax.experimental.pallas import tpu as pltpu

D_MODEL = 1024
GLA_HEADS = 4
GLA_DK = 128
GLA_DV = 256
GLA_QK_W = GLA_HEADS * GLA_DK
GLA_V_W = GLA_HEADS * GLA_DV
GATE_RANK = 16
GATE_NORM = 16.0
CHUNK = 64
CONV_W = 1024
EPS = 1e-6

LANES = 128
HALO = 16
LR_W = LANES

_OFF_Q = 0
_OFF_K = _OFF_Q + GLA_QK_W
_OFF_V = _OFF_K + GLA_QK_W
_OFF_ZA = _OFF_V + GLA_V_W
_OFF_BG = _OFF_ZA + GLA_V_W
_OFF_CG = _OFF_BG + CONV_W
_OFF_HC = _OFF_CG + CONV_W
_OFF_ZC = _OFF_HC + CONV_W
_OFF_LR = _OFF_ZC + CONV_W
W_ALL = _OFF_LR + LR_W

TM_PROJ = 512
TS_GLA = 512
TM_OUT = 512
VMEM_LIMIT = 56 * 1024 * 1024


def _silu(z):
    return z * (1.0 / (1.0 + jnp.exp(-z)))


def _inproj_kernel(x_ref, xp_ref, xn_ref, g_ref, w_ref, cw_ref, cb_ref,
                   q_ref, k_ref, v_ref, sz_ref, yc_ref, lr_ref, *, tiles_per_seq):
    tm = x_ref.shape[0]
    pos = pl.program_id(0) % tiles_per_seq

    def normed(x):
        ms = jnp.mean(x * x, axis=-1, keepdims=True)
        return (x * lax.rsqrt(ms + EPS) * g_ref[...]).astype(jnp.bfloat16)

    h = normed(x_ref[...])
    h_ext = jnp.concatenate([h, normed(xp_ref[...]), normed(xn_ref[...])], axis=0)

    def proj(lo, hi, lhs=h):
        return jnp.dot(lhs, w_ref[:, lo:hi], preferred_element_type=jnp.float32)

    u_ext = proj(_OFF_CG, _OFF_HC, h_ext) * proj(_OFF_HC, _OFF_ZC, h_ext)
    u = u_ext[:tm]
    prev_row = jnp.where(pos == 0, 0.0, u_ext[tm + HALO - 1:tm + HALO])
    next_row = jnp.where(pos == tiles_per_seq - 1, 0.0, u_ext[tm + HALO:tm + HALO + 1])
    t_i = lax.broadcasted_iota(jnp.int32, (tm, 1), 0)
    u_prev = jnp.where(t_i == 0, prev_row, pltpu.roll(u, 1, 0))
    u_next = jnp.where(t_i == tm - 1, next_row, pltpu.roll(u, tm - 1, 0))
    conv = cw_ref[0:1, :] * u_prev + cw_ref[1:2, :] * u + cw_ref[2:3, :] * u_next + cb_ref[...]
    yc_ref[...] = (proj(_OFF_BG, _OFF_CG) * conv * _silu(proj(_OFF_ZC, _OFF_LR))).astype(yc_ref.dtype)

    sz_ref[...] = _silu(proj(_OFF_ZA, _OFF_BG)).astype(sz_ref.dtype)
    q_ref[...] = (proj(_OFF_Q, _OFF_K) * (GLA_DK ** -0.5)).astype(q_ref.dtype)
    k_ref[...] = proj(_OFF_K, _OFF_V).astype(k_ref.dtype)
    v_ref[...] = proj(_OFF_V, _OFF_ZA).astype(v_ref.dtype)
    lr_ref[...] = proj(_OFF_LR, W_ALL)


def _inproj(x2, norm_g, w_all, conv_w, conv_b, *, seq):
    m = x2.shape[0]
    tm = TM_PROJ
    sub = tm // HALO
    nsub = m // HALO
    row = lambda w: pl.BlockSpec((tm, w), lambda i: (i, 0))
    const = lambda shp: pl.BlockSpec(shp, lambda i: (0, 0))
    halo_prev = pl.BlockSpec((HALO, D_MODEL), lambda i: (jnp.maximum(i * sub - 1, 0), 0))
    halo_next = pl.BlockSpec((HALO, D_MODEL), lambda i: (jnp.minimum((i + 1) * sub, nsub - 1), 0))
    bf = jnp.bfloat16
    return pl.pallas_call(
        functools.partial(_inproj_kernel, tiles_per_seq=seq // tm),
        grid=(m // tm,),
        in_specs=[
            row(D_MODEL), halo_prev, halo_next,
            const((1, D_MODEL)),
            pl.BlockSpec((D_MODEL, W_ALL), lambda i: (0, 0), pipeline_mode=pl.Buffered(1)),
            const((3, CONV_W)), const((1, CONV_W)),
        ],
        out_specs=[row(GLA_QK_W), row(GLA_QK_W), row(GLA_V_W), row(GLA_V_W),
                   row(CONV_W), row(LR_W)],
        out_shape=[
            jax.ShapeDtypeStruct((m, GLA_QK_W), bf),
            jax.ShapeDtypeStruct((m, GLA_QK_W), bf),
            jax.ShapeDtypeStruct((m, GLA_V_W), bf),
            jax.ShapeDtypeStruct((m, GLA_V_W), bf),
            jax.ShapeDtypeStruct((m, CONV_W), bf),
            jax.ShapeDtypeStruct((m, LR_W), jnp.float32),
        ],
        compiler_params=pltpu.CompilerParams(
            dimension_semantics=("arbitrary",), vmem_limit_bytes=VMEM_LIMIT),
        name="inproj",
    )(x2, x2, x2, norm_g, w_all, conv_w, conv_b)


def _pair_tri(rev):
    t = np.arange(CHUNK)
    tri = (t[None, :] >= t[:, None]) if rev else (t[None, :] <= t[:, None])
    zero = np.zeros_like(tri)
    keep = np.block([[tri, tri, zero, zero], [zero, zero, tri, tri]])
    return jnp.asarray(keep, dtype=jnp.bfloat16)


def _decay_terms(lr_ref, wg_ref, bg_ref, tri_ref):
    ts = lr_ref.shape[0]
    logits = jnp.dot(lr_ref[...].astype(jnp.bfloat16), wg_ref[...],
                     preferred_element_type=jnp.float32) + bg_ref[...]
    g = (jnp.minimum(logits, 0.0) - jnp.log(1.0 + jnp.exp(-jnp.abs(logits)))) * (1.0 / GATE_NORM)
    tri = tri_ref[...]
    g_hi = g.astype(jnp.bfloat16)
    g_lo = (g - g_hi.astype(jnp.float32)).astype(jnp.bfloat16)
    out = []
    for p in range(ts // (2 * CHUNK)):
        c0 = slice(2 * p * CHUNK, (2 * p + 1) * CHUNK)
        c1 = slice((2 * p + 1) * CHUNK, (2 * p + 2) * CHUNK)
        rhs = jnp.concatenate([g_hi[c0], g_lo[c0], g_hi[c1], g_lo[c1]], axis=0)
        out.append(jnp.dot(tri, rhs, preferred_element_type=jnp.float32))
    return jnp.concatenate(out, axis=0)


def _gla_kernel(qf_ref, kf_ref, vf_ref, lrf_ref, qb_ref, kb_ref, vb_ref, lrb_ref,
                wgf_ref, bgf_ref, wgb_ref, bgb_ref, trif_ref, trib_ref,
                of_ref, ob_ref, s_ref):
    ts = qf_ref.shape[0]
    nc = ts // CHUNK

    @pl.when(pl.program_id(1) == 0)
    def _():
        s_ref[...] = jnp.zeros_like(s_ref)

    ri = lax.broadcasted_iota(jnp.int32, (CHUNK, CHUNK), 0)
    ci = lax.broadcasted_iota(jnp.int32, (CHUNK, CHUNK), 1)
    dirs = (
        dict(q=qf_ref, k=kf_ref, v=vf_ref, o=of_ref, s=s_ref.at[0], amask=ci <= ri,
             ref_row=CHUNK // 2, last_row=CHUNK - 1, order=tuple(range(nc)),
             b=_decay_terms(lrf_ref, wgf_ref, bgf_ref, trif_ref)),
        dict(q=qb_ref, k=kb_ref, v=vb_ref, o=ob_ref, s=s_ref.at[1], amask=ci > ri,
             ref_row=CHUNK - 1 - CHUNK // 2, last_row=0, order=tuple(range(nc - 1, -1, -1)),
             b=_decay_terms(lrb_ref, wgb_ref, bgb_ref, trib_ref)),
    )

    streams = []
    for d in dirs:
        b_last_rows = d["b"].reshape(nc, CHUNK, GLA_QK_W)[:, d["last_row"], :]
        for h in range(GLA_HEADS):
            ksl = slice(h * GLA_DK, (h + 1) * GLA_DK)
            streams.append(dict(
                d=d, h=h, ksl=ksl, vsl=slice(h * GLA_DV, (h + 1) * GLA_DV),
                decay_t=jnp.transpose(jnp.exp(b_last_rows[:, ksl])),
            ))

    def intra(st, step):
        d = st["d"]
        c = d["order"][step]
        rows = slice(c * CHUNK, (c + 1) * CHUNK)
        b = d["b"][rows, st["ksl"]]
        b_mid = b[d["ref_row"]:d["ref_row"] + 1]
        b_last = b[d["last_row"]:d["last_row"] + 1]
        qe = d["q"][rows, st["ksl"]].astype(jnp.float32) * jnp.exp(b - b_mid)
        ke = d["k"][rows, st["ksl"]].astype(jnp.float32) * jnp.exp(b_mid - b)
        att = lax.dot_general(qe.astype(jnp.bfloat16), ke.astype(jnp.bfloat16),
                              (((1,), (1,)), ((), ())), preferred_element_type=jnp.float32)
        return dict(c=c, rows=rows, att=att,
                    q_in=(qe * jnp.exp(b_mid)).astype(jnp.bfloat16),
                    k_out=(ke * jnp.exp(b_last - b_mid)).astype(jnp.bfloat16))

    ahead = {id(st): intra(st, 0) for st in streams}
    for step in range(nc):
        for st in streams:
            d = st["d"]
            cur = ahead[id(st)]
            if step + 1 < nc:
                ahead[id(st)] = intra(st, step + 1)
            v = d["v"][cur["rows"], st["vsl"]]
            state = d["s"][st["h"]]
            att = jnp.where(d["amask"], cur["att"], 0.0).astype(jnp.bfloat16)
            lhs = jnp.concatenate([cur["q_in"], att], axis=1)
            rhs = jnp.concatenate([state.astype(jnp.bfloat16), v], axis=0)
            o = jnp.dot(lhs, rhs, preferred_element_type=jnp.float32)
            d["o"][cur["rows"], st["vsl"]] = o.astype(d["o"].dtype)
            upd = lax.dot_general(cur["k_out"], v, (((0,), (0,)), ((), ())),
                                  preferred_element_type=jnp.float32)
            decay = st["decay_t"][:, cur["c"]:cur["c"] + 1]
            d["s"][st["h"]] = decay * state + upd


def _gla(q, k, v, lr, wg_f, bg_f, wg_b, bg_b, *, batch, seq):
    ts = TS_GLA
    nblk = seq // ts
    fwd = lambda b, s: (b * nblk + s, 0)
    bwd = lambda b, s: (b * nblk + (nblk - 1 - s), 0)
    rows = lambda idx: [pl.BlockSpec((ts, w), idx) for w in (GLA_QK_W, GLA_QK_W, GLA_V_W, LR_W)]
    const = lambda shp: pl.BlockSpec(shp, lambda b, s: (0, 0))
    gate = [const((LR_W, GLA_QK_W)), const((1, GLA_QK_W))]
    o_shape = jax.ShapeDtypeStruct((batch * seq, GLA_V_W), jnp.bfloat16)
    return pl.pallas_call(
        _gla_kernel,
        grid=(batch, nblk),
        in_specs=rows(fwd) + rows(bwd) + gate + gate + [const((2 * CHUNK, 4 * CHUNK))] * 2,
        out_specs=[pl.BlockSpec((ts, GLA_V_W), fwd), pl.BlockSpec((ts, GLA_V_W), bwd)],
        out_shape=[o_shape, o_shape],
        scratch_shapes=[pltpu.VMEM((2, GLA_HEADS, GLA_DK, GLA_DV), jnp.float32)],
        compiler_params=pltpu.CompilerParams(
            dimension_semantics=("arbitrary", "arbitrary"), vmem_limit_bytes=VMEM_LIMIT),
        name="gla",
    )(q, k, v, lr, q, k, v, lr, wg_f, bg_f, wg_b, bg_b,
      _pair_tri(False), _pair_tri(True))


def _out_kernel(x_ref, of_ref, ob_ref, sz_ref, yc_ref, gn_ref, wo_ref, fg_ref, out_ref):
    acc = jnp.dot(yc_ref[...], wo_ref[GLA_V_W:, :], preferred_element_type=jnp.float32)
    for h in range(GLA_HEADS):
        sl = slice(h * GLA_DV, (h + 1) * GLA_DV)
        o_h = of_ref[:, sl].astype(jnp.float32) + ob_ref[:, sl].astype(jnp.float32)
        ms = jnp.mean(o_h * o_h, axis=-1, keepdims=True)
        y_h = o_h * lax.rsqrt(ms + EPS) * gn_ref[...] * sz_ref[:, sl].astype(jnp.float32)
        acc = acc + jnp.dot(y_h.astype(jnp.bfloat16), wo_ref[sl, :],
                            preferred_element_type=jnp.float32)
    xo = x_ref[...] + acc
    ms = jnp.mean(xo * xo, axis=-1, keepdims=True)
    out_ref[...] = xo * lax.rsqrt(ms + EPS) * fg_ref[...]


def _outproj(x2, o_f, o_b, sz, yc, gn, wo, fg):
    m = x2.shape[0]
    tm = TM_OUT
    row = lambda w: pl.BlockSpec((tm, w), lambda i: (i, 0))
    const = lambda shp: pl.BlockSpec(shp, lambda i: (0, 0))
    return pl.pallas_call(
        _out_kernel,
        grid=(m // tm,),
        in_specs=[row(D_MODEL), row(GLA_V_W), row(GLA_V_W), row(GLA_V_W), row(CONV_W),
                  const((1, GLA_DV)), const((GLA_V_W + CONV_W, D_MODEL)), const((1, D_MODEL))],
        out_specs=row(D_MODEL),
        out_shape=jax.ShapeDtypeStruct((m, D_MODEL), jnp.float32),
        compiler_params=pltpu.CompilerParams(
            dimension_semantics=("arbitrary",), vmem_limit_bytes=VMEM_LIMIT),
        name="outproj",
    )(x2, o_f, o_b, sz, yc, gn, wo, fg)


def _regroup_kernel(w_ref, o_ref):
    lr0 = _OFF_BG
    lr1 = lr0 + 2 * GATE_RANK
    rows = w_ref.shape[0]
    o_ref[:, :lr0] = w_ref[:, :lr0].astype(o_ref.dtype)
    o_ref[:, lr0:_OFF_LR] = w_ref[:, lr1:].astype(o_ref.dtype)
    o_ref[:, _OFF_LR:] = jnp.concatenate(
        [w_ref[:, lr0:lr1], jnp.zeros((rows, LR_W - 2 * GATE_RANK), w_ref.dtype)],
        axis=1).astype(o_ref.dtype)


def _regroup_w_in(w):
    k, n = w.shape
    rows = 128
    return pl.pallas_call(
        _regroup_kernel,
        grid=(k // rows,),
        in_specs=[pl.BlockSpec((rows, n), lambda i: (i, 0))],
        out_specs=pl.BlockSpec((rows, W_ALL), lambda i: (i, 0)),
        out_shape=jax.ShapeDtypeStruct((k, W_ALL), jnp.bfloat16),
        compiler_params=pltpu.CompilerParams(dimension_semantics=("arbitrary",)),
        name="regroup",
    )(w)


def _gate_weight(w_gk, first_row):
    full = jnp.zeros((LR_W, GLA_QK_W), jnp.float32)
    return lax.dynamic_update_slice(full, w_gk, (first_row, 0)).astype(jnp.bfloat16)


def kernel(x, norm_g, w_in, w_gk_f, b_gk_f, w_gk_b, b_gk_b, gla_norm_g, conv_w, conv_b, w_out, final_g):
    batch, seq, d = x.shape
    depth = w_in.shape[0]
    assert depth == 1 and d == D_MODEL
    assert seq % TM_PROJ == 0 and seq % TS_GLA == 0 and seq % TM_OUT == 0
    x2 = x.reshape(batch * seq, d)
    q, k, v, sz, yc, lr = _inproj(x2, norm_g[0][None, :], _regroup_w_in(w_in[0]),
                                  conv_w[0], conv_b[0][None, :], seq=seq)
    o_f, o_b = _gla(q, k, v, lr, _gate_weight(w_gk_f[0], 0), b_gk_f[0][None, :],
                    _gate_weight(w_gk_b[0], GATE_RANK), b_gk_b[0][None, :], batch=batch, seq=seq)
    out = _outproj(x2, o_f, o_b, sz, yc, gla_norm_g[0][None, :],
                   w_out[0].astype(jnp.bfloat16), final_g[None, :])
    return out.reshape(batch, seq, d)
```

```python
import functools

import jax
import jax.numpy as jnp
import numpy as np
from jax import lax
from jax.experimental import pallas as pl
from jax.experimental.pallas import tpu as pltpu

D_MODEL = 1024
GLA_HEADS = 4
GLA_DK = 128
GLA_DV = 256
GLA_QK_W = GLA_HEADS * GLA_DK
GLA_V_W = GLA_HEADS * GLA_DV
GATE_RANK = 16
GATE_NORM = 16.0
CHUNK = 64
CONV_W = 1024
EPS = 1e-6

LANES = 128
HALO = 16
LR_W = LANES
PIECE = 512

_OFF_Q = 0
_OFF_K = _OFF_Q + GLA_QK_W
_OFF_V = _OFF_K + GLA_QK_W
_OFF_ZA = _OFF_V + GLA_V_W
_OFF_BG = _OFF_ZA + GLA_V_W
_OFF_CG = _OFF_BG + CONV_W
_OFF_HC = _OFF_CG + CONV_W
_OFF_ZC = _OFF_HC + CONV_W
_OFF_LR = _OFF_ZC + CONV_W
W_ALL = _OFF_LR + LR_W

TM = 512
VMEM_LIMIT = 56 * 1024 * 1024


def _silu(z):
    return z * (1.0 / (1.0 + jnp.exp(-z)))


def _pair_tri(rev):
    t = np.arange(CHUNK)
    tri = (t[None, :] >= t[:, None]) if rev else (t[None, :] <= t[:, None])
    zero = np.zeros_like(tri)
    keep = np.block([[tri, tri, zero, zero], [zero, zero, tri, tri]])
    return jnp.asarray(keep, dtype=jnp.bfloat16)


def _gla_pieces(q_ref, k_ref, v_ref, lr_ref, wg_ref, bg_ref, tri_ref, o_ref, s_ref, *, rev):
    ts = q_ref.shape[0]
    nc = ts // CHUNK
    ri = lax.broadcasted_iota(jnp.int32, (CHUNK, CHUNK), 0)
    ci = lax.broadcasted_iota(jnp.int32, (CHUNK, CHUNK), 1)
    amask = (ci > ri) if rev else (ci <= ri)
    ref_row = CHUNK - 1 - CHUNK // 2 if rev else CHUNK // 2
    last_row = 0 if rev else CHUNK - 1
    order = tuple(range(nc - 1, -1, -1)) if rev else tuple(range(nc))
    ksl = [slice(h * GLA_DK, (h + 1) * GLA_DK) for h in range(GLA_HEADS)]
    vsl = [slice(h * GLA_DV, (h + 1) * GLA_DV) for h in range(GLA_HEADS)]
    ctx = {}

    def gates():
        logits = jnp.dot(lr_ref[...].astype(jnp.bfloat16), wg_ref[...],
                         preferred_element_type=jnp.float32) + bg_ref[...]
        ctx["g"] = ((jnp.minimum(logits, 0.0) - jnp.log(1.0 + jnp.exp(-jnp.abs(logits))))
                    * (1.0 / GATE_NORM))

    def cumsum():
        g = ctx.pop("g")
        tri = tri_ref[...]
        g_hi = g.astype(jnp.bfloat16)
        g_lo = (g - g_hi.astype(jnp.float32)).astype(jnp.bfloat16)
        out = []
        for p in range(ts // (2 * CHUNK)):
            c0 = slice(2 * p * CHUNK, (2 * p + 1) * CHUNK)
            c1 = slice((2 * p + 1) * CHUNK, (2 * p + 2) * CHUNK)
            rhs = jnp.concatenate([g_hi[c0], g_lo[c0], g_hi[c1], g_lo[c1]], axis=0)
            out.append(jnp.dot(tri, rhs, preferred_element_type=jnp.float32))
        b = jnp.concatenate(out, axis=0)
        ctx["b"] = b
        b_last_rows = b.reshape(nc, CHUNK, GLA_QK_W)[:, last_row, :]
        ctx["decay_t"] = [jnp.transpose(jnp.exp(b_last_rows[:, ksl[h]]))
                          for h in range(GLA_HEADS)]

    def intra(h, step):
        c = order[step]
        rows = slice(c * CHUNK, (c + 1) * CHUNK)
        b = ctx["b"][rows, ksl[h]]
        b_mid = b[ref_row:ref_row + 1]
        b_last = b[last_row:last_row + 1]
        qe = q_ref[rows, ksl[h]].astype(jnp.float32) * jnp.exp(b - b_mid)
        ke = k_ref[rows, ksl[h]].astype(jnp.float32) * jnp.exp(b_mid - b)
        att = lax.dot_general(qe.astype(jnp.bfloat16), ke.astype(jnp.bfloat16),
                              (((1,), (1,)), ((), ())), preferred_element_type=jnp.float32)
        return dict(c=c, rows=rows, att=att,
                    q_in=(qe * jnp.exp(b_mid)).astype(jnp.bfloat16),
                    k_out=(ke * jnp.exp(b_last - b_mid)).astype(jnp.bfloat16))

    def prep():
        ctx["ahead"] = [intra(h, 0) for h in range(GLA_HEADS)]

    def chunk_round(step):
        for h in range(GLA_HEADS):
            cur = ctx["ahead"][h]
            if step + 1 < nc:
                ctx["ahead"][h] = intra(h, step + 1)
            v = v_ref[cur["rows"], vsl[h]]
            state = s_ref[h]
            att = jnp.where(amask, cur["att"], 0.0).astype(jnp.bfloat16)
            lhs = jnp.concatenate([cur["q_in"], att], axis=1)
            rhs = jnp.concatenate([state.astype(jnp.bfloat16), v], axis=0)
            o = jnp.dot(lhs, rhs, preferred_element_type=jnp.float32)
            o_ref[cur["rows"], vsl[h]] = o.astype(o_ref.dtype)
            upd = lax.dot_general(cur["k_out"], v, (((0,), (0,)), ((), ())),
                                  preferred_element_type=jnp.float32)
            decay = ctx["decay_t"][h][:, cur["c"]:cur["c"] + 1]
            s_ref[h] = decay * state + upd

    return [gates, cumsum, prep] + [functools.partial(chunk_round, s) for s in range(nc)]


def _inproj_pieces(x_ref, xp_ref, xn_ref, g_ref, w_ref, cw_ref, cb_ref,
                   q_ref, k_ref, v_ref, sz_ref, yc_ref, lr_ref, keep, pos, tiles_per_seq):
    tm = x_ref.shape[0]
    ctx = {}

    def normed(x):
        ms = jnp.mean(x * x, axis=-1, keepdims=True)
        return (x * lax.rsqrt(ms + EPS) * g_ref[...]).astype(jnp.bfloat16)

    def norm():
        h = normed(x_ref[...])
        ctx["h"] = h
        ctx["h_ext"] = jnp.concatenate([h, normed(xp_ref[...]), normed(xn_ref[...])], axis=0)

    def proj(lo, width, lhs="h"):
        return jnp.dot(ctx[lhs], w_ref[:, lo:lo + width], preferred_element_type=jnp.float32)

    def conv(c0):
        cols = slice(c0, c0 + PIECE)
        u_ext = proj(_OFF_CG + c0, PIECE, "h_ext") * proj(_OFF_HC + c0, PIECE, "h_ext")
        u = u_ext[:tm]
        prev_row = jnp.where(pos == 0, 0.0, u_ext[tm + HALO - 1:tm + HALO])
        next_row = jnp.where(pos == tiles_per_seq - 1, 0.0, u_ext[tm + HALO:tm + HALO + 1])
        t_i = lax.broadcasted_iota(jnp.int32, (tm, 1), 0)
        u_prev = jnp.where(t_i == 0, prev_row, pltpu.roll(u, 1, 0))
        u_next = jnp.where(t_i == tm - 1, next_row, pltpu.roll(u, tm - 1, 0))
        ctx["conv"] = (cw_ref[0:1, cols] * u_prev + cw_ref[1:2, cols] * u
                       + cw_ref[2:3, cols] * u_next + cb_ref[:, cols])

    def gate_b(c0):
        ctx["conv"] = proj(_OFF_BG + c0, PIECE) * ctx["conv"]

    def gate_z(c0):
        yc_ref[:, c0:c0 + PIECE] = (ctx.pop("conv") * _silu(proj(_OFF_ZC + c0, PIECE))
                                    ).astype(yc_ref.dtype)

    def gate_a(c0):
        sz_ref[:, c0:c0 + PIECE] = _silu(proj(_OFF_ZA + c0, PIECE)).astype(sz_ref.dtype)

    def q():
        val = (proj(_OFF_Q, GLA_QK_W) * (GLA_DK ** -0.5)).astype(q_ref.dtype)
        q_ref[...] = val
        keep["q"][...] = val

    def k():
        val = proj(_OFF_K, GLA_QK_W).astype(k_ref.dtype)
        k_ref[...] = val
        keep["k"][...] = val

    def v(c0):
        val = proj(_OFF_V + c0, PIECE).astype(v_ref.dtype)
        v_ref[:, c0:c0 + PIECE] = val
        keep["v"][:, c0:c0 + PIECE] = val

    def lr():
        val = proj(_OFF_LR, LR_W)
        lr_ref[...] = val
        keep["lr"][...] = val

    part = functools.partial
    halves = range(0, CONV_W, PIECE)
    return dict(norm=norm, q=q, k=k, lr=lr,
                conv=[part(conv, c0) for c0 in halves],
                gate_b=[part(gate_b, c0) for c0 in halves],
                gate_z=[part(gate_z, c0) for c0 in halves],
                gate_a=[part(gate_a, c0) for c0 in halves],
                v=[part(v, c0) for c0 in halves])


def _stage_a_kernel(x_ref, xp_ref, xn_ref, g_ref, w_ref, cw_ref, cb_ref, wg_ref, bgk_ref, tri_ref,
                    q_ref, k_ref, v_ref, sz_ref, yc_ref, lr_ref, ob_ref,
                    qs_ref, ks_ref, vs_ref, lrs_ref, s_ref, *, tiles_per_seq, ntiles):
    t = pl.program_id(0)
    pos = tiles_per_seq - 1 - jnp.minimum(t, ntiles - 1) % tiles_per_seq

    @pl.when(t == 0)
    def _():
        for r in (qs_ref, ks_ref, vs_ref, lrs_ref):
            r[...] = jnp.zeros_like(r)

    @pl.when((t == 0) | ((t - 1) % tiles_per_seq == 0))
    def _():
        s_ref[...] = jnp.zeros_like(s_ref)

    old = (t + 1) % 2
    new = t % 2
    gla = _gla_pieces(qs_ref.at[old], ks_ref.at[old], vs_ref.at[old], lrs_ref.at[old],
                      wg_ref, bgk_ref, tri_ref, ob_ref, s_ref, rev=True)
    keep = dict(q=qs_ref.at[new], k=ks_ref.at[new], v=vs_ref.at[new], lr=lrs_ref.at[new])
    p = _inproj_pieces(x_ref, xp_ref, xn_ref, g_ref, w_ref, cw_ref, cb_ref,
                       q_ref, k_ref, v_ref, sz_ref, yc_ref, lr_ref, keep, pos, tiles_per_seq)
    gates, cumsum, prep, *rounds = gla
    order = [gates, p["norm"], p["conv"][0], cumsum, p["gate_b"][0], prep, p["gate_z"][0],
             rounds[0], p["conv"][1], rounds[1], p["gate_b"][1], rounds[2], p["gate_z"][1],
             rounds[3], p["gate_a"][0], rounds[4], p["gate_a"][1], rounds[5], p["q"],
             rounds[6], p["k"], rounds[7], p["v"][0], p["v"][1], p["lr"]]
    assert len(rounds) == 8
    for f in order:
        f()


def _stage_a(x2, norm_g, w_all, conv_w, conv_b, wg_b, bg_b, *, seq):
    m = x2.shape[0]
    tm = TM
    ntiles = m // tm
    tiles_per_seq = seq // tm
    sub = tm // HALO
    nsub = m // HALO

    def tile(t):
        t = jnp.minimum(t, ntiles - 1)
        return (t // tiles_per_seq) * tiles_per_seq + (tiles_per_seq - 1 - t % tiles_per_seq)

    cur = lambda t: (tile(t), 0)
    prev = lambda t: (tile(jnp.maximum(t - 1, 0)), 0)
    row = lambda w: pl.BlockSpec((tm, w), cur)
    const = lambda shp: pl.BlockSpec(shp, lambda t: (0, 0))
    halo_prev = pl.BlockSpec((HALO, D_MODEL), lambda t: (jnp.maximum(tile(t) * sub - 1, 0), 0))
    halo_next = pl.BlockSpec((HALO, D_MODEL),
                             lambda t: (jnp.minimum((tile(t) + 1) * sub, nsub - 1), 0))
    bf = jnp.bfloat16
    return pl.pallas_call(
        functools.partial(_stage_a_kernel, tiles_per_seq=tiles_per_seq, ntiles=ntiles),
        grid=(ntiles + 1,),
        in_specs=[
            row(D_MODEL), halo_prev, halo_next,
            const((1, D_MODEL)),
            pl.BlockSpec((D_MODEL, W_ALL), lambda t: (0, 0), pipeline_mode=pl.Buffered(1)),
            const((3, CONV_W)), const((1, CONV_W)),
            const((LR_W, GLA_QK_W)), const((1, GLA_QK_W)), const((2 * CHUNK, 4 * CHUNK)),
        ],
        out_specs=[row(GLA_QK_W), row(GLA_QK_W), row(GLA_V_W), row(GLA_V_W),
                   row(CONV_W), row(LR_W), pl.BlockSpec((tm, GLA_V_W), prev)],
        out_shape=[
            jax.ShapeDtypeStruct((m, GLA_QK_W), bf),
            jax.ShapeDtypeStruct((m, GLA_QK_W), bf),
            jax.ShapeDtypeStruct((m, GLA_V_W), bf),
            jax.ShapeDtypeStruct((m, GLA_V_W), bf),
            jax.ShapeDtypeStruct((m, CONV_W), bf),
            jax.ShapeDtypeStruct((m, LR_W), jnp.float32),
            jax.ShapeDtypeStruct((m, GLA_V_W), bf),
        ],
        scratch_shapes=[
            pltpu.VMEM((2, tm, GLA_QK_W), bf), pltpu.VMEM((2, tm, GLA_QK_W), bf),
            pltpu.VMEM((2, tm, GLA_V_W), bf), pltpu.VMEM((2, tm, LR_W), jnp.float32),
            pltpu.VMEM((GLA_HEADS, GLA_DK, GLA_DV), jnp.float32),
        ],
        compiler_params=pltpu.CompilerParams(
            dimension_semantics=("arbitrary",), vmem_limit_bytes=VMEM_LIMIT),
        name="stage_a",
    )(x2, x2, x2, norm_g, w_all, conv_w, conv_b, wg_b, bg_b, _pair_tri(True))


def _out_pieces(x_ref, of_ref, ob_ref, sz_ref, yc_ref, gn_ref, wo_ref, fg_ref, out_ref):
    n_piece = 2 * LANES
    starts = range(0, D_MODEL, n_piece)
    ctx = {"acc": {}, "y": []}

    def head(h):
        sl = slice(h * GLA_DV, (h + 1) * GLA_DV)
        o_h = of_ref[:, sl].astype(jnp.float32) + ob_ref[:, sl].astype(jnp.float32)
        ms = jnp.mean(o_h * o_h, axis=-1, keepdims=True)
        y_h = o_h * lax.rsqrt(ms + EPS) * gn_ref[...] * sz_ref[:, sl].astype(jnp.float32)
        ctx["y"].append(y_h.astype(jnp.bfloat16))

    def project_c(c0):
        ctx["acc"][c0] = jnp.dot(yc_ref[...], wo_ref[GLA_V_W:, c0:c0 + n_piece],
                                 preferred_element_type=jnp.float32)

    def project_a(c0):
        y_a = jnp.concatenate(ctx["y"], axis=1)
        ctx["acc"][c0] = ctx["acc"][c0] + jnp.dot(y_a, wo_ref[:GLA_V_W, c0:c0 + n_piece],
                                                  preferred_element_type=jnp.float32)

    def finish():
        xo = x_ref[...] + jnp.concatenate([ctx["acc"][c0] for c0 in starts], axis=1)
        ms = jnp.mean(xo * xo, axis=-1, keepdims=True)
        out_ref[...] = xo * lax.rsqrt(ms + EPS) * fg_ref[...]

    part = functools.partial
    return dict(head=[part(head, h) for h in range(GLA_HEADS)], finish=finish,
                project_c=[part(project_c, c0) for c0 in starts],
                project_a=[part(project_a, c0) for c0 in starts])


def _stage_b_kernel(q_ref, k_ref, v_ref, lr_ref, x_ref, ob_ref, sz_ref, yc_ref,
                    wg_ref, bgk_ref, tri_ref, gn_ref, wo_ref, fg_ref,
                    out_ref, of_ref, s_ref, *, tiles_per_seq):
    t = pl.program_id(0)

    @pl.when(t == 0)
    def _():
        of_ref[...] = jnp.zeros_like(of_ref)

    @pl.when(t % tiles_per_seq == 0)
    def _():
        s_ref[...] = jnp.zeros_like(s_ref)

    old = (t + 1) % 2
    new = t % 2
    gla = _gla_pieces(q_ref, k_ref, v_ref, lr_ref, wg_ref, bgk_ref, tri_ref,
                      of_ref.at[new], s_ref, rev=False)
    p = _out_pieces(x_ref, of_ref.at[old], ob_ref, sz_ref, yc_ref, gn_ref, wo_ref, fg_ref, out_ref)
    gates, cumsum, prep, *rounds = gla
    pc, pa = p["project_c"], p["project_a"]
    hd = p["head"]
    order = [gates, hd[0], hd[1], hd[2], hd[3], pc[0], cumsum, pa[0], prep, pc[1], rounds[0], pa[1],
             rounds[1], pc[2], rounds[2], pa[2], rounds[3], pc[3], rounds[4], pa[3], rounds[5],
             p["finish"], rounds[6], rounds[7]]
    assert len(rounds) == 8 and len(pc) == 4
    for f in order:
        f()


def _stage_b(q, k, v, lr, x2, o_b, sz, yc, wg_f, bg_f, gn, wo, fg, *, seq):
    m = x2.shape[0]
    tm = TM
    ntiles = m // tm
    cur = lambda t: (jnp.minimum(t, ntiles - 1), 0)
    prev = lambda t: (jnp.maximum(t - 1, 0), 0)
    const = lambda shp: pl.BlockSpec(shp, lambda t: (0, 0))
    return pl.pallas_call(
        functools.partial(_stage_b_kernel, tiles_per_seq=seq // tm),
        grid=(ntiles + 1,),
        in_specs=[pl.BlockSpec((tm, GLA_QK_W), cur), pl.BlockSpec((tm, GLA_QK_W), cur),
                  pl.BlockSpec((tm, GLA_V_W), cur), pl.BlockSpec((tm, LR_W), cur),
                  pl.BlockSpec((tm, D_MODEL), prev), pl.BlockSpec((tm, GLA_V_W), prev),
                  pl.BlockSpec((tm, GLA_V_W), prev), pl.BlockSpec((tm, CONV_W), prev),
                  const((LR_W, GLA_QK_W)), const((1, GLA_QK_W)), const((2 * CHUNK, 4 * CHUNK)),
                  const((1, GLA_DV)), const((GLA_V_W + CONV_W, D_MODEL)), const((1, D_MODEL))],
        out_specs=pl.BlockSpec((tm, D_MODEL), prev),
        out_shape=jax.ShapeDtypeStruct((m, D_MODEL), jnp.float32),
        scratch_shapes=[pltpu.VMEM((2, tm, GLA_V_W), jnp.bfloat16),
                        pltpu.VMEM((GLA_HEADS, GLA_DK, GLA_DV), jnp.float32)],
        compiler_params=pltpu.CompilerParams(
            dimension_semantics=("arbitrary",), vmem_limit_bytes=VMEM_LIMIT),
        name="stage_b",
    )(q, k, v, lr, x2, o_b, sz, yc, wg_f, bg_f, _pair_tri(False), gn, wo, fg)


def _regroup_kernel(w_ref, o_ref):
    lr0 = _OFF_BG
    lr1 = lr0 + 2 * GATE_RANK
    rows = w_ref.shape[0]
    o_ref[:, :lr0] = w_ref[:, :lr0].astype(o_ref.dtype)
    o_ref[:, lr0:_OFF_LR] = w_ref[:, lr1:].astype(o_ref.dtype)
    o_ref[:, _OFF_LR:] = jnp.concatenate(
        [w_ref[:, lr0:lr1], jnp.zeros((rows, LR_W - 2 * GATE_RANK), w_ref.dtype)],
        axis=1).astype(o_ref.dtype)


def _regroup_w_in(w):
    _, k, n = w.shape
    rows = 128
    return pl.pallas_call(
        _regroup_kernel,
        grid=(k // rows,),
        in_specs=[pl.BlockSpec((None, rows, n), lambda i: (0, i, 0))],
        out_specs=pl.BlockSpec((rows, W_ALL), lambda i: (i, 0)),
        out_shape=jax.ShapeDtypeStruct((k, W_ALL), jnp.bfloat16),
        compiler_params=pltpu.CompilerParams(dimension_semantics=("arbitrary",)),
        name="regroup",
    )(w)


def _gate_weight(w_gk, first_row):
    full = jnp.zeros((LR_W, GLA_QK_W), jnp.float32)
    return lax.dynamic_update_slice(full, w_gk, (first_row, 0)).astype(jnp.bfloat16)


def kernel(x, norm_g, w_in, w_gk_f, b_gk_f, w_gk_b, b_gk_b, gla_norm_g, conv_w, conv_b, w_out, final_g):
    batch, seq, d = x.shape
    depth = w_in.shape[0]
    assert depth == 1 and d == D_MODEL and seq % TM == 0
    x2 = x.reshape(batch * seq, d)
    q, k, v, sz, yc, lr, o_b = _stage_a(
        x2, norm_g[0][None, :], _regroup_w_in(w_in), conv_w[0], conv_b[0][None, :],
        _gate_weight(w_gk_b[0], GATE_RANK), b_gk_b[0][None, :], seq=seq)
    out = _stage_b(q, k, v, lr, x2, o_b, sz, yc,
                   _gate_weight(w_gk_f[0], 0), b_gk_f[0][None, :], gla_norm_g[0][None, :],
                   w_out[0].astype(jnp.bfloat16), final_g[None, :], seq=seq)
    return out.reshape(batch, seq, d)
```

```python
import functools

import jax
import jax.numpy as jnp
import numpy as np
from jax import lax
from jax.experimental import pallas as pl
from jax.experimental.pallas import tpu as pltpu

D_MODEL = 1024
GLA_HEADS = 4
GLA_DK = 128
GLA_DV = 256
GLA_QK_W = GLA_HEADS * GLA_DK
GLA_V_W = GLA_HEADS * GLA_DV
GATE_RANK = 16
GATE_NORM = 16.0
CHUNK = 64
CONV_W = 1024
EPS = 1e-6
LOG2_E = 1.4426950408889634

LANES = 128
HALO = 16
LR_W = LANES
ONES_COL = 2 * GATE_RANK

_OFF_Q = 0
_OFF_K = _OFF_Q + GLA_QK_W
_OFF_V = _OFF_K + GLA_QK_W
_OFF_ZA = _OFF_V + GLA_V_W
_OFF_BG = _OFF_ZA + GLA_V_W
_OFF_CG = _OFF_BG + CONV_W
_OFF_HC = _OFF_CG + CONV_W
_OFF_ZC = _OFF_HC + CONV_W
_OFF_LR = _OFF_ZC + CONV_W
W_ALL = _OFF_LR + LR_W

TM_PROJ = 512
TS_GLA = 512
TM_OUT = 512
VMEM_LIMIT = 56 * 1024 * 1024


def _silu(z):
    return z * (1.0 / (1.0 + jnp.exp(-z)))


def _inproj_kernel(x_ref, xp_ref, xn_ref, g_ref, w_ref, cw_ref, cb_ref,
                   q_ref, k_ref, v_ref, sz_ref, yc_ref, lr_ref, *, tiles_per_seq):
    tm = x_ref.shape[0]
    pos = pl.program_id(0) % tiles_per_seq

    def normed(x):
        ms = jnp.mean(x * x, axis=-1, keepdims=True)
        return (x * lax.rsqrt(ms + EPS) * g_ref[...]).astype(jnp.bfloat16)

    h = normed(x_ref[...])
    h_ext = jnp.concatenate([h, normed(xp_ref[...]), normed(xn_ref[...])], axis=0)

    def proj(lo, hi, lhs=h):
        return jnp.dot(lhs, w_ref[:, lo:hi], preferred_element_type=jnp.float32)

    u_ext = proj(_OFF_CG, _OFF_HC, h_ext) * proj(_OFF_HC, _OFF_ZC, h_ext)
    u = u_ext[:tm]
    prev_row = jnp.where(pos == 0, 0.0, u_ext[tm + HALO - 1:tm + HALO])
    next_row = jnp.where(pos == tiles_per_seq - 1, 0.0, u_ext[tm + HALO:tm + HALO + 1])
    t_i = lax.broadcasted_iota(jnp.int32, (tm, 1), 0)
    u_prev = jnp.where(t_i == 0, prev_row, pltpu.roll(u, 1, 0))
    u_next = jnp.where(t_i == tm - 1, next_row, pltpu.roll(u, tm - 1, 0))
    conv = cw_ref[0:1, :] * u_prev + cw_ref[1:2, :] * u + cw_ref[2:3, :] * u_next + cb_ref[...]
    yc_ref[...] = (proj(_OFF_BG, _OFF_CG) * conv * _silu(proj(_OFF_ZC, _OFF_LR))).astype(yc_ref.dtype)

    sz_ref[...] = _silu(proj(_OFF_ZA, _OFF_BG)).astype(sz_ref.dtype)
    q_ref[...] = (proj(_OFF_Q, _OFF_K) * (GLA_DK ** -0.5)).astype(q_ref.dtype)
    k_ref[...] = proj(_OFF_K, _OFF_V).astype(k_ref.dtype)
    v_ref[...] = proj(_OFF_V, _OFF_ZA).astype(v_ref.dtype)
    col = lax.broadcasted_iota(jnp.int32, (1, LR_W), 1)
    ones = jnp.where((col == ONES_COL) | (col == ONES_COL + 1), 1.0, 0.0)
    lr_ref[...] = proj(_OFF_LR, W_ALL) + ones


def _inproj(x2, norm_g, w_all, conv_w, conv_b, *, seq):
    m = x2.shape[0]
    tm = TM_PROJ
    sub = tm // HALO
    nsub = m // HALO
    row = lambda w: pl.BlockSpec((tm, w), lambda i: (i, 0))
    const = lambda shp: pl.BlockSpec(shp, lambda i: (0, 0))
    halo_prev = pl.BlockSpec((HALO, D_MODEL), lambda i: (jnp.maximum(i * sub - 1, 0), 0))
    halo_next = pl.BlockSpec((HALO, D_MODEL), lambda i: (jnp.minimum((i + 1) * sub, nsub - 1), 0))
    bf = jnp.bfloat16
    return pl.pallas_call(
        functools.partial(_inproj_kernel, tiles_per_seq=seq // tm),
        grid=(m // tm,),
        in_specs=[
            row(D_MODEL), halo_prev, halo_next,
            const((1, D_MODEL)),
            pl.BlockSpec((D_MODEL, W_ALL), lambda i: (0, 0), pipeline_mode=pl.Buffered(1)),
            const((3, CONV_W)), const((1, CONV_W)),
        ],
        out_specs=[row(GLA_QK_W), row(GLA_QK_W), row(GLA_V_W), row(GLA_V_W),
                   row(CONV_W), row(LR_W)],
        out_shape=[
            jax.ShapeDtypeStruct((m, GLA_QK_W), bf),
            jax.ShapeDtypeStruct((m, GLA_QK_W), bf),
            jax.ShapeDtypeStruct((m, GLA_V_W), bf),
            jax.ShapeDtypeStruct((m, GLA_V_W), bf),
            jax.ShapeDtypeStruct((m, CONV_W), bf),
            jax.ShapeDtypeStruct((m, LR_W), jnp.float32),
        ],
        compiler_params=pltpu.CompilerParams(
            dimension_semantics=("arbitrary",), vmem_limit_bytes=VMEM_LIMIT),
        name="inproj",
    )(x2, x2, x2, norm_g, w_all, conv_w, conv_b)


def _pair_tri(rev):
    t = np.arange(CHUNK)
    tri = (t[None, :] >= t[:, None]) if rev else (t[None, :] <= t[:, None])
    zero = np.zeros_like(tri)
    keep = np.block([[tri, tri, zero, zero], [zero, zero, tri, tri]])
    return jnp.asarray(keep / GATE_NORM, dtype=jnp.bfloat16)


def _decay_terms(lr_ref, wg_ref, tri_ref):
    ts = lr_ref.shape[0]
    x = jnp.dot(lr_ref[...].astype(jnp.bfloat16), wg_ref[...], preferred_element_type=jnp.float32)
    g = jnp.minimum(x, 0.0) - jnp.log2(1.0 + jnp.exp2(-jnp.abs(x)))
    tri = tri_ref[...]
    g_top = lax.bitcast_convert_type(
        lax.bitcast_convert_type(g, jnp.uint32) & jnp.uint32(0xFFFF0000), jnp.float32)
    g_hi = g_top.astype(jnp.bfloat16)
    g_lo = (g - g_top).astype(jnp.bfloat16)
    out = []
    for p in range(ts // (2 * CHUNK)):
        c0 = slice(2 * p * CHUNK, (2 * p + 1) * CHUNK)
        c1 = slice((2 * p + 1) * CHUNK, (2 * p + 2) * CHUNK)
        rhs = jnp.concatenate([g_hi[c0], g_lo[c0], g_hi[c1], g_lo[c1]], axis=0)
        out.append(jnp.dot(tri, rhs, preferred_element_type=jnp.float32))
    return jnp.concatenate(out, axis=0)


def _gla_kernel(qf_ref, kf_ref, vf_ref, lrf_ref, qb_ref, kb_ref, vb_ref, lrb_ref,
                wgf_ref, wgb_ref, trif_ref, trib_ref, of_ref, ob_ref, s_ref):
    ts = qf_ref.shape[0]
    nc = ts // CHUNK
    npair = nc // 2
    pair = 2 * CHUNK

    @pl.when(pl.program_id(1) == 0)
    def _():
        s_ref[...] = jnp.zeros_like(s_ref)

    ri = lax.broadcasted_iota(jnp.int32, (CHUNK, pair), 0)
    li = lax.broadcasted_iota(jnp.int32, (CHUNK, pair), 1)
    lo_half = li < CHUNK
    dirs = (
        dict(q=qf_ref, k=kf_ref, v=vf_ref, o=of_ref, s=s_ref.at[0], a=0, b_=1,
             mask_a=lo_half & (li <= ri), mask_b=lo_half | (li - CHUNK <= ri),
             ref_row=CHUNK // 2, last_row=CHUNK - 1, order=tuple(range(npair)),
             b=_decay_terms(lrf_ref, wgf_ref, trif_ref)),
        dict(q=qb_ref, k=kb_ref, v=vb_ref, o=ob_ref, s=s_ref.at[1], a=1, b_=0,
             mask_a=(~lo_half) & (li - CHUNK > ri), mask_b=(~lo_half) | (li > ri),
             ref_row=CHUNK - 1 - CHUNK // 2, last_row=0, order=tuple(range(npair - 1, -1, -1)),
             b=_decay_terms(lrb_ref, wgb_ref, trib_ref)),
    )

    streams = []
    for d in dirs:
        b_last_rows = d["b"].reshape(nc, CHUNK, GLA_QK_W)[:, d["last_row"], :]
        for h in range(GLA_HEADS):
            ksl = slice(h * GLA_DK, (h + 1) * GLA_DK)
            streams.append(dict(
                d=d, h=h, ksl=ksl, vsl=slice(h * GLA_DV, (h + 1) * GLA_DV),
                decay_t=jnp.transpose(jnp.exp2(b_last_rows[:, ksl])),
            ))

    bf = jnp.bfloat16
    zeros_k = jnp.zeros((CHUNK, GLA_DK), bf)

    def intra(st, step):
        d = st["d"]
        p = d["order"][step]
        rows = slice(p * pair, (p + 1) * pair)
        ia, ib = d["a"], d["b_"]
        b = d["b"][rows, st["ksl"]].reshape(2, CHUNK, GLA_DK)
        b_mid = b[:, d["ref_row"]:d["ref_row"] + 1, :]
        tot = b[:, d["last_row"]:d["last_row"] + 1, :]
        qe = (d["q"][rows, st["ksl"]].astype(jnp.float32).reshape(2, CHUNK, GLA_DK)
              * jnp.exp2(b - b_mid))
        ke = (d["k"][rows, st["ksl"]].astype(jnp.float32).reshape(2, CHUNK, GLA_DK)
              * jnp.exp2(b_mid - b))
        f_q = jnp.exp2(b_mid)
        f_k = jnp.exp2(tot - b_mid)
        q_in_a = (qe[ia] * f_q[ia]).astype(bf)
        q_in_b = (qe[ib] * f_q[ib]).astype(bf)
        q_in_b2 = (qe[ib] * (f_q[ib] * jnp.exp2(tot[ia]))).astype(bf)
        k_out_a = (ke[ia] * f_k[ia]).astype(bf)
        k_out_a2 = (ke[ia] * (f_k[ia] * jnp.exp2(tot[ib]))).astype(bf)
        k_out_b = (ke[ib] * f_k[ib]).astype(bf)
        ke_a = ke[ia].astype(bf)
        ke_b = ke[ib].astype(bf)
        nt = (((1,), (1,)), ((), ()))
        half = lambda x, slot: jnp.concatenate([x, zeros_k] if slot == 0 else [zeros_k, x], axis=0)
        att_a = lax.dot_general(qe[ia].astype(bf), half(ke_a, ia), nt,
                                preferred_element_type=jnp.float32)
        rhs_b = [None, None]
        rhs_b[ia] = jnp.concatenate([k_out_a, zeros_k], axis=1)
        rhs_b[ib] = jnp.concatenate([zeros_k, ke_b], axis=1)
        att_b = lax.dot_general(jnp.concatenate([q_in_b, qe[ib].astype(bf)], axis=1),
                                jnp.concatenate(rhs_b, axis=0), nt,
                                preferred_element_type=jnp.float32)
        q_rows = [None, None]
        q_rows[ia], q_rows[ib] = q_in_a, q_in_b2
        k_rows = [None, None]
        k_rows[ia], k_rows[ib] = k_out_a2, k_out_b
        return dict(p=p, rows=rows, att=(att_a, att_b), q_rows=q_rows,
                    k_pair=jnp.concatenate(k_rows, axis=0))

    ahead = {id(st): intra(st, 0) for st in streams}
    for step in range(npair):
        for st in streams:
            d = st["d"]
            cur = ahead[id(st)]
            if step + 1 < npair:
                ahead[id(st)] = intra(st, step + 1)
            ia, ib = d["a"], d["b_"]
            v = d["v"][cur["rows"], st["vsl"]]
            state = d["s"][st["h"]]
            att = [None, None]
            att[ia] = jnp.where(d["mask_a"], cur["att"][0], 0.0).astype(bf)
            att[ib] = jnp.where(d["mask_b"], cur["att"][1], 0.0).astype(bf)
            lhs = jnp.concatenate([jnp.concatenate([cur["q_rows"][0], att[0]], axis=1),
                                   jnp.concatenate([cur["q_rows"][1], att[1]], axis=1)], axis=0)
            rhs = jnp.concatenate([state.astype(bf), v], axis=0)
            o = jnp.dot(lhs, rhs, preferred_element_type=jnp.float32)
            d["o"][cur["rows"], st["vsl"]] = o.astype(d["o"].dtype)
            upd = lax.dot_general(cur["k_pair"], v, (((0,), (0,)), ((), ())),
                                  preferred_element_type=jnp.float32)
            c0 = 2 * cur["p"]
            decay = st["decay_t"][:, c0:c0 + 1] * st["decay_t"][:, c0 + 1:c0 + 2]
            d["s"][st["h"]] = decay * state + upd


def _gla(q, k, v, lr, wg_f, wg_b, *, batch, seq):
    ts = TS_GLA
    nblk = seq // ts
    fwd = lambda b, s: (b * nblk + s, 0)
    bwd = lambda b, s: (b * nblk + (nblk - 1 - s), 0)
    rows = lambda idx: [pl.BlockSpec((ts, w), idx) for w in (GLA_QK_W, GLA_QK_W, GLA_V_W, LR_W)]
    const = lambda shp: pl.BlockSpec(shp, lambda b, s: (0, 0))
    gate = [const((LR_W, GLA_QK_W))]
    o_shape = jax.ShapeDtypeStruct((batch * seq, GLA_V_W), jnp.bfloat16)
    return pl.pallas_call(
        _gla_kernel,
        grid=(batch, nblk),
        in_specs=rows(fwd) + rows(bwd) + gate + gate + [const((2 * CHUNK, 4 * CHUNK))] * 2,
        out_specs=[pl.BlockSpec((ts, GLA_V_W), fwd), pl.BlockSpec((ts, GLA_V_W), bwd)],
        out_shape=[o_shape, o_shape],
        scratch_shapes=[pltpu.VMEM((2, GLA_HEADS, GLA_DK, GLA_DV), jnp.float32)],
        compiler_params=pltpu.CompilerParams(
            dimension_semantics=("arbitrary", "arbitrary"), vmem_limit_bytes=VMEM_LIMIT),
        name="gla",
    )(q, k, v, lr, q, k, v, lr, wg_f, wg_b, _pair_tri(False), _pair_tri(True))


def _out_kernel(x_ref, of_ref, ob_ref, sz_ref, yc_ref, gn_ref, wo_ref, fg_ref, out_ref):
    acc = jnp.dot(yc_ref[...], wo_ref[GLA_V_W:, :], preferred_element_type=jnp.float32)
    for h in range(GLA_HEADS):
        sl = slice(h * GLA_DV, (h + 1) * GLA_DV)
        o_h = of_ref[:, sl].astype(jnp.float32) + ob_ref[:, sl].astype(jnp.float32)
        ms = jnp.mean(o_h * o_h, axis=-1, keepdims=True)
        y_h = o_h * lax.rsqrt(ms + EPS) * gn_ref[...] * sz_ref[:, sl].astype(jnp.float32)
        acc = acc + jnp.dot(y_h.astype(jnp.bfloat16), wo_ref[sl, :],
                            preferred_element_type=jnp.float32)
    xo = x_ref[...] + acc
    ms = jnp.mean(xo * xo, axis=-1, keepdims=True)
    out_ref[...] = xo * lax.rsqrt(ms + EPS) * fg_ref[...]


def _outproj(x2, o_f, o_b, sz, yc, gn, wo, fg):
    m = x2.shape[0]
    tm = TM_OUT
    row = lambda w: pl.BlockSpec((tm, w), lambda i: (i, 0))
    const = lambda shp: pl.BlockSpec(shp, lambda i: (0, 0))
    return pl.pallas_call(
        _out_kernel,
        grid=(m // tm,),
        in_specs=[row(D_MODEL), row(GLA_V_W), row(GLA_V_W), row(GLA_V_W), row(CONV_W),
                  const((1, GLA_DV)), const((GLA_V_W + CONV_W, D_MODEL)), const((1, D_MODEL))],
        out_specs=row(D_MODEL),
        out_shape=jax.ShapeDtypeStruct((m, D_MODEL), jnp.float32),
        compiler_params=pltpu.CompilerParams(
            dimension_semantics=("arbitrary",), vmem_limit_bytes=VMEM_LIMIT),
        name="outproj",
    )(x2, o_f, o_b, sz, yc, gn, wo, fg)


def _regroup_kernel(w_ref, o_ref):
    lr0 = _OFF_BG
    lr1 = lr0 + 2 * GATE_RANK
    rows = w_ref.shape[0]
    o_ref[:, :lr0] = w_ref[:, :lr0].astype(o_ref.dtype)
    o_ref[:, lr0:_OFF_LR] = w_ref[:, lr1:].astype(o_ref.dtype)
    o_ref[:, _OFF_LR:] = jnp.concatenate(
        [w_ref[:, lr0:lr1], jnp.zeros((rows, LR_W - 2 * GATE_RANK), w_ref.dtype)],
        axis=1).astype(o_ref.dtype)


def _regroup_w_in(w):
    _, k, n = w.shape
    rows = 128
    return pl.pallas_call(
        _regroup_kernel,
        grid=(k // rows,),
        in_specs=[pl.BlockSpec((None, rows, n), lambda i: (0, i, 0))],
        out_specs=pl.BlockSpec((rows, W_ALL), lambda i: (i, 0)),
        out_shape=jax.ShapeDtypeStruct((k, W_ALL), jnp.bfloat16),
        compiler_params=pltpu.CompilerParams(dimension_semantics=("arbitrary",)),
        name="regroup",
    )(w)


def _gate_weight(w_gk, b_gk, first_row):
    bias = b_gk * LOG2_E
    bias_hi = bias.astype(jnp.bfloat16)
    bias_lo = (bias - bias_hi.astype(jnp.float32)).astype(jnp.bfloat16)
    full = jnp.zeros((LR_W, GLA_QK_W), jnp.bfloat16)
    full = lax.dynamic_update_slice(full, (w_gk * LOG2_E).astype(jnp.bfloat16), (first_row, 0))
    return lax.dynamic_update_slice(full, jnp.stack([bias_hi, bias_lo]), (ONES_COL, 0))


def kernel(x, norm_g, w_in, w_gk_f, b_gk_f, w_gk_b, b_gk_b, gla_norm_g, conv_w, conv_b, w_out, final_g):
    batch, seq, d = x.shape
    depth = w_in.shape[0]
    assert depth == 1 and d == D_MODEL
    assert seq % TM_PROJ == 0 and seq % TS_GLA == 0 and seq % TM_OUT == 0
    x2 = x.reshape(batch * seq, d)
    q, k, v, sz, yc, lr = _inproj(x2, norm_g[0][None, :], _regroup_w_in(w_in),
                                  conv_w[0], conv_b[0][None, :], seq=seq)
    o_f, o_b = _gla(q, k, v, lr, _gate_weight(w_gk_f[0], b_gk_f[0], 0),
                    _gate_weight(w_gk_b[0], b_gk_b[0], GATE_RANK), batch=batch, seq=seq)
    out = _outproj(x2, o_f, o_b, sz, yc, gla_norm_g[0][None, :],
                   w_out[0].astype(jnp.bfloat16), final_g[None, :])
    return out.reshape(batch, seq, d)
```

```python
import functools

import jax
import jax.numpy as jnp
import numpy as np
from jax import lax
from jax.experimental import pallas as pl
from jax.experimental.pallas import tpu as pltpu

D_MODEL = 1024
GLA_HEADS = 4
GLA_DK = 128
GLA_DV = 256
GLA_QK_W = GLA_HEADS * GLA_DK
GLA_V_W = GLA_HEADS * GLA_DV
GATE_RANK = 16
GATE_NORM = 16.0
CHUNK = 64
CONV_W = 1024
EPS = 1e-6
LOG2_E = 1.4426950408889634

LANES = 128
HALO = 16
LR_W = LANES
ONES_COL = 2 * GATE_RANK

_OFF_Q = 0
_OFF_K = _OFF_Q + GLA_QK_W
_OFF_V = _OFF_K + GLA_QK_W
_OFF_ZA = _OFF_V + GLA_V_W
_OFF_BG = _OFF_ZA + GLA_V_W
_OFF_CG = _OFF_BG + CONV_W
_OFF_HC = _OFF_CG + CONV_W
_OFF_ZC = _OFF_HC + CONV_W
_OFF_LR = _OFF_ZC + CONV_W
REGROUP_TILE = 2 * LANES
REGROUP_HALO = 2 * GATE_RANK
W_ALL = _OFF_LR + REGROUP_TILE

TM_PROJ = 512
TS_GLA = 512
TM_OUT = 512
VMEM_LIMIT = 56 * 1024 * 1024


def _silu(z):
    return z * (1.0 / (1.0 + jnp.exp(-z)))


def _inproj_kernel(x_ref, xp_ref, xn_ref, g_ref, w_ref, cw_ref, cb_ref,
                   q_ref, k_ref, v_ref, sz_ref, yc_ref, lr_ref, *, tiles_per_seq):
    tm = x_ref.shape[0]
    pos = pl.program_id(0) % tiles_per_seq

    def normed(x):
        ms = jnp.mean(x * x, axis=-1, keepdims=True)
        return (x * lax.rsqrt(ms + EPS) * g_ref[...]).astype(jnp.bfloat16)

    h = normed(x_ref[...])
    h_ext = jnp.concatenate([h, normed(xp_ref[...]), normed(xn_ref[...])], axis=0)

    def proj(lo, hi, lhs=h):
        return jnp.dot(lhs, w_ref[:, lo:hi], preferred_element_type=jnp.float32)

    u_ext = proj(_OFF_CG, _OFF_HC, h_ext) * proj(_OFF_HC, _OFF_ZC, h_ext)
    u = u_ext[:tm]
    prev_row = jnp.where(pos == 0, 0.0, u_ext[tm + HALO - 1:tm + HALO])
    next_row = jnp.where(pos == tiles_per_seq - 1, 0.0, u_ext[tm + HALO:tm + HALO + 1])
    t_i = lax.broadcasted_iota(jnp.int32, (tm, 1), 0)
    u_prev = jnp.where(t_i == 0, prev_row, pltpu.roll(u, 1, 0))
    u_next = jnp.where(t_i == tm - 1, next_row, pltpu.roll(u, tm - 1, 0))
    conv = cw_ref[0:1, :] * u_prev + cw_ref[1:2, :] * u + cw_ref[2:3, :] * u_next + cb_ref[...]
    yc_ref[...] = (proj(_OFF_BG, _OFF_CG) * conv * _silu(proj(_OFF_ZC, _OFF_LR))).astype(yc_ref.dtype)

    sz_ref[...] = _silu(proj(_OFF_ZA, _OFF_BG)).astype(sz_ref.dtype)
    q_ref[...] = (proj(_OFF_Q, _OFF_K) * (GLA_DK ** -0.5)).astype(q_ref.dtype)
    k_ref[...] = proj(_OFF_K, _OFF_V).astype(k_ref.dtype)
    v_ref[...] = proj(_OFF_V, _OFF_ZA).astype(v_ref.dtype)
    col = lax.broadcasted_iota(jnp.int32, (1, LR_W), 1)
    ones = jnp.where((col == ONES_COL) | (col == ONES_COL + 1), 1.0, 0.0)
    lr_ref[...] = proj(_OFF_LR, _OFF_LR + LR_W) + ones


def _inproj(x2, norm_g, w_all, conv_w, conv_b, *, seq):
    m = x2.shape[0]
    tm = TM_PROJ
    sub = tm // HALO
    nsub = m // HALO
    row = lambda w: pl.BlockSpec((tm, w), lambda i: (i, 0))
    const = lambda shp: pl.BlockSpec(shp, lambda i: (0, 0))
    halo_prev = pl.BlockSpec((HALO, D_MODEL), lambda i: (jnp.maximum(i * sub - 1, 0), 0))
    halo_next = pl.BlockSpec((HALO, D_MODEL), lambda i: (jnp.minimum((i + 1) * sub, nsub - 1), 0))
    bf = jnp.bfloat16
    return pl.pallas_call(
        functools.partial(_inproj_kernel, tiles_per_seq=seq // tm),
        grid=(m // tm,),
        in_specs=[
            row(D_MODEL), halo_prev, halo_next,
            const((1, D_MODEL)),
            pl.BlockSpec((D_MODEL, W_ALL), lambda i: (0, 0), pipeline_mode=pl.Buffered(1)),
            const((3, CONV_W)), const((1, CONV_W)),
        ],
        out_specs=[row(GLA_QK_W), row(GLA_QK_W), row(GLA_V_W), row(GLA_V_W),
                   row(CONV_W), row(LR_W)],
        out_shape=[
            jax.ShapeDtypeStruct((m, GLA_QK_W), bf),
            jax.ShapeDtypeStruct((m, GLA_QK_W), bf),
            jax.ShapeDtypeStruct((m, GLA_V_W), bf),
            jax.ShapeDtypeStruct((m, GLA_V_W), bf),
            jax.ShapeDtypeStruct((m, CONV_W), bf),
            jax.ShapeDtypeStruct((m, LR_W), jnp.float32),
        ],
        compiler_params=pltpu.CompilerParams(
            dimension_semantics=("arbitrary",), vmem_limit_bytes=VMEM_LIMIT),
        name="inproj",
    )(x2, x2, x2, norm_g, w_all, conv_w, conv_b)


def _pair_tri(rev):
    t = np.arange(CHUNK)
    tri = (t[None, :] >= t[:, None]) if rev else (t[None, :] <= t[:, None])
    zero = np.zeros_like(tri)
    keep = np.block([[tri, tri, zero, zero], [zero, zero, tri, tri]])
    return jnp.asarray(keep / GATE_NORM, dtype=jnp.bfloat16)


def _decay_terms(lr_ref, wg_ref, tri_ref):
    ts = lr_ref.shape[0]
    x = jnp.dot(lr_ref[...].astype(jnp.bfloat16), wg_ref[...], preferred_element_type=jnp.float32)
    g = jnp.minimum(x, 0.0) - jnp.log2(1.0 + jnp.exp2(-jnp.abs(x)))
    tri = tri_ref[...]
    g_top = lax.bitcast_convert_type(
        lax.bitcast_convert_type(g, jnp.uint32) & jnp.uint32(0xFFFF0000), jnp.float32)
    g_hi = g_top.astype(jnp.bfloat16)
    g_lo = (g - g_top).astype(jnp.bfloat16)
    out = []
    for p in range(ts // (2 * CHUNK)):
        c0 = slice(2 * p * CHUNK, (2 * p + 1) * CHUNK)
        c1 = slice((2 * p + 1) * CHUNK, (2 * p + 2) * CHUNK)
        rhs = jnp.concatenate([g_hi[c0], g_lo[c0], g_hi[c1], g_lo[c1]], axis=0)
        out.append(jnp.dot(tri, rhs, preferred_element_type=jnp.float32))
    return jnp.concatenate(out, axis=0)


def _gla_kernel(qf_ref, kf_ref, vf_ref, lrf_ref, qb_ref, kb_ref, vb_ref, lrb_ref,
                wgf_ref, wgb_ref, trif_ref, trib_ref, of_ref, ob_ref, s_ref):
    ts = qf_ref.shape[0]
    nc = ts // CHUNK
    npair = nc // 2
    pair = 2 * CHUNK

    @pl.when(pl.program_id(1) == 0)
    def _():
        s_ref[...] = jnp.zeros_like(s_ref)

    ri = lax.broadcasted_iota(jnp.int32, (CHUNK, pair), 0)
    li = lax.broadcasted_iota(jnp.int32, (CHUNK, pair), 1)
    lo_half = li < CHUNK
    dirs = (
        dict(q=qf_ref, k=kf_ref, v=vf_ref, o=of_ref, s=s_ref.at[0], a=0, b_=1,
             mask_a=lo_half & (li <= ri), mask_b=lo_half | (li - CHUNK <= ri),
             ref_row=CHUNK // 2, last_row=CHUNK - 1, order=tuple(range(npair)),
             b=_decay_terms(lrf_ref, wgf_ref, trif_ref)),
        dict(q=qb_ref, k=kb_ref, v=vb_ref, o=ob_ref, s=s_ref.at[1], a=1, b_=0,
             mask_a=(~lo_half) & (li - CHUNK > ri), mask_b=(~lo_half) | (li > ri),
             ref_row=CHUNK - 1 - CHUNK // 2, last_row=0, order=tuple(range(npair - 1, -1, -1)),
             b=_decay_terms(lrb_ref, wgb_ref, trib_ref)),
    )

    streams = []
    for d in dirs:
        b_last_rows = d["b"].reshape(nc, CHUNK, GLA_QK_W)[:, d["last_row"], :]
        for h in range(GLA_HEADS):
            ksl = slice(h * GLA_DK, (h + 1) * GLA_DK)
            streams.append(dict(
                d=d, h=h, ksl=ksl, vsl=slice(h * GLA_DV, (h + 1) * GLA_DV),
                decay_t=jnp.transpose(jnp.exp2(b_last_rows[:, ksl])),
            ))

    bf = jnp.bfloat16
    zeros_k = jnp.zeros((CHUNK, GLA_DK), bf)

    def intra(st, step):
        d = st["d"]
        p = d["order"][step]
        rows = slice(p * pair, (p + 1) * pair)
        ia, ib = d["a"], d["b_"]
        b = d["b"][rows, st["ksl"]].reshape(2, CHUNK, GLA_DK)
        b_mid = b[:, d["ref_row"]:d["ref_row"] + 1, :]
        tot = b[:, d["last_row"]:d["last_row"] + 1, :]
        qe = (d["q"][rows, st["ksl"]].astype(jnp.float32).reshape(2, CHUNK, GLA_DK)
              * jnp.exp2(b - b_mid))
        ke = (d["k"][rows, st["ksl"]].astype(jnp.float32).reshape(2, CHUNK, GLA_DK)
              * jnp.exp2(b_mid - b))
        f_q = jnp.exp2(b_mid)
        f_k = jnp.exp2(tot - b_mid)
        q_in_a = (qe[ia] * f_q[ia]).astype(bf)
        q_in_b = (qe[ib] * f_q[ib]).astype(bf)
        q_in_b2 = (qe[ib] * (f_q[ib] * jnp.exp2(tot[ia]))).astype(bf)
        k_out_a = (ke[ia] * f_k[ia]).astype(bf)
        k_out_a2 = (ke[ia] * (f_k[ia] * jnp.exp2(tot[ib]))).astype(bf)
        k_out_b = (ke[ib] * f_k[ib]).astype(bf)
        ke_a = ke[ia].astype(bf)
        ke_b = ke[ib].astype(bf)
        nt = (((1,), (1,)), ((), ()))
        half = lambda x, slot: jnp.concatenate([x, zeros_k] if slot == 0 else [zeros_k, x], axis=0)
        att_a = lax.dot_general(qe[ia].astype(bf), half(ke_a, ia), nt,
                                preferred_element_type=jnp.float32)
        rhs_b = [None, None]
        rhs_b[ia] = jnp.concatenate([k_out_a, zeros_k], axis=1)
        rhs_b[ib] = jnp.concatenate([zeros_k, ke_b], axis=1)
        att_b = lax.dot_general(jnp.concatenate([q_in_b, qe[ib].astype(bf)], axis=1),
                                jnp.concatenate(rhs_b, axis=0), nt,
                                preferred_element_type=jnp.float32)
        q_rows = [None, None]
        q_rows[ia], q_rows[ib] = q_in_a, q_in_b2
        k_rows = [None, None]
        k_rows[ia], k_rows[ib] = k_out_a2, k_out_b
        return dict(p=p, rows=rows, att=(att_a, att_b), q_rows=q_rows,
                    k_pair=jnp.concatenate(k_rows, axis=0))

    ahead = {id(st): intra(st, 0) for st in streams}
    for step in range(npair):
        for st in streams:
            d = st["d"]
            cur = ahead[id(st)]
            if step + 1 < npair:
                ahead[id(st)] = intra(st, step + 1)
            ia, ib = d["a"], d["b_"]
            v = d["v"][cur["rows"], st["vsl"]]
            state = d["s"][st["h"]]
            att = [None, None]
            att[ia] = jnp.where(d["mask_a"], cur["att"][0], 0.0).astype(bf)
            att[ib] = jnp.where(d["mask_b"], cur["att"][1], 0.0).astype(bf)
            lhs = jnp.concatenate([jnp.concatenate([cur["q_rows"][0], att[0]], axis=1),
                                   jnp.concatenate([cur["q_rows"][1], att[1]], axis=1)], axis=0)
            rhs = jnp.concatenate([state.astype(bf), v], axis=0)
            o = jnp.dot(lhs, rhs, preferred_element_type=jnp.float32)
            d["o"][cur["rows"], st["vsl"]] = o.astype(d["o"].dtype)
            upd = lax.dot_general(cur["k_pair"], v, (((0,), (0,)), ((), ())),
                                  preferred_element_type=jnp.float32)
            c0 = 2 * cur["p"]
            decay = st["decay_t"][:, c0:c0 + 1] * st["decay_t"][:, c0 + 1:c0 + 2]
            d["s"][st["h"]] = decay * state + upd


def _gla(q, k, v, lr, wg_f, wg_b, *, batch, seq):
    ts = TS_GLA
    nblk = seq // ts
    fwd = lambda b, s: (b * nblk + s, 0)
    bwd = lambda b, s: (b * nblk + (nblk - 1 - s), 0)
    rows = lambda idx: [pl.BlockSpec((ts, w), idx) for w in (GLA_QK_W, GLA_QK_W, GLA_V_W, LR_W)]
    const = lambda shp: pl.BlockSpec(shp, lambda b, s: (0, 0))
    gate = [const((LR_W, GLA_QK_W))]
    o_shape = jax.ShapeDtypeStruct((batch * seq, GLA_V_W), jnp.bfloat16)
    return pl.pallas_call(
        _gla_kernel,
        grid=(batch, nblk),
        in_specs=rows(fwd) + rows(bwd) + gate + gate + [const((2 * CHUNK, 4 * CHUNK))] * 2,
        out_specs=[pl.BlockSpec((ts, GLA_V_W), fwd), pl.BlockSpec((ts, GLA_V_W), bwd)],
        out_shape=[o_shape, o_shape],
        scratch_shapes=[pltpu.VMEM((2, GLA_HEADS, GLA_DK, GLA_DV), jnp.float32)],
        compiler_params=pltpu.CompilerParams(
            dimension_semantics=("arbitrary", "arbitrary"), vmem_limit_bytes=VMEM_LIMIT),
        name="gla",
    )(q, k, v, lr, q, k, v, lr, wg_f, wg_b, _pair_tri(False), _pair_tri(True))


def _out_kernel(x_ref, of_ref, ob_ref, sz_ref, yc_ref, gn_ref, wo_ref, fg_ref, out_ref):
    acc = jnp.dot(yc_ref[...], wo_ref[GLA_V_W:, :], preferred_element_type=jnp.float32)
    for h in range(GLA_HEADS):
        sl = slice(h * GLA_DV, (h + 1) * GLA_DV)
        o_h = of_ref[:, sl].astype(jnp.float32) + ob_ref[:, sl].astype(jnp.float32)
        ms = jnp.mean(o_h * o_h, axis=-1, keepdims=True)
        y_h = o_h * lax.rsqrt(ms + EPS) * gn_ref[...] * sz_ref[:, sl].astype(jnp.float32)
        acc = acc + jnp.dot(y_h.astype(jnp.bfloat16), wo_ref[sl, :],
                            preferred_element_type=jnp.float32)
    xo = x_ref[...] + acc
    ms = jnp.mean(xo * xo, axis=-1, keepdims=True)
    out_ref[...] = xo * lax.rsqrt(ms + EPS) * fg_ref[...]


def _outproj(x2, o_f, o_b, sz, yc, gn, wo, fg):
    m = x2.shape[0]
    tm = TM_OUT
    row = lambda w: pl.BlockSpec((tm, w), lambda i: (i, 0))
    const = lambda shp: pl.BlockSpec(shp, lambda i: (0, 0))
    return pl.pallas_call(
        _out_kernel,
        grid=(m // tm,),
        in_specs=[row(D_MODEL), row(GLA_V_W), row(GLA_V_W), row(GLA_V_W), row(CONV_W),
                  const((1, GLA_DV)), const((GLA_V_W + CONV_W, D_MODEL)), const((1, D_MODEL))],
        out_specs=row(D_MODEL),
        out_shape=jax.ShapeDtypeStruct((m, D_MODEL), jnp.float32),
        compiler_params=pltpu.CompilerParams(
            dimension_semantics=("arbitrary",), vmem_limit_bytes=VMEM_LIMIT),
        name="outproj",
    )(x2, o_f, o_b, sz, yc, gn, wo, fg)


def _regroup_kernel(a_ref, b_ref, o_ref):
    j = pl.program_id(0)
    n_before = _OFF_BG // REGROUP_TILE
    n_main = _OFF_LR // REGROUP_TILE
    lr = 2 * GATE_RANK

    def put(rows):
        o_ref[...] = jnp.transpose(rows).astype(o_ref.dtype)

    @pl.when(j < n_before)
    def _():
        put(a_ref[...])

    @pl.when((j >= n_before) & (j < n_main))
    def _():
        put(jnp.concatenate([a_ref[lr:, :], b_ref[...]], axis=0))

    @pl.when(j == n_main)
    def _():
        put(jnp.concatenate([a_ref[:lr, :], jnp.zeros((REGROUP_TILE - lr, a_ref.shape[1]),
                                                      a_ref.dtype)], axis=0))


def _regroup_w_in(w):
    wt = jnp.swapaxes(w, 1, 2)
    _, n, k = wt.shape
    lr = 2 * GATE_RANK
    assert lr == REGROUP_HALO and _OFF_BG % REGROUP_TILE == 0 and _OFF_LR % REGROUP_TILE == 0
    n_main = _OFF_LR // REGROUP_TILE
    sub = REGROUP_TILE // REGROUP_HALO
    first = lambda j: jnp.where(j == n_main, _OFF_BG // REGROUP_TILE, j)
    return pl.pallas_call(
        _regroup_kernel,
        grid=(W_ALL // REGROUP_TILE,),
        in_specs=[pl.BlockSpec((None, REGROUP_TILE, k), lambda j: (0, first(j), 0)),
                  pl.BlockSpec((None, REGROUP_HALO, k),
                               lambda j: (0, jnp.minimum((j + 1) * sub, n // REGROUP_HALO - 1), 0))],
        out_specs=pl.BlockSpec((k, REGROUP_TILE), lambda j: (0, j)),
        out_shape=jax.ShapeDtypeStruct((k, W_ALL), jnp.bfloat16),
        compiler_params=pltpu.CompilerParams(dimension_semantics=("arbitrary",)),
        name="regroup",
    )(wt, wt)


def _gate_weight(w_gk, b_gk, first_row):
    bias = b_gk * LOG2_E
    bias_hi = bias.astype(jnp.bfloat16)
    bias_lo = (bias - bias_hi.astype(jnp.float32)).astype(jnp.bfloat16)
    full = jnp.zeros((LR_W, GLA_QK_W), jnp.bfloat16)
    full = lax.dynamic_update_slice(full, (w_gk * LOG2_E).astype(jnp.bfloat16), (first_row, 0))
    return lax.dynamic_update_slice(full, jnp.stack([bias_hi, bias_lo]), (ONES_COL, 0))


def kernel(x, norm_g, w_in, w_gk_f, b_gk_f, w_gk_b, b_gk_b, gla_norm_g, conv_w, conv_b, w_out, final_g):
    batch, seq, d = x.shape
    depth = w_in.shape[0]
    assert depth == 1 and d == D_MODEL
    assert seq % TM_PROJ == 0 and seq % TS_GLA == 0 and seq % TM_OUT == 0
    x2 = x.reshape(batch * seq, d)
    q, k, v, sz, yc, lr = _inproj(x2, norm_g[0][None, :], _regroup_w_in(w_in),
                                  conv_w[0], conv_b[0][None, :], seq=seq)
    o_f, o_b = _gla(q, k, v, lr, _gate_weight(w_gk_f[0], b_gk_f[0], 0),
                    _gate_weight(w_gk_b[0], b_gk_b[0], GATE_RANK), batch=batch, seq=seq)
    out = _outproj(x2, o_f, o_b, sz, yc, gla_norm_g[0][None, :],
                   w_out[0].astype(jnp.bfloat16), final_g[None, :])
    return out.reshape(batch, seq, d)
```

```python
import functools

import jax
import jax.numpy as jnp
import numpy as np
from jax import lax
from jax.experimental import pallas as pl
from jax.experimental.pallas import tpu as pltpu

D_MODEL = 1024
GLA_HEADS = 4
GLA_DK = 128
GLA_DV = 256
GLA_QK_W = GLA_HEADS * GLA_DK
GLA_V_W = GLA_HEADS * GLA_DV
GATE_RANK = 16
GATE_NORM = 16.0
CHUNK = 64
CONV_W = 1024
EPS = 1e-6
LOG2_E = 1.4426950408889634

LANES = 128
HALO = 16
LR_W = LANES
ONES_COL = 2 * GATE_RANK

_OFF_Q = 0
_OFF_K = _OFF_Q + GLA_QK_W
_OFF_V = _OFF_K + GLA_QK_W
_OFF_ZA = _OFF_V + GLA_V_W
_OFF_BG = _OFF_ZA + GLA_V_W
_OFF_CG = _OFF_BG + CONV_W
_OFF_HC = _OFF_CG + CONV_W
_OFF_ZC = _OFF_HC + CONV_W
_OFF_LR = _OFF_ZC + CONV_W
REGROUP_TILE = 4 * LANES
REGROUP_HALO = 2 * GATE_RANK
W_ALL = _OFF_LR + REGROUP_TILE

TILE = 512
VMEM_LIMIT = 56 * 1024 * 1024


def _silu(z):
    return z * (1.0 / (1.0 + jnp.exp(-z)))


def _inproj_kernel(x_ref, xp_ref, xn_ref, g_ref, w_ref, cw_ref, cb_ref,
                   q_ref, k_ref, v_ref, sz_ref, yc_ref, lr_ref, *, tiles_per_seq):
    tm = x_ref.shape[0]
    pos = pl.program_id(0) % tiles_per_seq

    def normed(x):
        ms = jnp.mean(x * x, axis=-1, keepdims=True)
        return (x * lax.rsqrt(ms + EPS) * g_ref[...]).astype(jnp.bfloat16)

    h = normed(x_ref[...])
    h_ext = jnp.concatenate([h, normed(xp_ref[...]), normed(xn_ref[...])], axis=0)

    def proj(lo, hi, lhs=h):
        return jnp.dot(lhs, w_ref[:, lo:hi], preferred_element_type=jnp.float32)

    u_ext = proj(_OFF_CG, _OFF_HC, h_ext) * proj(_OFF_HC, _OFF_ZC, h_ext)
    u = u_ext[:tm]
    prev_row = jnp.where(pos == 0, 0.0, u_ext[tm + HALO - 1:tm + HALO])
    next_row = jnp.where(pos == tiles_per_seq - 1, 0.0, u_ext[tm + HALO:tm + HALO + 1])
    t_i = lax.broadcasted_iota(jnp.int32, (tm, 1), 0)
    u_prev = jnp.where(t_i == 0, prev_row, pltpu.roll(u, 1, 0))
    u_next = jnp.where(t_i == tm - 1, next_row, pltpu.roll(u, tm - 1, 0))
    conv = cw_ref[0:1, :] * u_prev + cw_ref[1:2, :] * u + cw_ref[2:3, :] * u_next + cb_ref[...]
    yc_ref[...] = (proj(_OFF_BG, _OFF_CG) * conv * _silu(proj(_OFF_ZC, _OFF_LR))).astype(yc_ref.dtype)

    sz_ref[...] = _silu(proj(_OFF_ZA, _OFF_BG)).astype(sz_ref.dtype)
    q_ref[...] = (proj(_OFF_Q, _OFF_K) * (GLA_DK ** -0.5)).astype(q_ref.dtype)
    k_ref[...] = proj(_OFF_K, _OFF_V).astype(k_ref.dtype)
    v_ref[...] = proj(_OFF_V, _OFF_ZA).astype(v_ref.dtype)
    col = lax.broadcasted_iota(jnp.int32, (1, LR_W), 1)
    ones = jnp.where((col == ONES_COL) | (col == ONES_COL + 1), 1.0, 0.0)
    lr_ref[...] = proj(_OFF_LR, _OFF_LR + LR_W) + ones


def _inproj(x2, norm_g, w_all, conv_w, conv_b, *, seq):
    m = x2.shape[0]
    tm = TILE
    sub = tm // HALO
    nsub = m // HALO
    row = lambda w: pl.BlockSpec((tm, w), lambda i: (i, 0))
    const = lambda shp: pl.BlockSpec(shp, lambda i: (0, 0))
    halo_prev = pl.BlockSpec((HALO, D_MODEL), lambda i: (jnp.maximum(i * sub - 1, 0), 0))
    halo_next = pl.BlockSpec((HALO, D_MODEL), lambda i: (jnp.minimum((i + 1) * sub, nsub - 1), 0))
    bf = jnp.bfloat16
    return pl.pallas_call(
        functools.partial(_inproj_kernel, tiles_per_seq=seq // tm),
        grid=(m // tm,),
        in_specs=[
            row(D_MODEL), halo_prev, halo_next,
            const((1, D_MODEL)),
            pl.BlockSpec((D_MODEL, W_ALL), lambda i: (0, 0), pipeline_mode=pl.Buffered(1)),
            const((3, CONV_W)), const((1, CONV_W)),
        ],
        out_specs=[row(GLA_QK_W), row(GLA_QK_W), row(GLA_V_W), row(GLA_V_W),
                   row(CONV_W), row(LR_W)],
        out_shape=[
            jax.ShapeDtypeStruct((m, GLA_QK_W), bf),
            jax.ShapeDtypeStruct((m, GLA_QK_W), bf),
            jax.ShapeDtypeStruct((m, GLA_V_W), bf),
            jax.ShapeDtypeStruct((m, GLA_V_W), bf),
            jax.ShapeDtypeStruct((m, CONV_W), bf),
            jax.ShapeDtypeStruct((m, LR_W), jnp.float32),
        ],
        compiler_params=pltpu.CompilerParams(
            dimension_semantics=("arbitrary",), vmem_limit_bytes=VMEM_LIMIT),
        name="inproj",
    )(x2, x2, x2, norm_g, w_all, conv_w, conv_b)


def _pair_tri(rev):
    t = np.arange(CHUNK)
    tri = (t[None, :] >= t[:, None]) if rev else (t[None, :] <= t[:, None])
    zero = np.zeros_like(tri)
    keep = np.block([[tri, tri, zero, zero], [zero, zero, tri, tri]])
    return jnp.asarray(keep / GATE_NORM, dtype=jnp.bfloat16)


def _decay_terms(lr_ref, wg_ref, tri_ref):
    ts = lr_ref.shape[0]
    x = jnp.dot(lr_ref[...].astype(jnp.bfloat16), wg_ref[...], preferred_element_type=jnp.float32)
    g = jnp.minimum(x, 0.0) - jnp.log2(1.0 + jnp.exp2(-jnp.abs(x)))
    tri = tri_ref[...]
    g_top = lax.bitcast_convert_type(
        lax.bitcast_convert_type(g, jnp.uint32) & jnp.uint32(0xFFFF0000), jnp.float32)
    g_hi = g_top.astype(jnp.bfloat16)
    g_lo = (g - g_top).astype(jnp.bfloat16)
    out = []
    for p in range(ts // (2 * CHUNK)):
        c0 = slice(2 * p * CHUNK, (2 * p + 1) * CHUNK)
        c1 = slice((2 * p + 1) * CHUNK, (2 * p + 2) * CHUNK)
        rhs = jnp.concatenate([g_hi[c0], g_lo[c0], g_hi[c1], g_lo[c1]], axis=0)
        out.append(jnp.dot(tri, rhs, preferred_element_type=jnp.float32))
    return jnp.concatenate(out, axis=0)


def _gla_body(qf_ref, kf_ref, vf_ref, lrf_ref, qb_ref, kb_ref, vb_ref, lrb_ref,
              wgf_ref, wgb_ref, trif_ref, trib_ref, of_ref, ob_ref, s_ref, first_block):
    ts = qf_ref.shape[0]
    nc = ts // CHUNK
    npair = nc // 2
    pair = 2 * CHUNK

    @pl.when(first_block)
    def _():
        s_ref[...] = jnp.zeros_like(s_ref)

    ri = lax.broadcasted_iota(jnp.int32, (CHUNK, pair), 0)
    li = lax.broadcasted_iota(jnp.int32, (CHUNK, pair), 1)
    lo_half = li < CHUNK
    dirs = (
        dict(q=qf_ref, k=kf_ref, v=vf_ref, o=of_ref, s=s_ref.at[0], a=0, b_=1,
             mask_a=lo_half & (li <= ri), mask_b=lo_half | (li - CHUNK <= ri),
             ref_row=CHUNK // 2, last_row=CHUNK - 1, order=tuple(range(npair)),
             b=_decay_terms(lrf_ref, wgf_ref, trif_ref)),
        dict(q=qb_ref, k=kb_ref, v=vb_ref, o=ob_ref, s=s_ref.at[1], a=1, b_=0,
             mask_a=(~lo_half) & (li - CHUNK > ri), mask_b=(~lo_half) | (li > ri),
             ref_row=CHUNK - 1 - CHUNK // 2, last_row=0, order=tuple(range(npair - 1, -1, -1)),
             b=_decay_terms(lrb_ref, wgb_ref, trib_ref)),
    )

    streams = []
    for d in dirs:
        b_last_rows = d["b"].reshape(nc, CHUNK, GLA_QK_W)[:, d["last_row"], :]
        for h in range(GLA_HEADS):
            ksl = slice(h * GLA_DK, (h + 1) * GLA_DK)
            streams.append(dict(
                d=d, h=h, ksl=ksl, vsl=slice(h * GLA_DV, (h + 1) * GLA_DV),
                decay_t=jnp.transpose(jnp.exp2(b_last_rows[:, ksl])),
            ))

    bf = jnp.bfloat16
    zeros_k = jnp.zeros((CHUNK, GLA_DK), bf)

    def intra(st, step):
        d = st["d"]
        p = d["order"][step]
        rows = slice(p * pair, (p + 1) * pair)
        ia, ib = d["a"], d["b_"]
        b = d["b"][rows, st["ksl"]].reshape(2, CHUNK, GLA_DK)
        b_mid = b[:, d["ref_row"]:d["ref_row"] + 1, :]
        tot = b[:, d["last_row"]:d["last_row"] + 1, :]
        qe = (d["q"][rows, st["ksl"]].astype(jnp.float32).reshape(2, CHUNK, GLA_DK)
              * jnp.exp2(b - b_mid))
        ke = (d["k"][rows, st["ksl"]].astype(jnp.float32).reshape(2, CHUNK, GLA_DK)
              * jnp.exp2(b_mid - b))
        f_q = jnp.exp2(b_mid)
        f_k = jnp.exp2(tot - b_mid)
        q_in_a = (qe[ia] * f_q[ia]).astype(bf)
        q_in_b = (qe[ib] * f_q[ib]).astype(bf)
        q_in_b2 = (qe[ib] * (f_q[ib] * jnp.exp2(tot[ia]))).astype(bf)
        k_out_a = (ke[ia] * f_k[ia]).astype(bf)
        k_out_a2 = (ke[ia] * (f_k[ia] * jnp.exp2(tot[ib]))).astype(bf)
        k_out_b = (ke[ib] * f_k[ib]).astype(bf)
        ke_a = ke[ia].astype(bf)
        ke_b = ke[ib].astype(bf)
        nt = (((1,), (1,)), ((), ()))
        half = lambda x, slot: jnp.concatenate([x, zeros_k] if slot == 0 else [zeros_k, x], axis=0)
        att_a = lax.dot_general(qe[ia].astype(bf), half(ke_a, ia), nt,
                                preferred_element_type=jnp.float32)
        rhs_b = [None, None]
        rhs_b[ia] = jnp.concatenate([k_out_a, zeros_k], axis=1)
        rhs_b[ib] = jnp.concatenate([zeros_k, ke_b], axis=1)
        att_b = lax.dot_general(jnp.concatenate([q_in_b, qe[ib].astype(bf)], axis=1),
                                jnp.concatenate(rhs_b, axis=0), nt,
                                preferred_element_type=jnp.float32)
        q_rows = [None, None]
        q_rows[ia], q_rows[ib] = q_in_a, q_in_b2
        k_rows = [None, None]
        k_rows[ia], k_rows[ib] = k_out_a2, k_out_b
        return dict(p=p, rows=rows, att=(att_a, att_b), q_rows=q_rows,
                    k_pair=jnp.concatenate(k_rows, axis=0))

    ahead = {id(st): intra(st, 0) for st in streams}
    for step in range(npair):
        for st in streams:
            d = st["d"]
            cur = ahead[id(st)]
            if step + 1 < npair:
                ahead[id(st)] = intra(st, step + 1)
            ia, ib = d["a"], d["b_"]
            v = d["v"][cur["rows"], st["vsl"]]
            state = d["s"][st["h"]]
            att = [None, None]
            att[ia] = jnp.where(d["mask_a"], cur["att"][0], 0.0).astype(bf)
            att[ib] = jnp.where(d["mask_b"], cur["att"][1], 0.0).astype(bf)
            lhs = jnp.concatenate([jnp.concatenate([cur["q_rows"][0], att[0]], axis=1),
                                   jnp.concatenate([cur["q_rows"][1], att[1]], axis=1)], axis=0)
            rhs = jnp.concatenate([state.astype(bf), v], axis=0)
            o = jnp.dot(lhs, rhs, preferred_element_type=jnp.float32)
            d["o"][cur["rows"], st["vsl"]] = o.astype(d["o"].dtype)
            upd = lax.dot_general(cur["k_pair"], v, (((0,), (0,)), ((), ())),
                                  preferred_element_type=jnp.float32)
            c0 = 2 * cur["p"]
            decay = st["decay_t"][:, c0:c0 + 1] * st["decay_t"][:, c0 + 1:c0 + 2]
            d["s"][st["h"]] = decay * state + upd


def _out_body(x_ref, of_ref, ob_ref, sz_ref, yc_ref, gn_ref, wo_ref, fg_ref, out_ref):
    acc = jnp.dot(yc_ref[...], wo_ref[GLA_V_W:, :], preferred_element_type=jnp.float32)
    for h in range(GLA_HEADS):
        sl = slice(h * GLA_DV, (h + 1) * GLA_DV)
        o_h = of_ref[:, sl].astype(jnp.float32) + ob_ref[:, sl].astype(jnp.float32)
        ms = jnp.mean(o_h * o_h, axis=-1, keepdims=True)
        y_h = o_h * lax.rsqrt(ms + EPS) * gn_ref[...] * sz_ref[:, sl].astype(jnp.float32)
        acc = acc + jnp.dot(y_h.astype(jnp.bfloat16), wo_ref[sl, :],
                            preferred_element_type=jnp.float32)
    xo = x_ref[...] + acc
    ms = jnp.mean(xo * xo, axis=-1, keepdims=True)
    out_ref[...] = xo * lax.rsqrt(ms + EPS) * fg_ref[...]


def _mix_kernel(qf_ref, kf_ref, vf_ref, lrf_ref, qb_ref, kb_ref, vb_ref, lrb_ref,
                wgf_ref, wgb_ref, trif_ref, trib_ref,
                x_ref, sz_ref, yc_ref, gn_ref, wo_ref, fg_ref, out_ref, s_ref, o_ref):
    s = pl.program_id(1)
    nblk = pl.num_programs(1) // 2

    @pl.when(s < nblk)
    def _():
        _gla_body(qf_ref, kf_ref, vf_ref, lrf_ref, qb_ref, kb_ref, vb_ref, lrb_ref,
                  wgf_ref, wgb_ref, trif_ref, trib_ref,
                  o_ref.at[0, s], o_ref.at[1, nblk - 1 - s], s_ref, s == 0)

    @pl.when(s >= nblk)
    def _():
        _out_body(x_ref, o_ref.at[0, s - nblk], o_ref.at[1, s - nblk], sz_ref, yc_ref,
                  gn_ref, wo_ref, fg_ref, out_ref)


def _mix(q, k, v, lr, wg_f, wg_b, x2, sz, yc, gn, wo, fg, *, batch, seq):
    ts = TILE
    nblk = seq // ts
    gla_s = lambda s: jnp.minimum(s, nblk - 1)
    fwd = lambda b, s: (b * nblk + gla_s(s), 0)
    bwd = lambda b, s: (b * nblk + (nblk - 1 - gla_s(s)), 0)
    tile = lambda b, s: (b * nblk + jnp.maximum(s - nblk, 0), 0)
    rows = lambda idx: [pl.BlockSpec((ts, w), idx) for w in (GLA_QK_W, GLA_QK_W, GLA_V_W, LR_W)]
    const = lambda shp, **kw: pl.BlockSpec(shp, lambda b, s: (0, 0), **kw)
    return pl.pallas_call(
        _mix_kernel,
        grid=(batch, 2 * nblk),
        in_specs=(rows(fwd) + rows(bwd) + [const((LR_W, GLA_QK_W))] * 2
                  + [const((2 * CHUNK, 4 * CHUNK))] * 2
                  + [pl.BlockSpec((ts, D_MODEL), tile), pl.BlockSpec((ts, GLA_V_W), tile),
                     pl.BlockSpec((ts, CONV_W), tile), const((1, GLA_DV)),
                     const((GLA_V_W + CONV_W, D_MODEL), pipeline_mode=pl.Buffered(1)),
                     const((1, D_MODEL))]),
        out_specs=pl.BlockSpec((ts, D_MODEL), tile),
        out_shape=jax.ShapeDtypeStruct((batch * seq, D_MODEL), jnp.float32),
        scratch_shapes=[pltpu.VMEM((2, GLA_HEADS, GLA_DK, GLA_DV), jnp.float32),
                        pltpu.VMEM((2, nblk, ts, GLA_V_W), jnp.bfloat16)],
        compiler_params=pltpu.CompilerParams(
            dimension_semantics=("arbitrary", "arbitrary"), vmem_limit_bytes=VMEM_LIMIT),
        name="mix",
    )(q, k, v, lr, q, k, v, lr, wg_f, wg_b, _pair_tri(False), _pair_tri(True),
      x2, sz, yc, gn, wo, fg)


def _regroup_kernel(a_ref, b_ref, o_ref):
    j = pl.program_id(0)
    n_before = _OFF_BG // REGROUP_TILE
    n_main = _OFF_LR // REGROUP_TILE
    lr = 2 * GATE_RANK

    def put(rows):
        o_ref[...] = jnp.transpose(rows).astype(o_ref.dtype)

    @pl.when(j < n_before)
    def _():
        put(a_ref[...])

    @pl.when((j >= n_before) & (j < n_main))
    def _():
        put(jnp.concatenate([a_ref[lr:, :], b_ref[...]], axis=0))

    @pl.when(j == n_main)
    def _():
        put(jnp.concatenate([a_ref[:lr, :], jnp.zeros((REGROUP_TILE - lr, a_ref.shape[1]),
                                                      a_ref.dtype)], axis=0))


def _regroup_w_in(w):
    wt = jnp.swapaxes(w, 1, 2)
    _, n, k = wt.shape
    lr = 2 * GATE_RANK
    assert lr == REGROUP_HALO and _OFF_BG % REGROUP_TILE == 0 and _OFF_LR % REGROUP_TILE == 0
    n_main = _OFF_LR // REGROUP_TILE
    sub = REGROUP_TILE // REGROUP_HALO
    first = lambda j: jnp.where(j == n_main, _OFF_BG // REGROUP_TILE, j)
    return pl.pallas_call(
        _regroup_kernel,
        grid=(W_ALL // REGROUP_TILE,),
        in_specs=[pl.BlockSpec((None, REGROUP_TILE, k), lambda j: (0, first(j), 0)),
                  pl.BlockSpec((None, REGROUP_HALO, k),
                               lambda j: (0, jnp.minimum((j + 1) * sub, n // REGROUP_HALO - 1), 0))],
        out_specs=pl.BlockSpec((k, REGROUP_TILE), lambda j: (0, j)),
        out_shape=jax.ShapeDtypeStruct((k, W_ALL), jnp.bfloat16),
        compiler_params=pltpu.CompilerParams(dimension_semantics=("arbitrary",)),
        name="regroup",
    )(wt, wt)


def _gate_weight(w_gk, b_gk, first_row):
    bias = b_gk * LOG2_E
    bias_hi = bias.astype(jnp.bfloat16)
    bias_lo = (bias - bias_hi.astype(jnp.float32)).astype(jnp.bfloat16)
    full = jnp.zeros((LR_W, GLA_QK_W), jnp.bfloat16)
    full = lax.dynamic_update_slice(full, (w_gk * LOG2_E).astype(jnp.bfloat16), (first_row, 0))
    return lax.dynamic_update_slice(full, jnp.stack([bias_hi, bias_lo]), (ONES_COL, 0))


def kernel(x, norm_g, w_in, w_gk_f, b_gk_f, w_gk_b, b_gk_b, gla_norm_g, conv_w, conv_b, w_out, final_g):
    batch, seq, d = x.shape
    depth = w_in.shape[0]
    assert depth == 1 and d == D_MODEL
    assert seq % TILE == 0
    x2 = x.reshape(batch * seq, d)
    q, k, v, sz, yc, lr = _inproj(x2, norm_g[0][None, :], _regroup_w_in(w_in),
                                  conv_w[0], conv_b[0][None, :], seq=seq)
    out = _mix(q, k, v, lr, _gate_weight(w_gk_f[0], b_gk_f[0], 0),
               _gate_weight(w_gk_b[0], b_gk_b[0], GATE_RANK), x2, sz, yc,
               gla_norm_g[0][None, :], w_out[0].astype(jnp.bfloat16), final_g[None, :],
               batch=batch, seq=seq)
    return out.reshape(batch, seq, d)
```

```python
import functools

import jax
import jax.numpy as jnp
import numpy as np
from jax import lax
from jax.experimental import pallas as pl
from jax.experimental.pallas import tpu as pltpu

D_MODEL = 1024
GLA_HEADS = 4
GLA_DK = 128
GLA_DV = 256
GLA_QK_W = GLA_HEADS * GLA_DK
GLA_V_W = GLA_HEADS * GLA_DV
GATE_RANK = 16
GATE_NORM = 16.0
CHUNK = 64
CONV_W = 1024
EPS = 1e-6
LOG2_E = 1.4426950408889634

LANES = 128
HALO = 8
LR_W = LANES
ONES_COL = 2 * GATE_RANK

_OFF_Q = 0
_OFF_K = _OFF_Q + GLA_QK_W
_OFF_V = _OFF_K + GLA_QK_W
_OFF_ZA = _OFF_V + GLA_V_W
_OFF_BG = _OFF_ZA + GLA_V_W
_OFF_CG = _OFF_BG + CONV_W
_OFF_HC = _OFF_CG + CONV_W
_OFF_ZC = _OFF_HC + CONV_W
_OFF_LR = _OFF_ZC + CONV_W
REGROUP_TILE = 4 * LANES
REGROUP_HALO = 2 * GATE_RANK
W_ALL = _OFF_LR + REGROUP_TILE

TILE = 512
SUB_TILES = 1
VMEM_LIMIT = 56 * 1024 * 1024


def _silu(z):
    return z * (1.0 / (1.0 + jnp.exp(-z)))


def _inproj_kernel(x_ref, xp_ref, xn_ref, g_ref, w_ref, cw_ref, cb_ref,
                   q_ref, k_ref, v_ref, sz_ref, yc_ref, lr_ref, *, steps_per_seq):
    step = pl.program_id(0) % steps_per_seq

    def normed(x):
        ms = jnp.mean(x * x, axis=-1, keepdims=True)
        return x * lax.rsqrt(ms + EPS) * g_ref[...]

    for r in range(SUB_TILES):
        rows = slice(r * TILE, (r + 1) * TILE)
        first = r == 0
        last = r == SUB_TILES - 1
        h = normed(x_ref[rows, :]).astype(jnp.bfloat16)
        before = xp_ref[...] if first else x_ref[r * TILE - HALO:r * TILE, :]
        after = xn_ref[...] if last else x_ref[(r + 1) * TILE:(r + 1) * TILE + HALO, :]
        halo = jnp.concatenate([normed(before), normed(after)], axis=0).astype(jnp.bfloat16)
        h_ext = jnp.concatenate([h, halo], axis=0)

        def proj(lo, hi, lhs=h):
            return jnp.dot(lhs, w_ref[:, lo:hi], preferred_element_type=jnp.float32)

        u_ext = proj(_OFF_CG, _OFF_HC, h_ext) * proj(_OFF_HC, _OFF_ZC, h_ext)
        u = u_ext[:TILE]
        prev_row = u_ext[TILE + HALO - 1:TILE + HALO]
        next_row = u_ext[TILE + HALO:TILE + HALO + 1]
        if first:
            prev_row = jnp.where(step == 0, 0.0, prev_row)
        if last:
            next_row = jnp.where(step == steps_per_seq - 1, 0.0, next_row)
        t_i = lax.broadcasted_iota(jnp.int32, (TILE, 1), 0)
        u_prev = jnp.where(t_i == 0, prev_row, pltpu.roll(u, 1, 0))
        u_next = jnp.where(t_i == TILE - 1, next_row, pltpu.roll(u, TILE - 1, 0))
        conv = cw_ref[0:1, :] * u_prev + cw_ref[1:2, :] * u + cw_ref[2:3, :] * u_next + cb_ref[...]
        yc_ref[rows, :] = (proj(_OFF_BG, _OFF_CG) * conv * _silu(proj(_OFF_ZC, _OFF_LR))
                           ).astype(yc_ref.dtype)

        sz_ref[rows, :] = _silu(proj(_OFF_ZA, _OFF_BG)).astype(sz_ref.dtype)
        q_ref[rows, :] = (proj(_OFF_Q, _OFF_K) * (GLA_DK ** -0.5)).astype(q_ref.dtype)
        k_ref[rows, :] = proj(_OFF_K, _OFF_V).astype(k_ref.dtype)
        v_ref[rows, :] = proj(_OFF_V, _OFF_ZA).astype(v_ref.dtype)
        col = lax.broadcasted_iota(jnp.int32, (1, LR_W), 1)
        ones = jnp.where((col == ONES_COL) | (col == ONES_COL + 1), 1.0, 0.0)
        lr_ref[rows, :] = proj(_OFF_LR, _OFF_LR + LR_W) + ones


def _inproj(x2, norm_g, w_all, conv_w, conv_b, *, seq):
    m = x2.shape[0]
    tm = SUB_TILES * TILE
    assert seq % tm == 0
    sub = tm // HALO
    nsub = m // HALO
    row = lambda w: pl.BlockSpec((tm, w), lambda i: (i, 0))
    const = lambda shp: pl.BlockSpec(shp, lambda i: (0, 0))
    halo_prev = pl.BlockSpec((HALO, D_MODEL), lambda i: (jnp.maximum(i * sub - 1, 0), 0))
    halo_next = pl.BlockSpec((HALO, D_MODEL), lambda i: (jnp.minimum((i + 1) * sub, nsub - 1), 0))
    bf = jnp.bfloat16
    return pl.pallas_call(
        functools.partial(_inproj_kernel, steps_per_seq=seq // tm),
        grid=(m // tm,),
        in_specs=[
            row(D_MODEL), halo_prev, halo_next,
            const((1, D_MODEL)),
            pl.BlockSpec((D_MODEL, W_ALL), lambda i: (0, 0), pipeline_mode=pl.Buffered(1)),
            const((3, CONV_W)), const((1, CONV_W)),
        ],
        out_specs=[row(GLA_QK_W), row(GLA_QK_W), row(GLA_V_W), row(GLA_V_W),
                   row(CONV_W), row(LR_W)],
        out_shape=[
            jax.ShapeDtypeStruct((m, GLA_QK_W), bf),
            jax.ShapeDtypeStruct((m, GLA_QK_W), bf),
            jax.ShapeDtypeStruct((m, GLA_V_W), bf),
            jax.ShapeDtypeStruct((m, GLA_V_W), bf),
            jax.ShapeDtypeStruct((m, CONV_W), bf),
            jax.ShapeDtypeStruct((m, LR_W), jnp.float32),
        ],
        compiler_params=pltpu.CompilerParams(
            dimension_semantics=("arbitrary",), vmem_limit_bytes=VMEM_LIMIT),
        name="inproj",
    )(x2, x2, x2, norm_g, w_all, conv_w, conv_b)


def _pair_tri(rev):
    t = np.arange(CHUNK)
    tri = (t[None, :] >= t[:, None]) if rev else (t[None, :] <= t[:, None])
    zero = np.zeros_like(tri)
    keep = np.block([[tri, tri, zero, zero], [zero, zero, tri, tri]])
    return jnp.asarray(keep / GATE_NORM, dtype=jnp.bfloat16)


def _decay_terms(lr_ref, wg_ref, tri_ref):
    ts = lr_ref.shape[0]
    x = jnp.dot(lr_ref[...].astype(jnp.bfloat16), wg_ref[...], preferred_element_type=jnp.float32)
    g = jnp.minimum(x, 0.0) - jnp.log2(1.0 + jnp.exp2(-jnp.abs(x)))
    tri = tri_ref[...]
    g_top = lax.bitcast_convert_type(
        lax.bitcast_convert_type(g, jnp.uint32) & jnp.uint32(0xFFFF0000), jnp.float32)
    g_hi = g_top.astype(jnp.bfloat16)
    g_lo = (g - g_top).astype(jnp.bfloat16)
    out = []
    for p in range(ts // (2 * CHUNK)):
        c0 = slice(2 * p * CHUNK, (2 * p + 1) * CHUNK)
        c1 = slice((2 * p + 1) * CHUNK, (2 * p + 2) * CHUNK)
        rhs = jnp.concatenate([g_hi[c0], g_lo[c0], g_hi[c1], g_lo[c1]], axis=0)
        out.append(jnp.dot(tri, rhs, preferred_element_type=jnp.float32))
    return jnp.concatenate(out, axis=0)


def _gla_body(qf_ref, kf_ref, vf_ref, lrf_ref, qb_ref, kb_ref, vb_ref, lrb_ref,
              wgf_ref, wgb_ref, trif_ref, trib_ref, of_ref, ob_ref, s_ref, first_block):
    ts = qf_ref.shape[0]
    nc = ts // CHUNK
    npair = nc // 2
    pair = 2 * CHUNK

    @pl.when(first_block)
    def _():
        s_ref[...] = jnp.zeros_like(s_ref)

    ri = lax.broadcasted_iota(jnp.int32, (CHUNK, pair), 0)
    li = lax.broadcasted_iota(jnp.int32, (CHUNK, pair), 1)
    lo_half = li < CHUNK
    dirs = (
        dict(q=qf_ref, k=kf_ref, v=vf_ref, o=of_ref, s=s_ref.at[0], a=0, b_=1,
             mask_a=lo_half & (li <= ri), mask_b=lo_half | (li - CHUNK <= ri),
             ref_row=CHUNK // 2, last_row=CHUNK - 1, order=tuple(range(npair)),
             b=_decay_terms(lrf_ref, wgf_ref, trif_ref)),
        dict(q=qb_ref, k=kb_ref, v=vb_ref, o=ob_ref, s=s_ref.at[1], a=1, b_=0,
             mask_a=(~lo_half) & (li - CHUNK > ri), mask_b=(~lo_half) | (li > ri),
             ref_row=CHUNK - 1 - CHUNK // 2, last_row=0, order=tuple(range(npair - 1, -1, -1)),
             b=_decay_terms(lrb_ref, wgb_ref, trib_ref)),
    )

    streams = []
    for d in dirs:
        b_last_rows = d["b"].reshape(nc, CHUNK, GLA_QK_W)[:, d["last_row"], :]
        for h in range(GLA_HEADS):
            ksl = slice(h * GLA_DK, (h + 1) * GLA_DK)
            streams.append(dict(
                d=d, h=h, ksl=ksl, vsl=slice(h * GLA_DV, (h + 1) * GLA_DV),
                decay_t=jnp.transpose(jnp.exp2(b_last_rows[:, ksl])),
            ))

    bf = jnp.bfloat16
    zeros_k = jnp.zeros((CHUNK, GLA_DK), bf)

    def intra(st, step):
        d = st["d"]
        p = d["order"][step]
        rows = slice(p * pair, (p + 1) * pair)
        ia, ib = d["a"], d["b_"]
        b = d["b"][rows, st["ksl"]].reshape(2, CHUNK, GLA_DK)
        b_mid = b[:, d["ref_row"]:d["ref_row"] + 1, :]
        tot = b[:, d["last_row"]:d["last_row"] + 1, :]
        qe = (d["q"][rows, st["ksl"]].astype(jnp.float32).reshape(2, CHUNK, GLA_DK)
              * jnp.exp2(b - b_mid))
        ke = (d["k"][rows, st["ksl"]].astype(jnp.float32).reshape(2, CHUNK, GLA_DK)
              * jnp.exp2(b_mid - b))
        f_q = jnp.exp2(b_mid)
        f_k = jnp.exp2(tot - b_mid)
        q_in_a = (qe[ia] * f_q[ia]).astype(bf)
        q_in_b = (qe[ib] * f_q[ib]).astype(bf)
        q_in_b2 = (qe[ib] * (f_q[ib] * jnp.exp2(tot[ia]))).astype(bf)
        k_out_a = (ke[ia] * f_k[ia]).astype(bf)
        k_out_a2 = (ke[ia] * (f_k[ia] * jnp.exp2(tot[ib]))).astype(bf)
        k_out_b = (ke[ib] * f_k[ib]).astype(bf)
        ke_a = ke[ia].astype(bf)
        ke_b = ke[ib].astype(bf)
        nt = (((1,), (1,)), ((), ()))
        half = lambda x, slot: jnp.concatenate([x, zeros_k] if slot == 0 else [zeros_k, x], axis=0)
        att_a = lax.dot_general(qe[ia].astype(bf), half(ke_a, ia), nt,
                                preferred_element_type=jnp.float32)
        rhs_b = [None, None]
        rhs_b[ia] = jnp.concatenate([k_out_a, zeros_k], axis=1)
        rhs_b[ib] = jnp.concatenate([zeros_k, ke_b], axis=1)
        att_b = lax.dot_general(jnp.concatenate([q_in_b, qe[ib].astype(bf)], axis=1),
                                jnp.concatenate(rhs_b, axis=0), nt,
                                preferred_element_type=jnp.float32)
        q_rows = [None, None]
        q_rows[ia], q_rows[ib] = q_in_a, q_in_b2
        k_rows = [None, None]
        k_rows[ia], k_rows[ib] = k_out_a2, k_out_b
        return dict(p=p, rows=rows, att=(att_a, att_b), q_rows=q_rows,
                    k_pair=jnp.concatenate(k_rows, axis=0))

    ahead = {id(st): intra(st, 0) for st in streams}
    for step in range(npair):
        for st in streams:
            d = st["d"]
            cur = ahead[id(st)]
            if step + 1 < npair:
                ahead[id(st)] = intra(st, step + 1)
            ia, ib = d["a"], d["b_"]
            v = d["v"][cur["rows"], st["vsl"]]
            state = d["s"][st["h"]]
            att = [None, None]
            att[ia] = jnp.where(d["mask_a"], cur["att"][0], 0.0).astype(bf)
            att[ib] = jnp.where(d["mask_b"], cur["att"][1], 0.0).astype(bf)
            lhs = jnp.concatenate([jnp.concatenate([cur["q_rows"][0], att[0]], axis=1),
                                   jnp.concatenate([cur["q_rows"][1], att[1]], axis=1)], axis=0)
            rhs = jnp.concatenate([state.astype(bf), v], axis=0)
            o = jnp.dot(lhs, rhs, preferred_element_type=jnp.float32)
            d["o"][cur["rows"], st["vsl"]] = o.astype(d["o"].dtype)
            upd = lax.dot_general(cur["k_pair"], v, (((0,), (0,)), ((), ())),
                                  preferred_element_type=jnp.float32)
            c0 = 2 * cur["p"]
            decay = st["decay_t"][:, c0:c0 + 1] * st["decay_t"][:, c0 + 1:c0 + 2]
            d["s"][st["h"]] = decay * state + upd


def _out_body(x_ref, of_ref, ob_ref, sz_ref, yc_ref, gn_ref, wo_ref, fg_ref, out_ref):
    acc = jnp.dot(yc_ref[...], wo_ref[GLA_V_W:, :], preferred_element_type=jnp.float32)
    for h in range(GLA_HEADS):
        sl = slice(h * GLA_DV, (h + 1) * GLA_DV)
        o_h = of_ref[:, sl].astype(jnp.float32) + ob_ref[:, sl].astype(jnp.float32)
        ms = jnp.mean(o_h * o_h, axis=-1, keepdims=True)
        y_h = o_h * lax.rsqrt(ms + EPS) * gn_ref[...] * sz_ref[:, sl].astype(jnp.float32)
        acc = acc + jnp.dot(y_h.astype(jnp.bfloat16), wo_ref[sl, :],
                            preferred_element_type=jnp.float32)
    xo = x_ref[...] + acc
    ms = jnp.mean(xo * xo, axis=-1, keepdims=True)
    out_ref[...] = xo * lax.rsqrt(ms + EPS) * fg_ref[...]


def _mix_kernel(qf_ref, kf_ref, vf_ref, lrf_ref, qb_ref, kb_ref, vb_ref, lrb_ref,
                wgf_ref, wgb_ref, trif_ref, trib_ref,
                x_ref, sz_ref, yc_ref, gn_ref, wo_ref, fg_ref, out_ref, s_ref, o_ref):
    s = pl.program_id(1)
    nblk = pl.num_programs(1) // 2

    @pl.when(s < nblk)
    def _():
        _gla_body(qf_ref, kf_ref, vf_ref, lrf_ref, qb_ref, kb_ref, vb_ref, lrb_ref,
                  wgf_ref, wgb_ref, trif_ref, trib_ref,
                  o_ref.at[0, s], o_ref.at[1, nblk - 1 - s], s_ref, s == 0)

    @pl.when(s >= nblk)
    def _():
        _out_body(x_ref, o_ref.at[0, s - nblk], o_ref.at[1, s - nblk], sz_ref, yc_ref,
                  gn_ref, wo_ref, fg_ref, out_ref)


def _mix(q, k, v, lr, wg_f, wg_b, x2, sz, yc, gn, wo, fg, *, batch, seq):
    ts = TILE
    nblk = seq // ts
    gla_s = lambda s: jnp.minimum(s, nblk - 1)
    fwd = lambda b, s: (b * nblk + gla_s(s), 0)
    bwd = lambda b, s: (b * nblk + (nblk - 1 - gla_s(s)), 0)
    tile = lambda b, s: (b * nblk + jnp.maximum(s - nblk, 0), 0)
    rows = lambda idx: [pl.BlockSpec((ts, w), idx) for w in (GLA_QK_W, GLA_QK_W, GLA_V_W, LR_W)]
    const = lambda shp, **kw: pl.BlockSpec(shp, lambda b, s: (0, 0), **kw)
    return pl.pallas_call(
        _mix_kernel,
        grid=(batch, 2 * nblk),
        in_specs=(rows(fwd) + rows(bwd) + [const((LR_W, GLA_QK_W))] * 2
                  + [const((2 * CHUNK, 4 * CHUNK))] * 2
                  + [pl.BlockSpec((ts, D_MODEL), tile), pl.BlockSpec((ts, GLA_V_W), tile),
                     pl.BlockSpec((ts, CONV_W), tile), const((1, GLA_DV)),
                     const((GLA_V_W + CONV_W, D_MODEL), pipeline_mode=pl.Buffered(1)),
                     const((1, D_MODEL))]),
        out_specs=pl.BlockSpec((ts, D_MODEL), tile),
        out_shape=jax.ShapeDtypeStruct((batch * seq, D_MODEL), jnp.float32),
        scratch_shapes=[pltpu.VMEM((2, GLA_HEADS, GLA_DK, GLA_DV), jnp.float32),
                        pltpu.VMEM((2, nblk, ts, GLA_V_W), jnp.bfloat16)],
        compiler_params=pltpu.CompilerParams(
            dimension_semantics=("arbitrary", "arbitrary"), vmem_limit_bytes=VMEM_LIMIT),
        name="mix",
    )(q, k, v, lr, q, k, v, lr, wg_f, wg_b, _pair_tri(False), _pair_tri(True),
      x2, sz, yc, gn, wo, fg)


def _regroup_kernel(a_ref, b_ref, o_ref):
    j = pl.program_id(0)
    n_before = _OFF_BG // REGROUP_TILE
    n_main = _OFF_LR // REGROUP_TILE
    lr = 2 * GATE_RANK

    def put(rows):
        o_ref[...] = jnp.transpose(rows).astype(o_ref.dtype)

    @pl.when(j < n_before)
    def _():
        put(a_ref[...])

    @pl.when((j >= n_before) & (j < n_main))
    def _():
        put(jnp.concatenate([a_ref[lr:, :], b_ref[...]], axis=0))

    @pl.when(j == n_main)
    def _():
        put(jnp.concatenate([a_ref[:lr, :], jnp.zeros((REGROUP_TILE - lr, a_ref.shape[1]),
                                                      a_ref.dtype)], axis=0))


def _regroup_w_in(w):
    wt = jnp.swapaxes(w, 1, 2)
    _, n, k = wt.shape
    lr = 2 * GATE_RANK
    assert lr == REGROUP_HALO and _OFF_BG % REGROUP_TILE == 0 and _OFF_LR % REGROUP_TILE == 0
    n_main = _OFF_LR // REGROUP_TILE
    sub = REGROUP_TILE // REGROUP_HALO
    first = lambda j: jnp.where(j == n_main, _OFF_BG // REGROUP_TILE, j)
    return pl.pallas_call(
        _regroup_kernel,
        grid=(W_ALL // REGROUP_TILE,),
        in_specs=[pl.BlockSpec((None, REGROUP_TILE, k), lambda j: (0, first(j), 0)),
                  pl.BlockSpec((None, REGROUP_HALO, k),
                               lambda j: (0, jnp.minimum((j + 1) * sub, n // REGROUP_HALO - 1), 0))],
        out_specs=pl.BlockSpec((k, REGROUP_TILE), lambda j: (0, j)),
        out_shape=jax.ShapeDtypeStruct((k, W_ALL), jnp.bfloat16),
        compiler_params=pltpu.CompilerParams(dimension_semantics=("arbitrary",)),
        name="regroup",
    )(wt, wt)


def _gate_weight(w_gk, b_gk, first_row):
    bias = b_gk * LOG2_E
    bias_hi = bias.astype(jnp.bfloat16)
    bias_lo = (bias - bias_hi.astype(jnp.float32)).astype(jnp.bfloat16)
    full = jnp.zeros((LR_W, GLA_QK_W), jnp.bfloat16)
    full = lax.dynamic_update_slice(full, (w_gk * LOG2_E).astype(jnp.bfloat16), (first_row, 0))
    return lax.dynamic_update_slice(full, jnp.stack([bias_hi, bias_lo]), (ONES_COL, 0))


def kernel(x, norm_g, w_in, w_gk_f, b_gk_f, w_gk_b, b_gk_b, gla_norm_g, conv_w, conv_b, w_out, final_g):
    batch, seq, d = x.shape
    depth = w_in.shape[0]
    assert depth == 1 and d == D_MODEL
    assert seq % TILE == 0
    x2 = x.reshape(batch * seq, d)
    q, k, v, sz, yc, lr = _inproj(x2, norm_g[0][None, :], _regroup_w_in(w_in),
                                  conv_w[0], conv_b[0][None, :], seq=seq)
    out = _mix(q, k, v, lr, _gate_weight(w_gk_f[0], b_gk_f[0], 0),
               _gate_weight(w_gk_b[0], b_gk_b[0], GATE_RANK), x2, sz, yc,
               gla_norm_g[0][None, :], w_out[0].astype(jnp.bfloat16), final_g[None, :],
               batch=batch, seq=seq)
    return out.reshape(batch, seq, d)
```

```python
import functools

import jax
import jax.numpy as jnp
import numpy as np
from jax import lax
from jax.experimental import pallas as pl
from jax.experimental.pallas import tpu as pltpu

D_MODEL = 1024
GLA_HEADS = 4
GLA_DK = 128
GLA_DV = 256
GLA_QK_W = GLA_HEADS * GLA_DK
GLA_V_W = GLA_HEADS * GLA_DV
GATE_RANK = 16
GATE_NORM = 16.0
CHUNK = 64
CONV_W = 1024
EPS = 1e-6
LOG2_E = 1.4426950408889634

LANES = 128
HALO = 8
LR_W = LANES
ONES_COL = 2 * GATE_RANK

_OFF_Q = 0
_OFF_K = _OFF_Q + GLA_QK_W
_OFF_V = _OFF_K + GLA_QK_W
_OFF_ZA = _OFF_V + GLA_V_W
_OFF_BG = _OFF_ZA + GLA_V_W
_OFF_CG = _OFF_BG + CONV_W
_OFF_HC = _OFF_CG + CONV_W
_OFF_ZC = _OFF_HC + CONV_W
_OFF_LR = _OFF_ZC + CONV_W
REGROUP_TILE = 4 * LANES
REGROUP_HALO = 2 * GATE_RANK
W_ALL = _OFF_LR + REGROUP_TILE

TILE = 512
SUB_TILES = 2
VMEM_LIMIT = 56 * 1024 * 1024


def _silu(z):
    return z * (1.0 / (1.0 + jnp.exp(-z)))


def _inproj_kernel(x_ref, xp_ref, xn_ref, g_ref, w_ref, cw_ref, cb_ref,
                   q_ref, k_ref, v_ref, sz_ref, yc_ref, lr_ref, *, steps_per_seq):
    step = pl.program_id(0) % steps_per_seq

    def normed(x):
        ms = jnp.mean(x * x, axis=-1, keepdims=True)
        return x * lax.rsqrt(ms + EPS) * g_ref[...]

    def sub_tile(r, carry):
        row0 = pl.multiple_of(r * TILE, TILE)
        rows = pl.ds(row0, TILE)
        first = r == 0
        last = r == SUB_TILES - 1
        h = normed(x_ref[rows, :]).astype(jnp.bfloat16)
        inner_before = x_ref[pl.ds(pl.multiple_of(jnp.maximum(row0 - HALO, 0), HALO), HALO), :]
        inner_after = x_ref[pl.ds(pl.multiple_of(jnp.minimum(row0 + TILE, x_ref.shape[0] - HALO),
                                                 HALO), HALO), :]
        before = jnp.where(first, xp_ref[...], inner_before)
        after = jnp.where(last, xn_ref[...], inner_after)
        halo = jnp.concatenate([normed(before), normed(after)], axis=0).astype(jnp.bfloat16)
        h_ext = jnp.concatenate([h, halo], axis=0)

        def proj(lo, hi, lhs=h):
            return jnp.dot(lhs, w_ref[:, lo:hi], preferred_element_type=jnp.float32)

        u_ext = proj(_OFF_CG, _OFF_HC, h_ext) * proj(_OFF_HC, _OFF_ZC, h_ext)
        u = u_ext[:TILE]
        prev_row = jnp.where(first & (step == 0), 0.0, u_ext[TILE + HALO - 1:TILE + HALO])
        next_row = jnp.where(last & (step == steps_per_seq - 1), 0.0,
                             u_ext[TILE + HALO:TILE + HALO + 1])
        t_i = lax.broadcasted_iota(jnp.int32, (TILE, 1), 0)
        u_prev = jnp.where(t_i == 0, prev_row, pltpu.roll(u, 1, 0))
        u_next = jnp.where(t_i == TILE - 1, next_row, pltpu.roll(u, TILE - 1, 0))
        conv = cw_ref[0:1, :] * u_prev + cw_ref[1:2, :] * u + cw_ref[2:3, :] * u_next + cb_ref[...]
        yc_ref[rows, :] = (proj(_OFF_BG, _OFF_CG) * conv * _silu(proj(_OFF_ZC, _OFF_LR))
                           ).astype(yc_ref.dtype)

        sz_ref[rows, :] = _silu(proj(_OFF_ZA, _OFF_BG)).astype(sz_ref.dtype)
        q_ref[rows, :] = (proj(_OFF_Q, _OFF_K) * (GLA_DK ** -0.5)).astype(q_ref.dtype)
        k_ref[rows, :] = proj(_OFF_K, _OFF_V).astype(k_ref.dtype)
        v_ref[rows, :] = proj(_OFF_V, _OFF_ZA).astype(v_ref.dtype)
        col = lax.broadcasted_iota(jnp.int32, (1, LR_W), 1)
        ones = jnp.where((col == ONES_COL) | (col == ONES_COL + 1), 1.0, 0.0)
        lr_ref[rows, :] = proj(_OFF_LR, _OFF_LR + LR_W) + ones
        return carry

    lax.fori_loop(0, SUB_TILES, sub_tile, 0)


def _inproj(x2, norm_g, w_all, conv_w, conv_b, *, seq):
    m = x2.shape[0]
    tm = SUB_TILES * TILE
    assert seq % tm == 0
    sub = tm // HALO
    nsub = m // HALO
    row = lambda w: pl.BlockSpec((tm, w), lambda i: (i, 0))
    const = lambda shp: pl.BlockSpec(shp, lambda i: (0, 0))
    halo_prev = pl.BlockSpec((HALO, D_MODEL), lambda i: (jnp.maximum(i * sub - 1, 0), 0))
    halo_next = pl.BlockSpec((HALO, D_MODEL), lambda i: (jnp.minimum((i + 1) * sub, nsub - 1), 0))
    bf = jnp.bfloat16
    return pl.pallas_call(
        functools.partial(_inproj_kernel, steps_per_seq=seq // tm),
        grid=(m // tm,),
        in_specs=[
            row(D_MODEL), halo_prev, halo_next,
            const((1, D_MODEL)),
            pl.BlockSpec((D_MODEL, W_ALL), lambda i: (0, 0), pipeline_mode=pl.Buffered(1)),
            const((3, CONV_W)), const((1, CONV_W)),
        ],
        out_specs=[row(GLA_QK_W), row(GLA_QK_W), row(GLA_V_W), row(GLA_V_W),
                   row(CONV_W), row(LR_W)],
        out_shape=[
            jax.ShapeDtypeStruct((m, GLA_QK_W), bf),
            jax.ShapeDtypeStruct((m, GLA_QK_W), bf),
            jax.ShapeDtypeStruct((m, GLA_V_W), bf),
            jax.ShapeDtypeStruct((m, GLA_V_W), bf),
            jax.ShapeDtypeStruct((m, CONV_W), bf),
            jax.ShapeDtypeStruct((m, LR_W), jnp.float32),
        ],
        compiler_params=pltpu.CompilerParams(
            dimension_semantics=("arbitrary",), vmem_limit_bytes=VMEM_LIMIT),
        name="inproj",
    )(x2, x2, x2, norm_g, w_all, conv_w, conv_b)


def _pair_tri(rev):
    t = np.arange(CHUNK)
    tri = (t[None, :] >= t[:, None]) if rev else (t[None, :] <= t[:, None])
    zero = np.zeros_like(tri)
    keep = np.block([[tri, tri, zero, zero], [zero, zero, tri, tri]])
    return jnp.asarray(keep / GATE_NORM, dtype=jnp.bfloat16)


def _gate_terms(lr_ref, wg_ref):
    x = jnp.dot(lr_ref[...].astype(jnp.bfloat16), wg_ref[...], preferred_element_type=jnp.float32)
    return jnp.minimum(x, 0.0) - jnp.log2(1.0 + jnp.exp2(-jnp.abs(x)))


def _cumsum_into(g, tri_ref, b_ref):
    ts = g.shape[0]
    tri = tri_ref[...]
    g_top = lax.bitcast_convert_type(
        lax.bitcast_convert_type(g, jnp.uint32) & jnp.uint32(0xFFFF0000), jnp.float32)
    g_hi = g_top.astype(jnp.bfloat16)
    g_lo = (g - g_top).astype(jnp.bfloat16)
    for p in range(ts // (2 * CHUNK)):
        c0 = slice(2 * p * CHUNK, (2 * p + 1) * CHUNK)
        c1 = slice((2 * p + 1) * CHUNK, (2 * p + 2) * CHUNK)
        rhs = jnp.concatenate([g_hi[c0], g_lo[c0], g_hi[c1], g_lo[c1]], axis=0)
        b_ref[2 * p * CHUNK:(2 * p + 2) * CHUNK, :] = jnp.dot(
            tri, rhs, preferred_element_type=jnp.float32)


def _gla_body(qf_ref, kf_ref, vf_ref, bf_ref, qb_ref, kb_ref, vb_ref, bb_ref,
              of_ref, ob_ref, s_ref, first_block, mid_round, after_rounds):
    ts = qf_ref.shape[0]
    nc = ts // CHUNK
    npair = nc // 2
    pair = 2 * CHUNK

    @pl.when(first_block)
    def _():
        s_ref[...] = jnp.zeros_like(s_ref)

    ri = lax.broadcasted_iota(jnp.int32, (CHUNK, pair), 0)
    li = lax.broadcasted_iota(jnp.int32, (CHUNK, pair), 1)
    lo_half = li < CHUNK
    dirs = (
        dict(q=qf_ref, k=kf_ref, v=vf_ref, o=of_ref, s=s_ref.at[0], a=0, b_=1,
             mask_a=lo_half & (li <= ri), mask_b=lo_half | (li - CHUNK <= ri),
             ref_row=CHUNK // 2, last_row=CHUNK - 1, order=tuple(range(npair)),
             b=bf_ref),
        dict(q=qb_ref, k=kb_ref, v=vb_ref, o=ob_ref, s=s_ref.at[1], a=1, b_=0,
             mask_a=(~lo_half) & (li - CHUNK > ri), mask_b=(~lo_half) | (li > ri),
             ref_row=CHUNK - 1 - CHUNK // 2, last_row=0, order=tuple(range(npair - 1, -1, -1)),
             b=bb_ref),
    )

    streams = []
    for d in dirs:
        b_last_rows = jnp.concatenate(
            [d["b"][c * CHUNK + d["last_row"]:c * CHUNK + d["last_row"] + 1, :] for c in range(nc)],
            axis=0)
        for h in range(GLA_HEADS):
            ksl = slice(h * GLA_DK, (h + 1) * GLA_DK)
            streams.append(dict(
                d=d, h=h, ksl=ksl, vsl=slice(h * GLA_DV, (h + 1) * GLA_DV),
                decay_t=jnp.transpose(jnp.exp2(b_last_rows[:, ksl])),
            ))

    bf = jnp.bfloat16
    zeros_k = jnp.zeros((CHUNK, GLA_DK), bf)

    def intra(st, step):
        d = st["d"]
        p = d["order"][step]
        rows = slice(p * pair, (p + 1) * pair)
        ia, ib = d["a"], d["b_"]
        b = d["b"][rows, st["ksl"]].reshape(2, CHUNK, GLA_DK)
        b_mid = b[:, d["ref_row"]:d["ref_row"] + 1, :]
        tot = b[:, d["last_row"]:d["last_row"] + 1, :]
        qe = (d["q"][rows, st["ksl"]].astype(jnp.float32).reshape(2, CHUNK, GLA_DK)
              * jnp.exp2(b - b_mid))
        ke = (d["k"][rows, st["ksl"]].astype(jnp.float32).reshape(2, CHUNK, GLA_DK)
              * jnp.exp2(b_mid - b))
        f_q = jnp.exp2(b_mid)
        f_k = jnp.exp2(tot - b_mid)
        q_in_a = (qe[ia] * f_q[ia]).astype(bf)
        q_in_b = (qe[ib] * f_q[ib]).astype(bf)
        q_in_b2 = (qe[ib] * (f_q[ib] * jnp.exp2(tot[ia]))).astype(bf)
        k_out_a = (ke[ia] * f_k[ia]).astype(bf)
        k_out_a2 = (ke[ia] * (f_k[ia] * jnp.exp2(tot[ib]))).astype(bf)
        k_out_b = (ke[ib] * f_k[ib]).astype(bf)
        ke_a = ke[ia].astype(bf)
        ke_b = ke[ib].astype(bf)
        nt = (((1,), (1,)), ((), ()))
        half = lambda x, slot: jnp.concatenate([x, zeros_k] if slot == 0 else [zeros_k, x], axis=0)
        att_a = lax.dot_general(qe[ia].astype(bf), half(ke_a, ia), nt,
                                preferred_element_type=jnp.float32)
        rhs_b = [None, None]
        rhs_b[ia] = jnp.concatenate([k_out_a, zeros_k], axis=1)
        rhs_b[ib] = jnp.concatenate([zeros_k, ke_b], axis=1)
        att_b = lax.dot_general(jnp.concatenate([q_in_b, qe[ib].astype(bf)], axis=1),
                                jnp.concatenate(rhs_b, axis=0), nt,
                                preferred_element_type=jnp.float32)
        q_rows = [None, None]
        q_rows[ia], q_rows[ib] = q_in_a, q_in_b2
        k_rows = [None, None]
        k_rows[ia], k_rows[ib] = k_out_a2, k_out_b
        return dict(p=p, rows=rows, att=(att_a, att_b), q_rows=q_rows,
                    k_pair=jnp.concatenate(k_rows, axis=0))

    ahead = {id(st): intra(st, 0) for st in streams}
    for step in range(npair):
        if step == npair // 2:
            mid_round()
        for st in streams:
            d = st["d"]
            cur = ahead[id(st)]
            if step + 1 < npair:
                ahead[id(st)] = intra(st, step + 1)
            ia, ib = d["a"], d["b_"]
            v = d["v"][cur["rows"], st["vsl"]]
            state = d["s"][st["h"]]
            att = [None, None]
            att[ia] = jnp.where(d["mask_a"], cur["att"][0], 0.0).astype(bf)
            att[ib] = jnp.where(d["mask_b"], cur["att"][1], 0.0).astype(bf)
            lhs = jnp.concatenate([jnp.concatenate([cur["q_rows"][0], att[0]], axis=1),
                                   jnp.concatenate([cur["q_rows"][1], att[1]], axis=1)], axis=0)
            rhs = jnp.concatenate([state.astype(bf), v], axis=0)
            o = jnp.dot(lhs, rhs, preferred_element_type=jnp.float32)
            d["o"][cur["rows"], st["vsl"]] = o.astype(d["o"].dtype)
            upd = lax.dot_general(cur["k_pair"], v, (((0,), (0,)), ((), ())),
                                  preferred_element_type=jnp.float32)
            c0 = 2 * cur["p"]
            decay = st["decay_t"][:, c0:c0 + 1] * st["decay_t"][:, c0 + 1:c0 + 2]
            d["s"][st["h"]] = decay * state + upd
    after_rounds()


def _out_body(x_ref, of_ref, ob_ref, sz_ref, yc_ref, gn_ref, wo_ref, fg_ref, out_ref):
    acc = jnp.dot(yc_ref[...], wo_ref[GLA_V_W:, :], preferred_element_type=jnp.float32)
    for h in range(GLA_HEADS):
        sl = slice(h * GLA_DV, (h + 1) * GLA_DV)
        o_h = of_ref[:, sl].astype(jnp.float32) + ob_ref[:, sl].astype(jnp.float32)
        ms = jnp.mean(o_h * o_h, axis=-1, keepdims=True)
        y_h = o_h * lax.rsqrt(ms + EPS) * gn_ref[...] * sz_ref[:, sl].astype(jnp.float32)
        acc = acc + jnp.dot(y_h.astype(jnp.bfloat16), wo_ref[sl, :],
                            preferred_element_type=jnp.float32)
    xo = x_ref[...] + acc
    ms = jnp.mean(xo * xo, axis=-1, keepdims=True)
    out_ref[...] = xo * lax.rsqrt(ms + EPS) * fg_ref[...]


def _mix_kernel(qf_ref, kf_ref, vf_ref, lrf_ref, qb_ref, kb_ref, vb_ref, lrb_ref,
                lrf_next_ref, lrb_next_ref, wgf_ref, wgb_ref, trif_ref, trib_ref,
                x_ref, sz_ref, yc_ref, gn_ref, wo_ref, fg_ref, out_ref, s_ref, o_ref, b_ref):
    s = pl.program_id(1)
    nblk = pl.num_programs(1) // 2
    slot = s % 2

    @pl.when((pl.program_id(0) == 0) & (s == 0))
    def _():
        _cumsum_into(_gate_terms(lrf_ref, wgf_ref), trif_ref, b_ref.at[0, 0])
        _cumsum_into(_gate_terms(lrb_ref, wgb_ref), trib_ref, b_ref.at[0, 1])

    @pl.when(s < nblk)
    def _():
        g_next = {}

        def gates_next():
            g_next["f"] = _gate_terms(lrf_next_ref, wgf_ref)
            g_next["b"] = _gate_terms(lrb_next_ref, wgb_ref)

        def sums_next():
            _cumsum_into(g_next.pop("f"), trif_ref, b_ref.at[1 - slot, 0])
            _cumsum_into(g_next.pop("b"), trib_ref, b_ref.at[1 - slot, 1])

        _gla_body(qf_ref, kf_ref, vf_ref, b_ref.at[slot, 0], qb_ref, kb_ref, vb_ref,
                  b_ref.at[slot, 1], o_ref.at[0, s], o_ref.at[1, nblk - 1 - s], s_ref, s == 0,
                  gates_next, sums_next)

    @pl.when(s >= nblk)
    def _():
        _out_body(x_ref, o_ref.at[0, s - nblk], o_ref.at[1, s - nblk], sz_ref, yc_ref,
                  gn_ref, wo_ref, fg_ref, out_ref)


def _mix(q, k, v, lr, wg_f, wg_b, x2, sz, yc, gn, wo, fg, *, batch, seq):
    ts = TILE
    nblk = seq // ts
    assert nblk % 2 == 0
    gla_s = lambda s: jnp.minimum(s, nblk - 1)

    def following(b, s):
        nxt = jnp.minimum(b * nblk + gla_s(s) + 1, batch * nblk - 1)
        return nxt // nblk, nxt % nblk

    def fwd_next(b, s):
        b1, s1 = following(b, s)
        return (b1 * nblk + s1, 0)

    def bwd_next(b, s):
        b1, s1 = following(b, s)
        return (b1 * nblk + (nblk - 1 - s1), 0)

    fwd = lambda b, s: (b * nblk + gla_s(s), 0)
    bwd = lambda b, s: (b * nblk + (nblk - 1 - gla_s(s)), 0)
    tile = lambda b, s: (b * nblk + jnp.maximum(s - nblk, 0), 0)
    rows = lambda idx: [pl.BlockSpec((ts, w), idx) for w in (GLA_QK_W, GLA_QK_W, GLA_V_W, LR_W)]
    const = lambda shp, **kw: pl.BlockSpec(shp, lambda b, s: (0, 0), **kw)
    return pl.pallas_call(
        _mix_kernel,
        grid=(batch, 2 * nblk),
        in_specs=(rows(fwd) + rows(bwd)
                  + [pl.BlockSpec((ts, LR_W), fwd_next), pl.BlockSpec((ts, LR_W), bwd_next)]
                  + [const((LR_W, GLA_QK_W))] * 2
                  + [const((2 * CHUNK, 4 * CHUNK))] * 2
                  + [pl.BlockSpec((ts, D_MODEL), tile), pl.BlockSpec((ts, GLA_V_W), tile),
                     pl.BlockSpec((ts, CONV_W), tile), const((1, GLA_DV)),
                     const((GLA_V_W + CONV_W, D_MODEL), pipeline_mode=pl.Buffered(1)),
                     const((1, D_MODEL))]),
        out_specs=pl.BlockSpec((ts, D_MODEL), tile),
        out_shape=jax.ShapeDtypeStruct((batch * seq, D_MODEL), jnp.float32),
        scratch_shapes=[pltpu.VMEM((2, GLA_HEADS, GLA_DK, GLA_DV), jnp.float32),
                        pltpu.VMEM((2, nblk, ts, GLA_V_W), jnp.bfloat16),
                        pltpu.VMEM((2, 2, ts, GLA_QK_W), jnp.float32)],
        compiler_params=pltpu.CompilerParams(
            dimension_semantics=("arbitrary", "arbitrary"), vmem_limit_bytes=VMEM_LIMIT),
        name="mix",
    )(q, k, v, lr, q, k, v, lr, lr, lr, wg_f, wg_b, _pair_tri(False), _pair_tri(True),
      x2, sz, yc, gn, wo, fg)


def _regroup_kernel(a_ref, b_ref, o_ref):
    j = pl.program_id(0)
    n_before = _OFF_BG // REGROUP_TILE
    n_main = _OFF_LR // REGROUP_TILE
    lr = 2 * GATE_RANK

    def put(rows):
        o_ref[...] = jnp.transpose(rows).astype(o_ref.dtype)

    @pl.when(j < n_before)
    def _():
        put(a_ref[...])

    @pl.when((j >= n_before) & (j < n_main))
    def _():
        put(jnp.concatenate([a_ref[lr:, :], b_ref[...]], axis=0))

    @pl.when(j == n_main)
    def _():
        put(jnp.concatenate([a_ref[:lr, :], jnp.zeros((REGROUP_TILE - lr, a_ref.shape[1]),
                                                      a_ref.dtype)], axis=0))


def _regroup_w_in(w):
    wt = jnp.swapaxes(w, 1, 2)
    _, n, k = wt.shape
    lr = 2 * GATE_RANK
    assert lr == REGROUP_HALO and _OFF_BG % REGROUP_TILE == 0 and _OFF_LR % REGROUP_TILE == 0
    n_main = _OFF_LR // REGROUP_TILE
    sub = REGROUP_TILE // REGROUP_HALO
    first = lambda j: jnp.where(j == n_main, _OFF_BG // REGROUP_TILE, j)
    return pl.pallas_call(
        _regroup_kernel,
        grid=(W_ALL // REGROUP_TILE,),
        in_specs=[pl.BlockSpec((None, REGROUP_TILE, k), lambda j: (0, first(j), 0)),
                  pl.BlockSpec((None, REGROUP_HALO, k),
                               lambda j: (0, jnp.minimum((j + 1) * sub, n // REGROUP_HALO - 1), 0))],
        out_specs=pl.BlockSpec((k, REGROUP_TILE), lambda j: (0, j)),
        out_shape=jax.ShapeDtypeStruct((k, W_ALL), jnp.bfloat16),
        compiler_params=pltpu.CompilerParams(dimension_semantics=("arbitrary",)),
        name="regroup",
    )(wt, wt)


def _gate_weight(w_gk, b_gk, first_row):
    bias = b_gk * LOG2_E
    bias_hi = bias.astype(jnp.bfloat16)
    bias_lo = (bias - bias_hi.astype(jnp.float32)).astype(jnp.bfloat16)
    full = jnp.zeros((LR_W, GLA_QK_W), jnp.bfloat16)
    full = lax.dynamic_update_slice(full, (w_gk * LOG2_E).astype(jnp.bfloat16), (first_row, 0))
    return lax.dynamic_update_slice(full, jnp.stack([bias_hi, bias_lo]), (ONES_COL, 0))


def kernel(x, norm_g, w_in, w_gk_f, b_gk_f, w_gk_b, b_gk_b, gla_norm_g, conv_w, conv_b, w_out, final_g):
    batch, seq, d = x.shape
    depth = w_in.shape[0]
    assert depth == 1 and d == D_MODEL
    assert seq % TILE == 0
    x2 = x.reshape(batch * seq, d)
    q, k, v, sz, yc, lr = _inproj(x2, norm_g[0][None, :], _regroup_w_in(w_in),
                                  conv_w[0], conv_b[0][None, :], seq=seq)
    out = _mix(q, k, v, lr, _gate_weight(w_gk_f[0], b_gk_f[0], 0),
               _gate_weight(w_gk_b[0], b_gk_b[0], GATE_RANK), x2, sz, yc,
               gla_norm_g[0][None, :], w_out[0].astype(jnp.bfloat16), final_g[None, :],
               batch=batch, seq=seq)
    return out.reshape(batch, seq, d)
```

```python
import functools

import jax
import jax.numpy as jnp
import numpy as np
from jax import lax
from jax.experimental import pallas as pl
from jax.experimental.pallas import tpu as pltpu

D_MODEL = 1024
GLA_HEADS = 4
GLA_DK = 128
GLA_DV = 256
GLA_QK_W = GLA_HEADS * GLA_DK
GLA_V_W = GLA_HEADS * GLA_DV
GATE_RANK = 16
GATE_NORM = 16.0
CHUNK = 64
CONV_W = 1024
EPS = 1e-6
LOG2_E = 1.4426950408889634

LANES = 128
HALO = 8
LR_W = LANES
ONES_COL = 2 * GATE_RANK

_OFF_Q = 0
_OFF_K = _OFF_Q + GLA_QK_W
_OFF_V = _OFF_K + GLA_QK_W
_OFF_ZA = _OFF_V + GLA_V_W
_OFF_BG = _OFF_ZA + GLA_V_W
_OFF_CG = _OFF_BG + CONV_W
_OFF_HC = _OFF_CG + CONV_W
_OFF_ZC = _OFF_HC + CONV_W
_OFF_LR = _OFF_ZC + CONV_W
REGROUP_TILE = 4 * LANES
REGROUP_HALO = 2 * GATE_RANK
W_ALL = _OFF_LR + REGROUP_TILE

TILE = 512
VMEM_LIMIT = 58 * 1024 * 1024
SCORE_LEAD = 12


def _silu(z):
    return z * (1.0 / (1.0 + jnp.exp(-z)))


def _inproj_kernel(x_ref, xp_ref, xn_ref, g_ref, w_ref, cw_ref, cb_ref,
                   q_ref, k_ref, v_ref, sz_ref, yc_ref, lr_ref, *, tiles_per_seq):
    tm = x_ref.shape[0]
    pos = pl.program_id(0) % tiles_per_seq

    def normed(x):
        ms = jnp.mean(x * x, axis=-1, keepdims=True)
        return x * lax.rsqrt(ms + EPS) * g_ref[...]

    h = normed(x_ref[...]).astype(jnp.bfloat16)
    halo = jnp.concatenate([normed(xp_ref[...]), normed(xn_ref[...])], axis=0).astype(jnp.bfloat16)
    h_ext = jnp.concatenate([h, halo], axis=0)

    def proj(lo, hi, lhs=h):
        return jnp.dot(lhs, w_ref[:, lo:hi], preferred_element_type=jnp.float32)

    u_ext = proj(_OFF_CG, _OFF_HC, h_ext) * proj(_OFF_HC, _OFF_ZC, h_ext)
    u = u_ext[:tm]
    prev_row = jnp.where(pos == 0, 0.0, u_ext[tm + HALO - 1:tm + HALO])
    next_row = jnp.where(pos == tiles_per_seq - 1, 0.0, u_ext[tm + HALO:tm + HALO + 1])
    t_i = lax.broadcasted_iota(jnp.int32, (tm, 1), 0)
    u_prev = jnp.where(t_i == 0, prev_row, pltpu.roll(u, 1, 0))
    u_next = jnp.where(t_i == tm - 1, next_row, pltpu.roll(u, tm - 1, 0))
    conv = cw_ref[0:1, :] * u_prev + cw_ref[1:2, :] * u + cw_ref[2:3, :] * u_next + cb_ref[...]
    yc_ref[...] = (proj(_OFF_BG, _OFF_CG) * conv * _silu(proj(_OFF_ZC, _OFF_LR))).astype(yc_ref.dtype)

    sz_ref[...] = _silu(proj(_OFF_ZA, _OFF_BG)).astype(sz_ref.dtype)
    q_ref[...] = (proj(_OFF_Q, _OFF_K) * (GLA_DK ** -0.5)).astype(q_ref.dtype)
    k_ref[...] = proj(_OFF_K, _OFF_V).astype(k_ref.dtype)
    v_ref[...] = proj(_OFF_V, _OFF_ZA).astype(v_ref.dtype)
    col = lax.broadcasted_iota(jnp.int32, (1, LR_W), 1)
    ones = jnp.where((col == ONES_COL) | (col == ONES_COL + 1), 1.0, 0.0)
    lr_ref[...] = proj(_OFF_LR, _OFF_LR + LR_W) + ones


def _inproj(x2, norm_g, w_all, conv_w, conv_b, *, seq):
    m = x2.shape[0]
    tm = TILE
    sub = tm // HALO
    nsub = m // HALO
    row = lambda w: pl.BlockSpec((tm, w), lambda i: (i, 0))
    const = lambda shp: pl.BlockSpec(shp, lambda i: (0, 0))
    halo_prev = pl.BlockSpec((HALO, D_MODEL), lambda i: (jnp.maximum(i * sub - 1, 0), 0))
    halo_next = pl.BlockSpec((HALO, D_MODEL), lambda i: (jnp.minimum((i + 1) * sub, nsub - 1), 0))
    bf = jnp.bfloat16
    return pl.pallas_call(
        functools.partial(_inproj_kernel, tiles_per_seq=seq // tm),
        grid=(m // tm,),
        in_specs=[
            row(D_MODEL), halo_prev, halo_next,
            const((1, D_MODEL)),
            pl.BlockSpec((D_MODEL, W_ALL), lambda i: (0, 0), pipeline_mode=pl.Buffered(1)),
            const((3, CONV_W)), const((1, CONV_W)),
        ],
        out_specs=[row(GLA_QK_W), row(GLA_QK_W), row(GLA_V_W), row(GLA_V_W),
                   row(CONV_W), row(LR_W)],
        out_shape=[
            jax.ShapeDtypeStruct((m, GLA_QK_W), bf),
            jax.ShapeDtypeStruct((m, GLA_QK_W), bf),
            jax.ShapeDtypeStruct((m, GLA_V_W), bf),
            jax.ShapeDtypeStruct((m, GLA_V_W), bf),
            jax.ShapeDtypeStruct((m, CONV_W), bf),
            jax.ShapeDtypeStruct((m, LR_W), jnp.float32),
        ],
        compiler_params=pltpu.CompilerParams(
            dimension_semantics=("arbitrary",), vmem_limit_bytes=VMEM_LIMIT),
        name="inproj",
    )(x2, x2, x2, norm_g, w_all, conv_w, conv_b)


def _pair_tri(rev):
    t = np.arange(CHUNK)
    tri = (t[None, :] >= t[:, None]) if rev else (t[None, :] <= t[:, None])
    zero = np.zeros_like(tri)
    keep = np.block([[tri, tri, zero, zero], [zero, zero, tri, tri]])
    return jnp.asarray(keep / GATE_NORM, dtype=jnp.bfloat16)


def _gate_terms_into(lr_ref, wg_ref, gs_ref):
    ts = lr_ref.shape[0]
    x = jnp.dot(lr_ref[...].astype(jnp.bfloat16), wg_ref[...], preferred_element_type=jnp.float32)
    g = jnp.minimum(x, 0.0) - jnp.log2(1.0 + jnp.exp2(-jnp.abs(x)))
    g_top = lax.bitcast_convert_type(
        lax.bitcast_convert_type(g, jnp.uint32) & jnp.uint32(0xFFFF0000), jnp.float32)
    g_hi = g_top.astype(gs_ref.dtype)
    g_lo = (g - g_top).astype(gs_ref.dtype)
    for c in range(ts // CHUNK):
        rows = slice(c * CHUNK, (c + 1) * CHUNK)
        gs_ref[2 * c * CHUNK:(2 * c + 1) * CHUNK, :] = g_hi[rows]
        gs_ref[(2 * c + 1) * CHUNK:(2 * c + 2) * CHUNK, :] = g_lo[rows]


def _cumsum_into(gs_ref, tri_ref, b_ref):
    tri = tri_ref[...]
    for p in range(b_ref.shape[0] // (2 * CHUNK)):
        b_ref[2 * p * CHUNK:(2 * p + 2) * CHUNK, :] = jnp.dot(
            tri, gs_ref[4 * p * CHUNK:(4 * p + 4) * CHUNK, :], preferred_element_type=jnp.float32)


def _gla_body(qf_ref, kf_ref, vf_ref, bf_ref, qb_ref, kb_ref, vb_ref, bb_ref,
              of_ref, ob_ref, s_ref, first_block, mid_round, after_rounds):
    ts = qf_ref.shape[0]
    nc = ts // CHUNK
    npair = nc // 2
    pair = 2 * CHUNK

    @pl.when(first_block)
    def _():
        s_ref[...] = jnp.zeros_like(s_ref)

    ri = lax.broadcasted_iota(jnp.int32, (CHUNK, pair), 0)
    li = lax.broadcasted_iota(jnp.int32, (CHUNK, pair), 1)
    lo_half = li < CHUNK
    dirs = (
        dict(q=qf_ref, k=kf_ref, v=vf_ref, o=of_ref, s=s_ref.at[0], a=0, b_=1,
             mask_a=lo_half & (li <= ri), mask_b=lo_half | (li - CHUNK <= ri),
             ref_row=CHUNK // 2, last_row=CHUNK - 1, order=tuple(range(npair)),
             b=bf_ref),
        dict(q=qb_ref, k=kb_ref, v=vb_ref, o=ob_ref, s=s_ref.at[1], a=1, b_=0,
             mask_a=(~lo_half) & (li - CHUNK > ri), mask_b=(~lo_half) | (li > ri),
             ref_row=CHUNK - 1 - CHUNK // 2, last_row=0, order=tuple(range(npair - 1, -1, -1)),
             b=bb_ref),
    )

    streams = []
    for d in dirs:
        b_last_rows = jnp.concatenate(
            [d["b"][c * CHUNK + d["last_row"]:c * CHUNK + d["last_row"] + 1, :] for c in range(nc)],
            axis=0)
        for h in range(GLA_HEADS):
            ksl = slice(h * GLA_DK, (h + 1) * GLA_DK)
            streams.append(dict(
                d=d, h=h, ksl=ksl, vsl=slice(h * GLA_DV, (h + 1) * GLA_DV),
                decay_t=jnp.transpose(jnp.exp2(b_last_rows[:, ksl])),
            ))

    bf = jnp.bfloat16
    zeros_k = jnp.zeros((CHUNK, GLA_DK), bf)

    def intra(st, step):
        d = st["d"]
        p = d["order"][step]
        rows = slice(p * pair, (p + 1) * pair)
        ia, ib = d["a"], d["b_"]
        b = d["b"][rows, st["ksl"]].reshape(2, CHUNK, GLA_DK)
        b_mid = b[:, d["ref_row"]:d["ref_row"] + 1, :]
        tot = b[:, d["last_row"]:d["last_row"] + 1, :]
        qe = (d["q"][rows, st["ksl"]].astype(jnp.float32).reshape(2, CHUNK, GLA_DK)
              * jnp.exp2(b - b_mid))
        ke = (d["k"][rows, st["ksl"]].astype(jnp.float32).reshape(2, CHUNK, GLA_DK)
              * jnp.exp2(b_mid - b))
        f_q = jnp.exp2(b_mid)
        f_k = jnp.exp2(tot - b_mid)
        q_in_a = (qe[ia] * f_q[ia]).astype(bf)
        q_in_b = (qe[ib] * f_q[ib]).astype(bf)
        q_in_b2 = (qe[ib] * (f_q[ib] * jnp.exp2(tot[ia]))).astype(bf)
        k_out_a = (ke[ia] * f_k[ia]).astype(bf)
        k_out_a2 = (ke[ia] * (f_k[ia] * jnp.exp2(tot[ib]))).astype(bf)
        k_out_b = (ke[ib] * f_k[ib]).astype(bf)
        ke_a = ke[ia].astype(bf)
        ke_b = ke[ib].astype(bf)
        nt = (((1,), (1,)), ((), ()))
        half = lambda x, slot: jnp.concatenate([x, zeros_k] if slot == 0 else [zeros_k, x], axis=0)
        att_a = lax.dot_general(qe[ia].astype(bf), half(ke_a, ia), nt,
                                preferred_element_type=jnp.float32)
        rhs_b = [None, None]
        rhs_b[ia] = jnp.concatenate([k_out_a, zeros_k], axis=1)
        rhs_b[ib] = jnp.concatenate([zeros_k, ke_b], axis=1)
        att_b = lax.dot_general(jnp.concatenate([q_in_b, qe[ib].astype(bf)], axis=1),
                                jnp.concatenate(rhs_b, axis=0), nt,
                                preferred_element_type=jnp.float32)
        q_rows = [None, None]
        q_rows[ia], q_rows[ib] = q_in_a, q_in_b2
        k_rows = [None, None]
        k_rows[ia], k_rows[ib] = k_out_a2, k_out_b
        return dict(p=p, rows=rows, att=(att_a, att_b), q_rows=q_rows,
                    k_pair=jnp.concatenate(k_rows, axis=0))

    items = [(step, st) for step in range(npair) for st in streams]
    ahead = {(step, id(st)): intra(st, step) for step, st in items[:SCORE_LEAD]}
    for i, (step, st) in enumerate(items):
        if step == npair // 2 and st is streams[0]:
            mid_round()
        d = st["d"]
        cur = ahead.pop((step, id(st)))
        if i + SCORE_LEAD < len(items):
            nstep, nst = items[i + SCORE_LEAD]
            ahead[(nstep, id(nst))] = intra(nst, nstep)
        ia, ib = d["a"], d["b_"]
        v = d["v"][cur["rows"], st["vsl"]]
        state = d["s"][st["h"]]
        att = [None, None]
        att[ia] = jnp.where(d["mask_a"], cur["att"][0], 0.0).astype(bf)
        att[ib] = jnp.where(d["mask_b"], cur["att"][1], 0.0).astype(bf)
        lhs = jnp.concatenate([jnp.concatenate([cur["q_rows"][0], att[0]], axis=1),
                               jnp.concatenate([cur["q_rows"][1], att[1]], axis=1)], axis=0)
        rhs = jnp.concatenate([state.astype(bf), v], axis=0)
        o = jnp.dot(lhs, rhs, preferred_element_type=jnp.float32)
        d["o"][cur["rows"], st["vsl"]] = o.astype(d["o"].dtype)
        upd = lax.dot_general(cur["k_pair"], v, (((0,), (0,)), ((), ())),
                              preferred_element_type=jnp.float32)
        c0 = 2 * cur["p"]
        decay = st["decay_t"][:, c0:c0 + 1] * st["decay_t"][:, c0 + 1:c0 + 2]
        d["s"][st["h"]] = decay * state + upd
    after_rounds()


def _out_body(x_ref, of_ref, ob_ref, sz_ref, yc_ref, gn_ref, wo_ref, fg_ref, out_ref):
    acc = jnp.dot(yc_ref[...], wo_ref[GLA_V_W:, :], preferred_element_type=jnp.float32)
    for h in range(GLA_HEADS):
        sl = slice(h * GLA_DV, (h + 1) * GLA_DV)
        o_h = of_ref[:, sl].astype(jnp.float32) + ob_ref[:, sl].astype(jnp.float32)
        ms = jnp.mean(o_h * o_h, axis=-1, keepdims=True)
        y_h = o_h * lax.rsqrt(ms + EPS) * gn_ref[...] * sz_ref[:, sl].astype(jnp.float32)
        acc = acc + jnp.dot(y_h.astype(jnp.bfloat16), wo_ref[sl, :],
                            preferred_element_type=jnp.float32)
    xo = x_ref[...] + acc
    ms = jnp.mean(xo * xo, axis=-1, keepdims=True)
    out_ref[...] = xo * lax.rsqrt(ms + EPS) * fg_ref[...]


def _mix_kernel(qf_ref, kf_ref, vf_ref, lrf_ref, qb_ref, kb_ref, vb_ref, lrb_ref,
                lrf_next_ref, lrb_next_ref, wgf_ref, wgb_ref, trif_ref, trib_ref,
                x_ref, sz_ref, yc_ref, gn_ref, wo_ref, fg_ref, out_ref, s_ref, o_ref, b_ref,
                gs_ref):
    s = pl.program_id(1)
    nblk = pl.num_programs(1) // 2
    slot = s % 2

    @pl.when((pl.program_id(0) == 0) & (s == 0))
    def _():
        _gate_terms_into(lrf_ref, wgf_ref, gs_ref.at[0])
        _gate_terms_into(lrb_ref, wgb_ref, gs_ref.at[1])
        _cumsum_into(gs_ref.at[0], trif_ref, b_ref.at[0, 0])
        _cumsum_into(gs_ref.at[1], trib_ref, b_ref.at[0, 1])

    @pl.when(s < nblk)
    def _():
        def gates_next():
            _gate_terms_into(lrf_next_ref, wgf_ref, gs_ref.at[0])
            _gate_terms_into(lrb_next_ref, wgb_ref, gs_ref.at[1])

        def sums_next():
            _cumsum_into(gs_ref.at[0], trif_ref, b_ref.at[1 - slot, 0])
            _cumsum_into(gs_ref.at[1], trib_ref, b_ref.at[1 - slot, 1])

        _gla_body(qf_ref, kf_ref, vf_ref, b_ref.at[slot, 0], qb_ref, kb_ref, vb_ref,
                  b_ref.at[slot, 1], o_ref.at[0, s], o_ref.at[1, nblk - 1 - s], s_ref, s == 0,
                  gates_next, sums_next)

    @pl.when(s >= nblk)
    def _():
        _out_body(x_ref, o_ref.at[0, s - nblk], o_ref.at[1, s - nblk], sz_ref, yc_ref,
                  gn_ref, wo_ref, fg_ref, out_ref)


def _mix(q, k, v, lr, wg_f, wg_b, x2, sz, yc, gn, wo, fg, *, batch, seq):
    ts = TILE
    nblk = seq // ts
    assert nblk % 2 == 0
    gla_s = lambda s: jnp.minimum(s, nblk - 1)

    def following(b, s):
        nxt = jnp.minimum(b * nblk + gla_s(s) + 1, batch * nblk - 1)
        return nxt // nblk, nxt % nblk

    def fwd_next(b, s):
        b1, s1 = following(b, s)
        return (b1 * nblk + s1, 0)

    def bwd_next(b, s):
        b1, s1 = following(b, s)
        return (b1 * nblk + (nblk - 1 - s1), 0)

    fwd = lambda b, s: (b * nblk + gla_s(s), 0)
    bwd = lambda b, s: (b * nblk + (nblk - 1 - gla_s(s)), 0)
    tile = lambda b, s: (b * nblk + jnp.maximum(s - nblk, 0), 0)
    rows = lambda idx: [pl.BlockSpec((ts, w), idx) for w in (GLA_QK_W, GLA_QK_W, GLA_V_W, LR_W)]
    const = lambda shp, **kw: pl.BlockSpec(shp, lambda b, s: (0, 0), **kw)
    return pl.pallas_call(
        _mix_kernel,
        grid=(batch, 2 * nblk),
        in_specs=(rows(fwd) + rows(bwd)
                  + [pl.BlockSpec((ts, LR_W), fwd_next), pl.BlockSpec((ts, LR_W), bwd_next)]
                  + [const((LR_W, GLA_QK_W))] * 2
                  + [const((2 * CHUNK, 4 * CHUNK))] * 2
                  + [pl.BlockSpec((ts, D_MODEL), tile), pl.BlockSpec((ts, GLA_V_W), tile),
                     pl.BlockSpec((ts, CONV_W), tile), const((1, GLA_DV)),
                     const((GLA_V_W + CONV_W, D_MODEL), pipeline_mode=pl.Buffered(1)),
                     const((1, D_MODEL))]),
        out_specs=pl.BlockSpec((ts, D_MODEL), tile),
        out_shape=jax.ShapeDtypeStruct((batch * seq, D_MODEL), jnp.float32),
        scratch_shapes=[pltpu.VMEM((2, GLA_HEADS, GLA_DK, GLA_DV), jnp.float32),
                        pltpu.VMEM((2, nblk, ts, GLA_V_W), jnp.bfloat16),
                        pltpu.VMEM((2, 2, ts, GLA_QK_W), jnp.float32),
                        pltpu.VMEM((2, 2 * ts, GLA_QK_W), jnp.bfloat16)],
        compiler_params=pltpu.CompilerParams(
            dimension_semantics=("arbitrary", "arbitrary"), vmem_limit_bytes=VMEM_LIMIT),
        name="mix",
    )(q, k, v, lr, q, k, v, lr, lr, lr, wg_f, wg_b, _pair_tri(False), _pair_tri(True),
      x2, sz, yc, gn, wo, fg)


def _regroup_kernel(a_ref, b_ref, o_ref):
    j = pl.program_id(0)
    n_before = _OFF_BG // REGROUP_TILE
    n_main = _OFF_LR // REGROUP_TILE
    lr = 2 * GATE_RANK

    def put(rows):
        o_ref[...] = jnp.transpose(rows).astype(o_ref.dtype)

    @pl.when(j < n_before)
    def _():
        put(a_ref[...])

    @pl.when((j >= n_before) & (j < n_main))
    def _():
        put(jnp.concatenate([a_ref[lr:, :], b_ref[...]], axis=0))

    @pl.when(j == n_main)
    def _():
        put(jnp.concatenate([a_ref[:lr, :], jnp.zeros((REGROUP_TILE - lr, a_ref.shape[1]),
                                                      a_ref.dtype)], axis=0))


def _regroup_w_in(w):
    wt = jnp.swapaxes(w, 1, 2)
    _, n, k = wt.shape
    lr = 2 * GATE_RANK
    assert lr == REGROUP_HALO and _OFF_BG % REGROUP_TILE == 0 and _OFF_LR % REGROUP_TILE == 0
    n_main = _OFF_LR // REGROUP_TILE
    sub = REGROUP_TILE // REGROUP_HALO
    first = lambda j: jnp.where(j == n_main, _OFF_BG // REGROUP_TILE, j)
    return pl.pallas_call(
        _regroup_kernel,
        grid=(W_ALL // REGROUP_TILE,),
        in_specs=[pl.BlockSpec((None, REGROUP_TILE, k), lambda j: (0, first(j), 0)),
                  pl.BlockSpec((None, REGROUP_HALO, k),
                               lambda j: (0, jnp.minimum((j + 1) * sub, n // REGROUP_HALO - 1), 0))],
        out_specs=pl.BlockSpec((k, REGROUP_TILE), lambda j: (0, j)),
        out_shape=jax.ShapeDtypeStruct((k, W_ALL), jnp.bfloat16),
        compiler_params=pltpu.CompilerParams(dimension_semantics=("arbitrary",)),
        name="regroup",
    )(wt, wt)


def _gate_weight(w_gk, b_gk, first_row):
    bias = b_gk * LOG2_E
    bias_hi = bias.astype(jnp.bfloat16)
    bias_lo = (bias - bias_hi.astype(jnp.float32)).astype(jnp.bfloat16)
    full = jnp.zeros((LR_W, GLA_QK_W), jnp.bfloat16)
    full = lax.dynamic_update_slice(full, (w_gk * LOG2_E).astype(jnp.bfloat16), (first_row, 0))
    return lax.dynamic_update_slice(full, jnp.stack([bias_hi, bias_lo]), (ONES_COL, 0))


def kernel(x, norm_g, w_in, w_gk_f, b_gk_f, w_gk_b, b_gk_b, gla_norm_g, conv_w, conv_b, w_out, final_g):
    batch, seq, d = x.shape
    depth = w_in.shape[0]
    assert depth == 1 and d == D_MODEL
    assert seq % TILE == 0
    x2 = x.reshape(batch * seq, d)
    q, k, v, sz, yc, lr = _inproj(x2, norm_g[0][None, :], _regroup_w_in(w_in),
                                  conv_w[0], conv_b[0][None, :], seq=seq)
    out = _mix(q, k, v, lr, _gate_weight(w_gk_f[0], b_gk_f[0], 0),
               _gate_weight(w_gk_b[0], b_gk_b[0], GATE_RANK), x2, sz, yc,
               gla_norm_g[0][None, :], w_out[0].astype(jnp.bfloat16), final_g[None, :],
               batch=batch, seq=seq)
    return out.reshape(batch, seq, d)
```

```python
import functools

import jax
import jax.numpy as jnp
import numpy as np
from jax import lax
from jax.experimental import pallas as pl
from jax.experimental.pallas import tpu as pltpu

D_MODEL = 1024
GLA_HEADS = 4
GLA_DK = 128
GLA_DV = 256
GLA_QK_W = GLA_HEADS * GLA_DK
GLA_V_W = GLA_HEADS * GLA_DV
GATE_RANK = 16
GATE_NORM = 16.0
CHUNK = 64
CONV_W = 1024
EPS = 1e-6
LOG2_E = 1.4426950408889634

LANES = 128
HALO = 8
LR_W = LANES
ONES_COL = 2 * GATE_RANK

_OFF_Q = 0
_OFF_K = _OFF_Q + GLA_QK_W
_OFF_V = _OFF_K + GLA_QK_W
_OFF_ZA = _OFF_V + GLA_V_W
_OFF_BG = _OFF_ZA + GLA_V_W
_OFF_CG = _OFF_BG + CONV_W
_OFF_HC = _OFF_CG + CONV_W
_OFF_ZC = _OFF_HC + CONV_W
_OFF_LR = _OFF_ZC + CONV_W
QKV_W = _OFF_ZA
REGROUP_TILE = 4 * LANES
REGROUP_HALO = 2 * GATE_RANK
W_ALL = _OFF_LR + REGROUP_TILE

TILE = 512
VMEM_LIMIT = 58 * 1024 * 1024
SCORE_LEAD = 12


def _silu(z):
    return z * (1.0 / (1.0 + jnp.exp(-z)))


def _inproj_kernel(x_ref, xp_ref, xn_ref, g_ref, w_ref, cw_ref, cb_ref,
                   qkv_ref, sz_ref, yc_ref, lr_ref, *, tiles_per_seq):
    tm = x_ref.shape[0]
    pos = pl.program_id(0) % tiles_per_seq

    def normed(x):
        ms = jnp.mean(x * x, axis=-1, keepdims=True)
        return x * lax.rsqrt(ms + EPS) * g_ref[...]

    h = normed(x_ref[...]).astype(jnp.bfloat16)
    halo = jnp.concatenate([normed(xp_ref[...]), normed(xn_ref[...])], axis=0).astype(jnp.bfloat16)
    h_ext = jnp.concatenate([h, halo], axis=0)

    def proj(lo, hi, lhs=h):
        return jnp.dot(lhs, w_ref[:, lo:hi], preferred_element_type=jnp.float32)

    u_ext = proj(_OFF_CG, _OFF_HC, h_ext) * proj(_OFF_HC, _OFF_ZC, h_ext)
    u = u_ext[:tm]
    prev_row = jnp.where(pos == 0, 0.0, u_ext[tm + HALO - 1:tm + HALO])
    next_row = jnp.where(pos == tiles_per_seq - 1, 0.0, u_ext[tm + HALO:tm + HALO + 1])
    t_i = lax.broadcasted_iota(jnp.int32, (tm, 1), 0)
    u_prev = jnp.where(t_i == 0, prev_row, pltpu.roll(u, 1, 0))
    u_next = jnp.where(t_i == tm - 1, next_row, pltpu.roll(u, tm - 1, 0))
    conv = cw_ref[0:1, :] * u_prev + cw_ref[1:2, :] * u + cw_ref[2:3, :] * u_next + cb_ref[...]
    yc_ref[...] = (proj(_OFF_BG, _OFF_CG) * conv * _silu(proj(_OFF_ZC, _OFF_LR))).astype(yc_ref.dtype)

    sz_ref[...] = _silu(proj(_OFF_ZA, _OFF_BG)).astype(sz_ref.dtype)
    qkv_ref[:, _OFF_Q:_OFF_K] = (proj(_OFF_Q, _OFF_K) * (GLA_DK ** -0.5)).astype(qkv_ref.dtype)
    qkv_ref[:, _OFF_K:_OFF_V] = proj(_OFF_K, _OFF_V).astype(qkv_ref.dtype)
    qkv_ref[:, _OFF_V:_OFF_ZA] = proj(_OFF_V, _OFF_ZA).astype(qkv_ref.dtype)
    col = lax.broadcasted_iota(jnp.int32, (1, LR_W), 1)
    ones = jnp.where((col == ONES_COL) | (col == ONES_COL + 1), 1.0, 0.0)
    lr_ref[...] = proj(_OFF_LR, _OFF_LR + LR_W) + ones


def _inproj(x2, norm_g, w_all, conv_w, conv_b, *, seq):
    m = x2.shape[0]
    tm = TILE
    sub = tm // HALO
    nsub = m // HALO
    row = lambda w: pl.BlockSpec((tm, w), lambda i: (i, 0))
    const = lambda shp: pl.BlockSpec(shp, lambda i: (0, 0))
    halo_prev = pl.BlockSpec((HALO, D_MODEL), lambda i: (jnp.maximum(i * sub - 1, 0), 0))
    halo_next = pl.BlockSpec((HALO, D_MODEL), lambda i: (jnp.minimum((i + 1) * sub, nsub - 1), 0))
    bf = jnp.bfloat16
    return pl.pallas_call(
        functools.partial(_inproj_kernel, tiles_per_seq=seq // tm),
        grid=(m // tm,),
        in_specs=[
            row(D_MODEL), halo_prev, halo_next,
            const((1, D_MODEL)),
            pl.BlockSpec((D_MODEL, W_ALL), lambda i: (0, 0), pipeline_mode=pl.Buffered(1)),
            const((3, CONV_W)), const((1, CONV_W)),
        ],
        out_specs=[row(QKV_W), row(GLA_V_W), row(CONV_W), row(LR_W)],
        out_shape=[
            jax.ShapeDtypeStruct((m, QKV_W), bf),
            jax.ShapeDtypeStruct((m, GLA_V_W), bf),
            jax.ShapeDtypeStruct((m, CONV_W), bf),
            jax.ShapeDtypeStruct((m, LR_W), jnp.float32),
        ],
        compiler_params=pltpu.CompilerParams(
            dimension_semantics=("arbitrary",), vmem_limit_bytes=VMEM_LIMIT),
        name="inproj",
    )(x2, x2, x2, norm_g, w_all, conv_w, conv_b)


def _pair_tri(rev):
    t = np.arange(CHUNK)
    tri = (t[None, :] >= t[:, None]) if rev else (t[None, :] <= t[:, None])
    zero = np.zeros_like(tri)
    keep = np.block([[tri, tri, zero, zero], [zero, zero, tri, tri]])
    return jnp.asarray(keep / GATE_NORM, dtype=jnp.bfloat16)


def _gate_terms_into(lr_ref, wg_ref, gs_ref):
    ts = lr_ref.shape[0]
    x = jnp.dot(lr_ref[...].astype(jnp.bfloat16), wg_ref[...], preferred_element_type=jnp.float32)
    g = jnp.minimum(x, 0.0) - jnp.log2(1.0 + jnp.exp2(-jnp.abs(x)))
    g_top = lax.bitcast_convert_type(
        lax.bitcast_convert_type(g, jnp.uint32) & jnp.uint32(0xFFFF0000), jnp.float32)
    g_hi = g_top.astype(gs_ref.dtype)
    g_lo = (g - g_top).astype(gs_ref.dtype)
    for c in range(ts // CHUNK):
        rows = slice(c * CHUNK, (c + 1) * CHUNK)
        gs_ref[2 * c * CHUNK:(2 * c + 1) * CHUNK, :] = g_hi[rows]
        gs_ref[(2 * c + 1) * CHUNK:(2 * c + 2) * CHUNK, :] = g_lo[rows]


def _cumsum_into(gs_ref, tri_ref, b_ref):
    tri = tri_ref[...]
    for p in range(b_ref.shape[0] // (2 * CHUNK)):
        b_ref[2 * p * CHUNK:(2 * p + 2) * CHUNK, :] = jnp.dot(
            tri, gs_ref[4 * p * CHUNK:(4 * p + 4) * CHUNK, :], preferred_element_type=jnp.float32)


def _gla_body(qf_ref, kf_ref, vf_ref, bf_ref, qb_ref, kb_ref, vb_ref, bb_ref,
              of_ref, ob_ref, s_ref, first_block, mid_round, after_rounds):
    ts = qf_ref.shape[0]
    nc = ts // CHUNK
    npair = nc // 2
    pair = 2 * CHUNK

    @pl.when(first_block)
    def _():
        s_ref[...] = jnp.zeros_like(s_ref)

    ri = lax.broadcasted_iota(jnp.int32, (CHUNK, pair), 0)
    li = lax.broadcasted_iota(jnp.int32, (CHUNK, pair), 1)
    lo_half = li < CHUNK
    dirs = (
        dict(q=qf_ref, k=kf_ref, v=vf_ref, o=of_ref, s=s_ref.at[0], a=0, b_=1,
             mask_a=lo_half & (li <= ri), mask_b=lo_half | (li - CHUNK <= ri),
             ref_row=CHUNK // 2, last_row=CHUNK - 1, order=tuple(range(npair)),
             b=bf_ref),
        dict(q=qb_ref, k=kb_ref, v=vb_ref, o=ob_ref, s=s_ref.at[1], a=1, b_=0,
             mask_a=(~lo_half) & (li - CHUNK > ri), mask_b=(~lo_half) | (li > ri),
             ref_row=CHUNK - 1 - CHUNK // 2, last_row=0, order=tuple(range(npair - 1, -1, -1)),
             b=bb_ref),
    )

    streams = []
    for d in dirs:
        b_last_rows = jnp.concatenate(
            [d["b"][c * CHUNK + d["last_row"]:c * CHUNK + d["last_row"] + 1, :] for c in range(nc)],
            axis=0)
        for h in range(GLA_HEADS):
            ksl = slice(h * GLA_DK, (h + 1) * GLA_DK)
            streams.append(dict(
                d=d, h=h, ksl=ksl, vsl=slice(h * GLA_DV, (h + 1) * GLA_DV),
                decay_t=jnp.transpose(jnp.exp2(b_last_rows[:, ksl])),
            ))

    bf = jnp.bfloat16
    zeros_k = jnp.zeros((CHUNK, GLA_DK), bf)

    def intra(st, step):
        d = st["d"]
        p = d["order"][step]
        rows = slice(p * pair, (p + 1) * pair)
        ia, ib = d["a"], d["b_"]
        b = d["b"][rows, st["ksl"]].reshape(2, CHUNK, GLA_DK)
        b_mid = b[:, d["ref_row"]:d["ref_row"] + 1, :]
        tot = b[:, d["last_row"]:d["last_row"] + 1, :]
        qe = (d["q"][rows, st["ksl"]].astype(jnp.float32).reshape(2, CHUNK, GLA_DK)
              * jnp.exp2(b - b_mid))
        ke = (d["k"][rows, st["ksl"]].astype(jnp.float32).reshape(2, CHUNK, GLA_DK)
              * jnp.exp2(b_mid - b))
        f_q = jnp.exp2(b_mid)
        f_k = jnp.exp2(tot - b_mid)
        q_in_a = (qe[ia] * f_q[ia]).astype(bf)
        q_in_b = (qe[ib] * f_q[ib]).astype(bf)
        q_in_b2 = (qe[ib] * (f_q[ib] * jnp.exp2(tot[ia]))).astype(bf)
        k_out_a = (ke[ia] * f_k[ia]).astype(bf)
        k_out_a2 = (ke[ia] * (f_k[ia] * jnp.exp2(tot[ib]))).astype(bf)
        k_out_b = (ke[ib] * f_k[ib]).astype(bf)
        ke_a = ke[ia].astype(bf)
        ke_b = ke[ib].astype(bf)
        nt = (((1,), (1,)), ((), ()))
        half = lambda x, slot: jnp.concatenate([x, zeros_k] if slot == 0 else [zeros_k, x], axis=0)
        att_a = lax.dot_general(qe[ia].astype(bf), half(ke_a, ia), nt,
                                preferred_element_type=jnp.float32)
        rhs_b = [None, None]
        rhs_b[ia] = jnp.concatenate([k_out_a, zeros_k], axis=1)
        rhs_b[ib] = jnp.concatenate([zeros_k, ke_b], axis=1)
        att_b = lax.dot_general(jnp.concatenate([q_in_b, qe[ib].astype(bf)], axis=1),
                                jnp.concatenate(rhs_b, axis=0), nt,
                                preferred_element_type=jnp.float32)
        q_rows = [None, None]
        q_rows[ia], q_rows[ib] = q_in_a, q_in_b2
        k_rows = [None, None]
        k_rows[ia], k_rows[ib] = k_out_a2, k_out_b
        return dict(p=p, rows=rows, att=(att_a, att_b), q_rows=q_rows,
                    k_pair=jnp.concatenate(k_rows, axis=0))

    items = [(step, st) for step in range(npair) for st in streams]
    ahead = {(step, id(st)): intra(st, step) for step, st in items[:SCORE_LEAD]}
    for i, (step, st) in enumerate(items):
        if step == npair // 2 and st is streams[0]:
            mid_round()
        d = st["d"]
        cur = ahead.pop((step, id(st)))
        if i + SCORE_LEAD < len(items):
            nstep, nst = items[i + SCORE_LEAD]
            ahead[(nstep, id(nst))] = intra(nst, nstep)
        ia, ib = d["a"], d["b_"]
        v = d["v"][cur["rows"], st["vsl"]]
        state = d["s"][st["h"]]
        att = [None, None]
        att[ia] = jnp.where(d["mask_a"], cur["att"][0], 0.0).astype(bf)
        att[ib] = jnp.where(d["mask_b"], cur["att"][1], 0.0).astype(bf)
        lhs = jnp.concatenate([jnp.concatenate([cur["q_rows"][0], att[0]], axis=1),
                               jnp.concatenate([cur["q_rows"][1], att[1]], axis=1)], axis=0)
        rhs = jnp.concatenate([state.astype(bf), v], axis=0)
        o = jnp.dot(lhs, rhs, preferred_element_type=jnp.float32)
        d["o"][cur["rows"], st["vsl"]] = o.astype(d["o"].dtype)
        upd = lax.dot_general(cur["k_pair"], v, (((0,), (0,)), ((), ())),
                              preferred_element_type=jnp.float32)
        c0 = 2 * cur["p"]
        decay = st["decay_t"][:, c0:c0 + 1] * st["decay_t"][:, c0 + 1:c0 + 2]
        d["s"][st["h"]] = decay * state + upd
    after_rounds()


def _out_body(x_ref, of_ref, ob_ref, sz_ref, yc_ref, gn_ref, wo_ref, fg_ref, out_ref):
    acc = jnp.dot(yc_ref[...], wo_ref[GLA_V_W:, :], preferred_element_type=jnp.float32)
    for h in range(GLA_HEADS):
        sl = slice(h * GLA_DV, (h + 1) * GLA_DV)
        o_h = of_ref[:, sl].astype(jnp.float32) + ob_ref[:, sl].astype(jnp.float32)
        ms = jnp.mean(o_h * o_h, axis=-1, keepdims=True)
        y_h = o_h * lax.rsqrt(ms + EPS) * gn_ref[...] * sz_ref[:, sl].astype(jnp.float32)
        acc = acc + jnp.dot(y_h.astype(jnp.bfloat16), wo_ref[sl, :],
                            preferred_element_type=jnp.float32)
    xo = x_ref[...] + acc
    ms = jnp.mean(xo * xo, axis=-1, keepdims=True)
    out_ref[...] = xo * lax.rsqrt(ms + EPS) * fg_ref[...]


def _mix_kernel(qkvf_ref, lrf_ref, qkvb_ref, lrb_ref, lrf_next_ref, lrb_next_ref, wg_ref, tri_ref,
                x_ref, sz_ref, yc_ref, gn_ref, wo_ref, fg_ref, out_ref, s_ref, o_ref, b_ref,
                gs_ref):
    s = pl.program_id(1)
    nblk = pl.num_programs(1) // 2
    slot = s % 2
    wgf_ref, wgb_ref = wg_ref.at[0], wg_ref.at[1]
    trif_ref, trib_ref = tri_ref.at[0], tri_ref.at[1]
    split = lambda r: (r.at[:, _OFF_Q:_OFF_K], r.at[:, _OFF_K:_OFF_V], r.at[:, _OFF_V:_OFF_ZA])
    qf_ref, kf_ref, vf_ref = split(qkvf_ref)
    qb_ref, kb_ref, vb_ref = split(qkvb_ref)

    @pl.when((pl.program_id(0) == 0) & (s == 0))
    def _():
        _gate_terms_into(lrf_ref, wgf_ref, gs_ref.at[0])
        _gate_terms_into(lrb_ref, wgb_ref, gs_ref.at[1])
        _cumsum_into(gs_ref.at[0], trif_ref, b_ref.at[0, 0])
        _cumsum_into(gs_ref.at[1], trib_ref, b_ref.at[0, 1])

    @pl.when(s < nblk)
    def _():
        def gates_next():
            _gate_terms_into(lrf_next_ref, wgf_ref, gs_ref.at[0])
            _gate_terms_into(lrb_next_ref, wgb_ref, gs_ref.at[1])

        def sums_next():
            _cumsum_into(gs_ref.at[0], trif_ref, b_ref.at[1 - slot, 0])
            _cumsum_into(gs_ref.at[1], trib_ref, b_ref.at[1 - slot, 1])

        _gla_body(qf_ref, kf_ref, vf_ref, b_ref.at[slot, 0], qb_ref, kb_ref, vb_ref,
                  b_ref.at[slot, 1], o_ref.at[0, s], o_ref.at[1, nblk - 1 - s], s_ref, s == 0,
                  gates_next, sums_next)

    @pl.when(s >= nblk)
    def _():
        _out_body(x_ref, o_ref.at[0, s - nblk], o_ref.at[1, s - nblk], sz_ref, yc_ref,
                  gn_ref, wo_ref, fg_ref, out_ref)


def _mix(qkv, lr, wg, x2, sz, yc, gn, wo, fg, *, batch, seq):
    ts = TILE
    nblk = seq // ts
    assert nblk % 2 == 0
    gla_s = lambda s: jnp.minimum(s, nblk - 1)

    def following(b, s):
        nxt = jnp.minimum(b * nblk + gla_s(s) + 1, batch * nblk - 1)
        return nxt // nblk, nxt % nblk

    def fwd_next(b, s):
        b1, s1 = following(b, s)
        return (b1 * nblk + s1, 0)

    def bwd_next(b, s):
        b1, s1 = following(b, s)
        return (b1 * nblk + (nblk - 1 - s1), 0)

    fwd = lambda b, s: (b * nblk + gla_s(s), 0)
    bwd = lambda b, s: (b * nblk + (nblk - 1 - gla_s(s)), 0)
    tile = lambda b, s: (b * nblk + jnp.maximum(s - nblk, 0), 0)
    rows = lambda idx: [pl.BlockSpec((ts, QKV_W), idx), pl.BlockSpec((ts, LR_W), idx)]
    const = lambda shp, **kw: pl.BlockSpec(shp, lambda b, s: (0, 0), **kw)
    return pl.pallas_call(
        _mix_kernel,
        grid=(batch, 2 * nblk),
        in_specs=(rows(fwd) + rows(bwd)
                  + [pl.BlockSpec((ts, LR_W), fwd_next), pl.BlockSpec((ts, LR_W), bwd_next)]
                  + [pl.BlockSpec((2, LR_W, GLA_QK_W), lambda b, s: (0, 0, 0)),
                     pl.BlockSpec((2, 2 * CHUNK, 4 * CHUNK), lambda b, s: (0, 0, 0))]
                  + [pl.BlockSpec((ts, D_MODEL), tile), pl.BlockSpec((ts, GLA_V_W), tile),
                     pl.BlockSpec((ts, CONV_W), tile), const((1, GLA_DV)),
                     const((GLA_V_W + CONV_W, D_MODEL), pipeline_mode=pl.Buffered(1)),
                     const((1, D_MODEL))]),
        out_specs=pl.BlockSpec((ts, D_MODEL), tile),
        out_shape=jax.ShapeDtypeStruct((batch * seq, D_MODEL), jnp.float32),
        scratch_shapes=[pltpu.VMEM((2, GLA_HEADS, GLA_DK, GLA_DV), jnp.float32),
                        pltpu.VMEM((2, nblk, ts, GLA_V_W), jnp.bfloat16),
                        pltpu.VMEM((2, 2, ts, GLA_QK_W), jnp.float32),
                        pltpu.VMEM((2, 2 * ts, GLA_QK_W), jnp.bfloat16)],
        compiler_params=pltpu.CompilerParams(
            dimension_semantics=("arbitrary", "arbitrary"), vmem_limit_bytes=VMEM_LIMIT),
        name="mix",
    )(qkv, lr, qkv, lr, lr, lr, wg, jnp.stack([_pair_tri(False), _pair_tri(True)]),
      x2, sz, yc, gn, wo, fg)


def _regroup_kernel(a_ref, b_ref, o_ref):
    j = pl.program_id(0)
    n_before = _OFF_BG // REGROUP_TILE
    n_main = _OFF_LR // REGROUP_TILE
    lr = 2 * GATE_RANK

    def put(rows):
        o_ref[...] = jnp.transpose(rows).astype(o_ref.dtype)

    @pl.when(j < n_before)
    def _():
        put(a_ref[...])

    @pl.when((j >= n_before) & (j < n_main))
    def _():
        put(jnp.concatenate([a_ref[lr:, :], b_ref[...]], axis=0))

    @pl.when(j == n_main)
    def _():
        put(jnp.concatenate([a_ref[:lr, :], jnp.zeros((REGROUP_TILE - lr, a_ref.shape[1]),
                                                      a_ref.dtype)], axis=0))


def _regroup_w_in(w):
    wt = jnp.swapaxes(w, 1, 2)
    _, n, k = wt.shape
    lr = 2 * GATE_RANK
    assert lr == REGROUP_HALO and _OFF_BG % REGROUP_TILE == 0 and _OFF_LR % REGROUP_TILE == 0
    n_main = _OFF_LR // REGROUP_TILE
    sub = REGROUP_TILE // REGROUP_HALO
    first = lambda j: jnp.where(j == n_main, _OFF_BG // REGROUP_TILE, j)
    return pl.pallas_call(
        _regroup_kernel,
        grid=(W_ALL // REGROUP_TILE,),
        in_specs=[pl.BlockSpec((None, REGROUP_TILE, k), lambda j: (0, first(j), 0)),
                  pl.BlockSpec((None, REGROUP_HALO, k),
                               lambda j: (0, jnp.minimum((j + 1) * sub, n // REGROUP_HALO - 1), 0))],
        out_specs=pl.BlockSpec((k, REGROUP_TILE), lambda j: (0, j)),
        out_shape=jax.ShapeDtypeStruct((k, W_ALL), jnp.bfloat16),
        compiler_params=pltpu.CompilerParams(dimension_semantics=("arbitrary",)),
        name="regroup",
    )(wt, wt)


def _gate_weight(w_gk, b_gk, first_row):
    bias = b_gk * LOG2_E
    bias_hi = bias.astype(jnp.bfloat16)
    bias_lo = (bias - bias_hi.astype(jnp.float32)).astype(jnp.bfloat16)
    full = jnp.zeros((LR_W, GLA_QK_W), jnp.bfloat16)
    full = lax.dynamic_update_slice(full, (w_gk * LOG2_E).astype(jnp.bfloat16), (first_row, 0))
    return lax.dynamic_update_slice(full, jnp.stack([bias_hi, bias_lo]), (ONES_COL, 0))


def kernel(x, norm_g, w_in, w_gk_f, b_gk_f, w_gk_b, b_gk_b, gla_norm_g, conv_w, conv_b, w_out, final_g):
    batch, seq, d = x.shape
    depth = w_in.shape[0]
    assert depth == 1 and d == D_MODEL
    assert seq % TILE == 0
    x2 = x.reshape(batch * seq, d)
    qkv, sz, yc, lr = _inproj(x2, norm_g[0][None, :], _regroup_w_in(w_in),
                              conv_w[0], conv_b[0][None, :], seq=seq)
    wg = jnp.stack([_gate_weight(w_gk_f[0], b_gk_f[0], 0),
                    _gate_weight(w_gk_b[0], b_gk_b[0], GATE_RANK)])
    out = _mix(qkv, lr, wg, x2, sz, yc, gla_norm_g[0][None, :],
               w_out[0].astype(jnp.bfloat16), final_g[None, :], batch=batch, seq=seq)
    return out.reshape(batch, seq, d)
```

```python
import functools

import jax
import jax.numpy as jnp
import numpy as np
from jax import lax
from jax.experimental import pallas as pl
from jax.experimental.pallas import tpu as pltpu

D_MODEL = 1024
GLA_HEADS = 4
GLA_DK = 128
GLA_DV = 256
GLA_QK_W = GLA_HEADS * GLA_DK
GLA_V_W = GLA_HEADS * GLA_DV
GATE_RANK = 16
GATE_NORM = 16.0
CHUNK = 64
CONV_W = 1024
EPS = 1e-6
LOG2_E = 1.4426950408889634

LANES = 128
HALO = 8
LR_W = LANES
ONES_COL = 2 * GATE_RANK

_OFF_Q = 0
_OFF_K = _OFF_Q + GLA_QK_W
_OFF_V = _OFF_K + GLA_QK_W
_OFF_ZA = _OFF_V + GLA_V_W
_OFF_BG = _OFF_ZA + GLA_V_W
_OFF_CG = _OFF_BG + CONV_W
_OFF_HC = _OFF_CG + CONV_W
_OFF_ZC = _OFF_HC + CONV_W
_OFF_LR = _OFF_ZC + CONV_W
REGROUP_TILE = 4 * LANES
REGROUP_HALO = 2 * GATE_RANK
W_ALL = _OFF_LR + REGROUP_TILE

TILE = 512
VMEM_LIMIT = 58 * 1024 * 1024
SCORE_LEAD = 12


def _silu(z):
    return z * (1.0 / (1.0 + jnp.exp(-z)))


def _inproj_kernel(x_ref, xp_ref, xn_ref, g_ref, w_ref, cw_ref, cb_ref,
                   q_ref, k_ref, v_ref, sz_ref, yc_ref, lr_ref, *, tiles_per_seq):
    tm = x_ref.shape[0]
    pos = pl.program_id(0) % tiles_per_seq

    def normed(x):
        ms = jnp.mean(x * x, axis=-1, keepdims=True)
        return x * lax.rsqrt(ms + EPS) * g_ref[...]

    h = normed(x_ref[...]).astype(jnp.bfloat16)
    halo = jnp.concatenate([normed(xp_ref[...]), normed(xn_ref[...])], axis=0).astype(jnp.bfloat16)
    h_ext = jnp.concatenate([h, halo], axis=0)

    def proj(lo, hi, lhs=h):
        return jnp.dot(lhs, w_ref[:, lo:hi], preferred_element_type=jnp.float32)

    u_ext = proj(_OFF_CG, _OFF_HC, h_ext) * proj(_OFF_HC, _OFF_ZC, h_ext)
    u = u_ext[:tm]
    prev_row = jnp.where(pos == 0, 0.0, u_ext[tm + HALO - 1:tm + HALO])
    next_row = jnp.where(pos == tiles_per_seq - 1, 0.0, u_ext[tm + HALO:tm + HALO + 1])
    t_i = lax.broadcasted_iota(jnp.int32, (tm, 1), 0)
    u_prev = jnp.where(t_i == 0, prev_row, pltpu.roll(u, 1, 0))
    u_next = jnp.where(t_i == tm - 1, next_row, pltpu.roll(u, tm - 1, 0))
    conv = cw_ref[0:1, :] * u_prev + cw_ref[1:2, :] * u + cw_ref[2:3, :] * u_next + cb_ref[...]
    yc_ref[...] = (proj(_OFF_BG, _OFF_CG) * conv * _silu(proj(_OFF_ZC, _OFF_LR))).astype(yc_ref.dtype)

    sz_ref[...] = _silu(proj(_OFF_ZA, _OFF_BG)).astype(sz_ref.dtype)
    q_ref[...] = (proj(_OFF_Q, _OFF_K) * (GLA_DK ** -0.5)).astype(q_ref.dtype)
    k_ref[...] = proj(_OFF_K, _OFF_V).astype(k_ref.dtype)
    v_ref[...] = proj(_OFF_V, _OFF_ZA).astype(v_ref.dtype)
    col = lax.broadcasted_iota(jnp.int32, (1, LR_W), 1)
    ones = jnp.where((col == ONES_COL) | (col == ONES_COL + 1), 1.0, 0.0)
    lr_ref[...] = proj(_OFF_LR, _OFF_LR + LR_W) + ones


def _inproj(x2, norm_g, w_all, conv_w, conv_b, *, seq):
    m = x2.shape[0]
    tm = TILE
    sub = tm // HALO
    nsub = m // HALO
    row = lambda w: pl.BlockSpec((tm, w), lambda i: (i, 0))
    const = lambda shp: pl.BlockSpec(shp, lambda i: (0, 0))
    halo_prev = pl.BlockSpec((HALO, D_MODEL), lambda i: (jnp.maximum(i * sub - 1, 0), 0))
    halo_next = pl.BlockSpec((HALO, D_MODEL), lambda i: (jnp.minimum((i + 1) * sub, nsub - 1), 0))
    bf = jnp.bfloat16
    return pl.pallas_call(
        functools.partial(_inproj_kernel, tiles_per_seq=seq // tm),
        grid=(m // tm,),
        in_specs=[
            row(D_MODEL), halo_prev, halo_next,
            const((1, D_MODEL)),
            pl.BlockSpec((D_MODEL, W_ALL), lambda i: (0, 0), pipeline_mode=pl.Buffered(1)),
            const((3, CONV_W)), const((1, CONV_W)),
        ],
        out_specs=[row(GLA_QK_W), row(GLA_QK_W), row(GLA_V_W), row(GLA_V_W),
                   row(CONV_W), row(LR_W)],
        out_shape=[
            jax.ShapeDtypeStruct((m, GLA_QK_W), bf),
            jax.ShapeDtypeStruct((m, GLA_QK_W), bf),
            jax.ShapeDtypeStruct((m, GLA_V_W), bf),
            jax.ShapeDtypeStruct((m, GLA_V_W), bf),
            jax.ShapeDtypeStruct((m, CONV_W), bf),
            jax.ShapeDtypeStruct((m, LR_W), jnp.float32),
        ],
        compiler_params=pltpu.CompilerParams(
            dimension_semantics=("arbitrary",), vmem_limit_bytes=VMEM_LIMIT),
        name="inproj",
    )(x2, x2, x2, norm_g, w_all, conv_w, conv_b)


def _pair_tri(rev):
    t = np.arange(CHUNK)
    tri = (t[None, :] >= t[:, None]) if rev else (t[None, :] <= t[:, None])
    zero = np.zeros_like(tri)
    keep = np.block([[tri, tri, zero, zero], [zero, zero, tri, tri]])
    return jnp.asarray(keep / GATE_NORM, dtype=jnp.bfloat16)


def _gate_terms_into(lr_ref, wg_ref, gs_ref):
    ts = lr_ref.shape[0]
    x = jnp.dot(lr_ref[...].astype(jnp.bfloat16), wg_ref[...], preferred_element_type=jnp.float32)
    g = jnp.minimum(x, 0.0) - jnp.log2(1.0 + jnp.exp2(-jnp.abs(x)))
    g_top = lax.bitcast_convert_type(
        lax.bitcast_convert_type(g, jnp.uint32) & jnp.uint32(0xFFFF0000), jnp.float32)
    g_hi = g_top.astype(gs_ref.dtype)
    g_lo = (g - g_top).astype(gs_ref.dtype)
    for c in range(ts // CHUNK):
        rows = slice(c * CHUNK, (c + 1) * CHUNK)
        gs_ref[2 * c * CHUNK:(2 * c + 1) * CHUNK, :] = g_hi[rows]
        gs_ref[(2 * c + 1) * CHUNK:(2 * c + 2) * CHUNK, :] = g_lo[rows]


def _cumsum_into(gs_ref, tri_ref, b_ref):
    tri = tri_ref[...]
    for p in range(b_ref.shape[0] // (2 * CHUNK)):
        b_ref[2 * p * CHUNK:(2 * p + 2) * CHUNK, :] = jnp.dot(
            tri, gs_ref[4 * p * CHUNK:(4 * p + 4) * CHUNK, :], preferred_element_type=jnp.float32)


def _gla_body(qf_ref, kf_ref, vf_ref, bf_ref, qb_ref, kb_ref, vb_ref, bb_ref,
              of_ref, ob_ref, s_ref, first_block, mid_round, after_rounds):
    ts = qf_ref.shape[0]
    nc = ts // CHUNK
    npair = nc // 2
    pair = 2 * CHUNK

    @pl.when(first_block)
    def _():
        s_ref[...] = jnp.zeros_like(s_ref)

    ri = lax.broadcasted_iota(jnp.int32, (CHUNK, pair), 0)
    li = lax.broadcasted_iota(jnp.int32, (CHUNK, pair), 1)
    lo_half = li < CHUNK
    dirs = (
        dict(q=qf_ref, k=kf_ref, v=vf_ref, o=of_ref, s=s_ref.at[0], a=0, b_=1,
             mask_a=lo_half & (li <= ri), mask_b=lo_half | (li - CHUNK <= ri),
             ref_row=CHUNK // 2, last_row=CHUNK - 1, order=tuple(range(npair)),
             b=bf_ref),
        dict(q=qb_ref, k=kb_ref, v=vb_ref, o=ob_ref, s=s_ref.at[1], a=1, b_=0,
             mask_a=(~lo_half) & (li - CHUNK > ri), mask_b=(~lo_half) | (li > ri),
             ref_row=CHUNK - 1 - CHUNK // 2, last_row=0, order=tuple(range(npair - 1, -1, -1)),
             b=bb_ref),
    )

    streams = []
    for d in dirs:
        b_last_rows = jnp.concatenate(
            [d["b"][c * CHUNK + d["last_row"]:c * CHUNK + d["last_row"] + 1, :] for c in range(nc)],
            axis=0)
        for h in range(GLA_HEADS):
            ksl = slice(h * GLA_DK, (h + 1) * GLA_DK)
            streams.append(dict(
                d=d, h=h, ksl=ksl, vsl=slice(h * GLA_DV, (h + 1) * GLA_DV),
                decay_t=jnp.transpose(jnp.exp2(b_last_rows[:, ksl])),
            ))

    bf = jnp.bfloat16
    zeros_k = jnp.zeros((CHUNK, GLA_DK), bf)

    def intra(st, step):
        d = st["d"]
        p = d["order"][step]
        rows = slice(p * pair, (p + 1) * pair)
        ia, ib = d["a"], d["b_"]
        b = d["b"][rows, st["ksl"]].reshape(2, CHUNK, GLA_DK)
        b_mid = b[:, d["ref_row"]:d["ref_row"] + 1, :]
        tot = b[:, d["last_row"]:d["last_row"] + 1, :]
        qe = (d["q"][rows, st["ksl"]].astype(jnp.float32).reshape(2, CHUNK, GLA_DK)
              * jnp.exp2(b - b_mid))
        ke = (d["k"][rows, st["ksl"]].astype(jnp.float32).reshape(2, CHUNK, GLA_DK)
              * jnp.exp2(b_mid - b))
        f_q = jnp.exp2(b_mid)
        f_k = jnp.exp2(tot - b_mid)
        q_in_a = (qe[ia] * f_q[ia]).astype(bf)
        q_in_b = (qe[ib] * f_q[ib]).astype(bf)
        q_in_b2 = (qe[ib] * (f_q[ib] * jnp.exp2(tot[ia]))).astype(bf)
        k_out_a = (ke[ia] * f_k[ia]).astype(bf)
        k_out_a2 = (ke[ia] * (f_k[ia] * jnp.exp2(tot[ib]))).astype(bf)
        k_out_b = (ke[ib] * f_k[ib]).astype(bf)
        ke_a = ke[ia].astype(bf)
        ke_b = ke[ib].astype(bf)
        nt = (((1,), (1,)), ((), ()))
        half = lambda x, slot: jnp.concatenate([x, zeros_k] if slot == 0 else [zeros_k, x], axis=0)
        att_a = lax.dot_general(qe[ia].astype(bf), half(ke_a, ia), nt,
                                preferred_element_type=jnp.float32)
        rhs_b = [None, None]
        rhs_b[ia] = jnp.concatenate([k_out_a, zeros_k], axis=1)
        rhs_b[ib] = jnp.concatenate([zeros_k, ke_b], axis=1)
        att_b = lax.dot_general(jnp.concatenate([q_in_b, qe[ib].astype(bf)], axis=1),
                                jnp.concatenate(rhs_b, axis=0), nt,
                                preferred_element_type=jnp.float32)
        q_rows = [None, None]
        q_rows[ia], q_rows[ib] = q_in_a, q_in_b2
        k_rows = [None, None]
        k_rows[ia], k_rows[ib] = k_out_a2, k_out_b
        return dict(p=p, rows=rows, att=(att_a, att_b), q_rows=q_rows,
                    k_pair=jnp.concatenate(k_rows, axis=0))

    items = [(step, st) for step in range(npair) for st in streams]
    ahead = {(step, id(st)): intra(st, step) for step, st in items[:SCORE_LEAD]}
    for i, (step, st) in enumerate(items):
        if step == npair // 2 and st is streams[0]:
            mid_round()
        d = st["d"]
        cur = ahead.pop((step, id(st)))
        if i + SCORE_LEAD < len(items):
            nstep, nst = items[i + SCORE_LEAD]
            ahead[(nstep, id(nst))] = intra(nst, nstep)
        ia, ib = d["a"], d["b_"]
        v = d["v"][cur["rows"], st["vsl"]]
        state = d["s"][st["h"]]
        att = [None, None]
        att[ia] = jnp.where(d["mask_a"], cur["att"][0], 0.0).astype(bf)
        att[ib] = jnp.where(d["mask_b"], cur["att"][1], 0.0).astype(bf)
        lhs = jnp.concatenate([jnp.concatenate([cur["q_rows"][0], att[0]], axis=1),
                               jnp.concatenate([cur["q_rows"][1], att[1]], axis=1)], axis=0)
        rhs = jnp.concatenate([state.astype(bf), v], axis=0)
        o = jnp.dot(lhs, rhs, preferred_element_type=jnp.float32)
        d["o"][cur["rows"], st["vsl"]] = o.astype(d["o"].dtype)
        upd = lax.dot_general(cur["k_pair"], v, (((0,), (0,)), ((), ())),
                              preferred_element_type=jnp.float32)
        c0 = 2 * cur["p"]
        decay = st["decay_t"][:, c0:c0 + 1] * st["decay_t"][:, c0 + 1:c0 + 2]
        d["s"][st["h"]] = decay * state + upd
    after_rounds()


def _out_body(x_ref, of_ref, ob_ref, sz_ref, yc_ref, wo_ref, fg_ref, out_ref):
    acc = jnp.dot(yc_ref[...], wo_ref[GLA_V_W:, :], preferred_element_type=jnp.float32)
    for h in range(GLA_HEADS):
        sl = slice(h * GLA_DV, (h + 1) * GLA_DV)
        o_h = (of_ref[:, sl] + ob_ref[:, sl]).astype(jnp.float32)
        ms = jnp.mean(o_h * o_h, axis=-1, keepdims=True)
        y_h = o_h * lax.rsqrt(ms + EPS) * sz_ref[:, sl].astype(jnp.float32)
        acc = acc + jnp.dot(y_h.astype(jnp.bfloat16), wo_ref[sl, :],
                            preferred_element_type=jnp.float32)
    xo = x_ref[...] + acc
    ms = jnp.mean(xo * xo, axis=-1, keepdims=True)
    out_ref[...] = xo * lax.rsqrt(ms + EPS) * fg_ref[...]


def _mix_kernel(qf_ref, kf_ref, vf_ref, lrf_ref, qb_ref, kb_ref, vb_ref, lrb_ref,
                lrf_next_ref, lrb_next_ref, wgf_ref, wgb_ref, trif_ref, trib_ref,
                x_ref, sz_ref, yc_ref, wo_ref, fg_ref, out_ref, s_ref, o_ref, b_ref,
                gs_ref):
    s = pl.program_id(1)
    nblk = pl.num_programs(1) // 2
    slot = s % 2

    @pl.when((pl.program_id(0) == 0) & (s == 0))
    def _():
        _gate_terms_into(lrf_ref, wgf_ref, gs_ref.at[0])
        _gate_terms_into(lrb_ref, wgb_ref, gs_ref.at[1])
        _cumsum_into(gs_ref.at[0], trif_ref, b_ref.at[0, 0])
        _cumsum_into(gs_ref.at[1], trib_ref, b_ref.at[0, 1])

    @pl.when(s < nblk)
    def _():
        def gates_next():
            _gate_terms_into(lrf_next_ref, wgf_ref, gs_ref.at[0])
            _gate_terms_into(lrb_next_ref, wgb_ref, gs_ref.at[1])

        def sums_next():
            _cumsum_into(gs_ref.at[0], trif_ref, b_ref.at[1 - slot, 0])
            _cumsum_into(gs_ref.at[1], trib_ref, b_ref.at[1 - slot, 1])

        _gla_body(qf_ref, kf_ref, vf_ref, b_ref.at[slot, 0], qb_ref, kb_ref, vb_ref,
                  b_ref.at[slot, 1], o_ref.at[0, s], o_ref.at[1, nblk - 1 - s], s_ref, s == 0,
                  gates_next, sums_next)

    @pl.when(s >= nblk)
    def _():
        _out_body(x_ref, o_ref.at[0, s - nblk], o_ref.at[1, s - nblk], sz_ref, yc_ref,
                  wo_ref, fg_ref, out_ref)


def _mix(q, k, v, lr, wg_f, wg_b, x2, sz, yc, wo, fg, *, batch, seq):
    ts = TILE
    nblk = seq // ts
    assert nblk % 2 == 0
    gla_s = lambda s: jnp.minimum(s, nblk - 1)

    def following(b, s):
        nxt = jnp.minimum(b * nblk + gla_s(s) + 1, batch * nblk - 1)
        return nxt // nblk, nxt % nblk

    def fwd_next(b, s):
        b1, s1 = following(b, s)
        return (b1 * nblk + s1, 0)

    def bwd_next(b, s):
        b1, s1 = following(b, s)
        return (b1 * nblk + (nblk - 1 - s1), 0)

    fwd = lambda b, s: (b * nblk + gla_s(s), 0)
    bwd = lambda b, s: (b * nblk + (nblk - 1 - gla_s(s)), 0)
    tile = lambda b, s: (b * nblk + jnp.maximum(s - nblk, 0), 0)
    rows = lambda idx: [pl.BlockSpec((ts, w), idx) for w in (GLA_QK_W, GLA_QK_W, GLA_V_W, LR_W)]
    const = lambda shp, **kw: pl.BlockSpec(shp, lambda b, s: (0, 0), **kw)
    return pl.pallas_call(
        _mix_kernel,
        grid=(batch, 2 * nblk),
        in_specs=(rows(fwd) + rows(bwd)
                  + [pl.BlockSpec((ts, LR_W), fwd_next), pl.BlockSpec((ts, LR_W), bwd_next)]
                  + [const((LR_W, GLA_QK_W))] * 2
                  + [const((2 * CHUNK, 4 * CHUNK))] * 2
                  + [pl.BlockSpec((ts, D_MODEL), tile), pl.BlockSpec((ts, GLA_V_W), tile),
                     pl.BlockSpec((ts, CONV_W), tile),
                     const((GLA_V_W + CONV_W, D_MODEL), pipeline_mode=pl.Buffered(1)),
                     const((1, D_MODEL))]),
        out_specs=pl.BlockSpec((ts, D_MODEL), tile),
        out_shape=jax.ShapeDtypeStruct((batch * seq, D_MODEL), jnp.float32),
        scratch_shapes=[pltpu.VMEM((2, GLA_HEADS, GLA_DK, GLA_DV), jnp.float32),
                        pltpu.VMEM((2, nblk, ts, GLA_V_W), jnp.bfloat16),
                        pltpu.VMEM((2, 2, ts, GLA_QK_W), jnp.float32),
                        pltpu.VMEM((2, 2 * ts, GLA_QK_W), jnp.bfloat16)],
        compiler_params=pltpu.CompilerParams(
            dimension_semantics=("arbitrary", "arbitrary"), vmem_limit_bytes=VMEM_LIMIT),
        name="mix",
    )(q, k, v, lr, q, k, v, lr, lr, lr, wg_f, wg_b, _pair_tri(False), _pair_tri(True),
      x2, sz, yc, wo, fg)


def _regroup_kernel(a_ref, b_ref, o_ref):
    j = pl.program_id(0)
    n_before = _OFF_BG // REGROUP_TILE
    n_main = _OFF_LR // REGROUP_TILE
    lr = 2 * GATE_RANK

    def put(rows):
        o_ref[...] = jnp.transpose(rows).astype(o_ref.dtype)

    @pl.when(j < n_before)
    def _():
        put(a_ref[...])

    @pl.when((j >= n_before) & (j < n_main))
    def _():
        put(jnp.concatenate([a_ref[lr:, :], b_ref[...]], axis=0))

    @pl.when(j == n_main)
    def _():
        put(jnp.concatenate([a_ref[:lr, :], jnp.zeros((REGROUP_TILE - lr, a_ref.shape[1]),
                                                      a_ref.dtype)], axis=0))


def _regroup_w_in(w):
    wt = jnp.swapaxes(w, 1, 2)
    _, n, k = wt.shape
    lr = 2 * GATE_RANK
    assert lr == REGROUP_HALO and _OFF_BG % REGROUP_TILE == 0 and _OFF_LR % REGROUP_TILE == 0
    n_main = _OFF_LR // REGROUP_TILE
    sub = REGROUP_TILE // REGROUP_HALO
    first = lambda j: jnp.where(j == n_main, _OFF_BG // REGROUP_TILE, j)
    return pl.pallas_call(
        _regroup_kernel,
        grid=(W_ALL // REGROUP_TILE,),
        in_specs=[pl.BlockSpec((None, REGROUP_TILE, k), lambda j: (0, first(j), 0)),
                  pl.BlockSpec((None, REGROUP_HALO, k),
                               lambda j: (0, jnp.minimum((j + 1) * sub, n // REGROUP_HALO - 1), 0))],
        out_specs=pl.BlockSpec((k, REGROUP_TILE), lambda j: (0, j)),
        out_shape=jax.ShapeDtypeStruct((k, W_ALL), jnp.bfloat16),
        compiler_params=pltpu.CompilerParams(dimension_semantics=("arbitrary",)),
        name="regroup",
    )(wt, wt)


def _gate_weight(w_gk, b_gk, first_row):
    bias = b_gk * LOG2_E
    bias_hi = bias.astype(jnp.bfloat16)
    bias_lo = (bias - bias_hi.astype(jnp.float32)).astype(jnp.bfloat16)
    full = jnp.zeros((LR_W, GLA_QK_W), jnp.bfloat16)
    full = lax.dynamic_update_slice(full, (w_gk * LOG2_E).astype(jnp.bfloat16), (first_row, 0))
    return lax.dynamic_update_slice(full, jnp.stack([bias_hi, bias_lo]), (ONES_COL, 0))


def kernel(x, norm_g, w_in, w_gk_f, b_gk_f, w_gk_b, b_gk_b, gla_norm_g, conv_w, conv_b, w_out, final_g):
    batch, seq, d = x.shape
    depth = w_in.shape[0]
    assert depth == 1 and d == D_MODEL
    assert seq % TILE == 0
    x2 = x.reshape(batch * seq, d)
    q, k, v, sz, yc, lr = _inproj(x2, norm_g[0][None, :], _regroup_w_in(w_in),
                                  conv_w[0], conv_b[0][None, :], seq=seq)
    gain = jnp.concatenate([jnp.tile(gla_norm_g[0], GLA_HEADS), jnp.ones((CONV_W,), jnp.float32)])
    wo = (w_out[0] * gain[:, None]).astype(jnp.bfloat16)
    out = _mix(q, k, v, lr, _gate_weight(w_gk_f[0], b_gk_f[0], 0),
               _gate_weight(w_gk_b[0], b_gk_b[0], GATE_RANK), x2, sz, yc, wo, final_g[None, :],
               batch=batch, seq=seq)
    return out.reshape(batch, seq, d)
```

```python
import functools

import jax
import jax.numpy as jnp
import numpy as np
from jax import lax
from jax.experimental import pallas as pl
from jax.experimental.pallas import tpu as pltpu

D_MODEL = 1024
GLA_HEADS = 4
GLA_DK = 128
GLA_DV = 256
GLA_QK_W = GLA_HEADS * GLA_DK
GLA_V_W = GLA_HEADS * GLA_DV
GATE_RANK = 16
GATE_NORM = 16.0
CHUNK = 64
CONV_W = 1024
EPS = 1e-6
LOG2_E = 1.4426950408889634

LANES = 128
HALO = 8
LR_W = LANES
ONES_COL = 2 * GATE_RANK

_OFF_Q = 0
_OFF_K = _OFF_Q + GLA_QK_W
_OFF_V = _OFF_K + GLA_QK_W
_OFF_ZA = _OFF_V + GLA_V_W
_OFF_BG = _OFF_ZA + GLA_V_W
_OFF_CG = _OFF_BG + CONV_W
_OFF_HC = _OFF_CG + CONV_W
_OFF_ZC = _OFF_HC + CONV_W
_OFF_LR = _OFF_ZC + CONV_W
REGROUP_TILE = 4 * LANES
REGROUP_HALO = 2 * GATE_RANK
W_ALL = _OFF_LR + REGROUP_TILE

TILE = 512
VMEM_LIMIT = 58 * 1024 * 1024
SCORE_LEAD = 12


def _silu(z):
    return z * (1.0 / (1.0 + jnp.exp(-z)))


def _inproj_kernel(x_ref, xp_ref, xn_ref, g_ref, w_ref, cw_ref, cb_ref,
                   q_ref, k_ref, v_ref, sz_ref, yc_ref, lr_ref, *, tiles_per_seq):
    tm = x_ref.shape[0]
    pos = pl.program_id(0) % tiles_per_seq

    def normed(x):
        ms = jnp.mean(x * x, axis=-1, keepdims=True)
        return x * lax.rsqrt(ms + EPS) * g_ref[...]

    h = normed(x_ref[...]).astype(jnp.bfloat16)
    halo = jnp.concatenate([normed(xp_ref[...]), normed(xn_ref[...])], axis=0).astype(jnp.bfloat16)
    h_ext = jnp.concatenate([h, halo], axis=0)

    def proj(lo, hi, lhs=h):
        return jnp.dot(lhs, w_ref[:, lo:hi], preferred_element_type=jnp.float32)

    u_ext = proj(_OFF_CG, _OFF_HC, h_ext) * proj(_OFF_HC, _OFF_ZC, h_ext)
    u = u_ext[:tm]
    prev_row = jnp.where(pos == 0, 0.0, u_ext[tm + HALO - 1:tm + HALO])
    next_row = jnp.where(pos == tiles_per_seq - 1, 0.0, u_ext[tm + HALO:tm + HALO + 1])
    t_i = lax.broadcasted_iota(jnp.int32, (tm, 1), 0)
    u_prev = jnp.where(t_i == 0, prev_row, pltpu.roll(u, 1, 0))
    u_next = jnp.where(t_i == tm - 1, next_row, pltpu.roll(u, tm - 1, 0))
    conv = cw_ref[0:1, :] * u_prev + cw_ref[1:2, :] * u + cw_ref[2:3, :] * u_next + cb_ref[...]
    yc_ref[...] = (proj(_OFF_BG, _OFF_CG) * conv * _silu(proj(_OFF_ZC, _OFF_LR))).astype(yc_ref.dtype)

    sz_ref[...] = _silu(proj(_OFF_ZA, _OFF_BG)).astype(sz_ref.dtype)
    q_ref[...] = (proj(_OFF_Q, _OFF_K) * (GLA_DK ** -0.5)).astype(q_ref.dtype)
    k_ref[...] = proj(_OFF_K, _OFF_V).astype(k_ref.dtype)
    v_ref[...] = proj(_OFF_V, _OFF_ZA).astype(v_ref.dtype)
    col = lax.broadcasted_iota(jnp.int32, (1, LR_W), 1)
    ones = jnp.where((col == ONES_COL) | (col == ONES_COL + 1), 1.0, 0.0)
    lr_ref[...] = proj(_OFF_LR, _OFF_LR + LR_W) + ones


def _inproj(x2, norm_g, w_all, conv_w, conv_b, *, seq):
    m = x2.shape[0]
    tm = TILE
    sub = tm // HALO
    nsub = m // HALO
    row = lambda w: pl.BlockSpec((tm, w), lambda i: (i, 0))
    const = lambda shp: pl.BlockSpec(shp, lambda i: (0, 0))
    halo_prev = pl.BlockSpec((HALO, D_MODEL), lambda i: (jnp.maximum(i * sub - 1, 0), 0))
    halo_next = pl.BlockSpec((HALO, D_MODEL), lambda i: (jnp.minimum((i + 1) * sub, nsub - 1), 0))
    bf = jnp.bfloat16
    return pl.pallas_call(
        functools.partial(_inproj_kernel, tiles_per_seq=seq // tm),
        grid=(m // tm,),
        in_specs=[
            row(D_MODEL), halo_prev, halo_next,
            const((1, D_MODEL)),
            pl.BlockSpec((D_MODEL, W_ALL), lambda i: (0, 0), pipeline_mode=pl.Buffered(1)),
            const((3, CONV_W)), const((1, CONV_W)),
        ],
        out_specs=[row(GLA_QK_W), row(GLA_QK_W), row(GLA_V_W), row(GLA_V_W),
                   row(CONV_W), row(LR_W)],
        out_shape=[
            jax.ShapeDtypeStruct((m, GLA_QK_W), bf),
            jax.ShapeDtypeStruct((m, GLA_QK_W), bf),
            jax.ShapeDtypeStruct((m, GLA_V_W), bf),
            jax.ShapeDtypeStruct((m, GLA_V_W), bf),
            jax.ShapeDtypeStruct((m, CONV_W), bf),
            jax.ShapeDtypeStruct((m, LR_W), jnp.float32),
        ],
        compiler_params=pltpu.CompilerParams(
            dimension_semantics=("arbitrary",), vmem_limit_bytes=VMEM_LIMIT),
        name="inproj",
    )(x2, x2, x2, norm_g, w_all, conv_w, conv_b)


def _pair_tri(rev):
    t = np.arange(CHUNK)
    tri = (t[None, :] >= t[:, None]) if rev else (t[None, :] <= t[:, None])
    zero = np.zeros_like(tri)
    keep = np.block([[tri, tri, zero, zero], [zero, zero, tri, tri]])
    return jnp.asarray(keep / GATE_NORM, dtype=jnp.bfloat16)


def _gate_terms_into(lr_ref, wg_ref, gs_ref):
    ts = lr_ref.shape[0]
    x = jnp.dot(lr_ref[...].astype(jnp.bfloat16), wg_ref[...], preferred_element_type=jnp.float32)
    g = jnp.minimum(x, 0.0) - jnp.log2(1.0 + jnp.exp2(-jnp.abs(x)))
    g_top = lax.bitcast_convert_type(
        lax.bitcast_convert_type(g, jnp.uint32) & jnp.uint32(0xFFFF0000), jnp.float32)
    g_hi = g_top.astype(gs_ref.dtype)
    g_lo = (g - g_top).astype(gs_ref.dtype)
    for c in range(ts // CHUNK):
        rows = slice(c * CHUNK, (c + 1) * CHUNK)
        gs_ref[2 * c * CHUNK:(2 * c + 1) * CHUNK, :] = g_hi[rows]
        gs_ref[(2 * c + 1) * CHUNK:(2 * c + 2) * CHUNK, :] = g_lo[rows]


def _cumsum_into(gs_ref, tri_ref, b_ref):
    tri = tri_ref[...]
    for p in range(b_ref.shape[0] // (2 * CHUNK)):
        b_ref[2 * p * CHUNK:(2 * p + 2) * CHUNK, :] = jnp.dot(
            tri, gs_ref[4 * p * CHUNK:(4 * p + 4) * CHUNK, :], preferred_element_type=jnp.float32)


def _gla_body(qf_ref, kf_ref, vf_ref, bf_ref, qb_ref, kb_ref, vb_ref, bb_ref,
              of_ref, ob_ref, s_ref, first_block, mid_round, after_rounds):
    ts = qf_ref.shape[0]
    nc = ts // CHUNK
    npair = nc // 2
    pair = 2 * CHUNK

    @pl.when(first_block)
    def _():
        s_ref[...] = jnp.zeros_like(s_ref)

    ri = lax.broadcasted_iota(jnp.int32, (CHUNK, pair), 0)
    li = lax.broadcasted_iota(jnp.int32, (CHUNK, pair), 1)
    lo_half = li < CHUNK
    dirs = (
        dict(q=qf_ref, k=kf_ref, v=vf_ref, o=of_ref, s=s_ref.at[0], a=0, b_=1,
             mask_a=lo_half & (li <= ri), mask_b=lo_half | (li - CHUNK <= ri),
             ref_row=CHUNK // 2, last_row=CHUNK - 1, order=tuple(range(npair)),
             b=bf_ref),
        dict(q=qb_ref, k=kb_ref, v=vb_ref, o=ob_ref, s=s_ref.at[1], a=1, b_=0,
             mask_a=(~lo_half) & (li - CHUNK > ri), mask_b=(~lo_half) | (li > ri),
             ref_row=CHUNK - 1 - CHUNK // 2, last_row=0, order=tuple(range(npair - 1, -1, -1)),
             b=bb_ref),
    )

    streams = []
    for d in dirs:
        b_last_rows = jnp.concatenate(
            [d["b"][c * CHUNK + d["last_row"]:c * CHUNK + d["last_row"] + 1, :] for c in range(nc)],
            axis=0)
        for h in range(GLA_HEADS):
            ksl = slice(h * GLA_DK, (h + 1) * GLA_DK)
            streams.append(dict(
                d=d, h=h, ksl=ksl, vsl=slice(h * GLA_DV, (h + 1) * GLA_DV),
                decay_t=jnp.transpose(jnp.exp2(b_last_rows[:, ksl])),
            ))

    bf = jnp.bfloat16
    zeros_k = jnp.zeros((CHUNK, GLA_DK), bf)

    def intra(st, step):
        d = st["d"]
        p = d["order"][step]
        rows = slice(p * pair, (p + 1) * pair)
        ia, ib = d["a"], d["b_"]
        b = d["b"][rows, st["ksl"]].reshape(2, CHUNK, GLA_DK)
        b_mid = b[:, d["ref_row"]:d["ref_row"] + 1, :]
        tot = b[:, d["last_row"]:d["last_row"] + 1, :]
        qe = (d["q"][rows, st["ksl"]].astype(jnp.float32).reshape(2, CHUNK, GLA_DK)
              * jnp.exp2(b - b_mid))
        ke = (d["k"][rows, st["ksl"]].astype(jnp.float32).reshape(2, CHUNK, GLA_DK)
              * jnp.exp2(b_mid - b))
        f_q = jnp.exp2(b_mid)
        f_k = jnp.exp2(tot - b_mid)
        q_in_a = (qe[ia] * f_q[ia]).astype(bf)
        q_in_b = (qe[ib] * f_q[ib]).astype(bf)
        q_in_b2 = (qe[ib] * (f_q[ib] * jnp.exp2(tot[ia]))).astype(bf)
        k_out_a = (ke[ia] * f_k[ia]).astype(bf)
        k_out_a2 = (ke[ia] * (f_k[ia] * jnp.exp2(tot[ib]))).astype(bf)
        k_out_b = (ke[ib] * f_k[ib]).astype(bf)
        ke_a = ke[ia].astype(bf)
        ke_b = ke[ib].astype(bf)
        nt = (((1,), (1,)), ((), ()))
        half = lambda x, slot: jnp.concatenate([x, zeros_k] if slot == 0 else [zeros_k, x], axis=0)
        att_a = lax.dot_general(qe[ia].astype(bf), half(ke_a, ia), nt,
                                preferred_element_type=jnp.float32)
        rhs_b = [None, None]
        rhs_b[ia] = jnp.concatenate([k_out_a, zeros_k], axis=1)
        rhs_b[ib] = jnp.concatenate([zeros_k, ke_b], axis=1)
        att_b = lax.dot_general(jnp.concatenate([q_in_b, qe[ib].astype(bf)], axis=1),
                                jnp.concatenate(rhs_b, axis=0), nt,
                                preferred_element_type=jnp.float32)
        q_rows = [None, None]
        q_rows[ia], q_rows[ib] = q_in_a, q_in_b2
        k_rows = [None, None]
        k_rows[ia], k_rows[ib] = k_out_a2, k_out_b
        return dict(p=p, rows=rows, att=(att_a, att_b), q_rows=q_rows,
                    k_pair=jnp.concatenate(k_rows, axis=0))

    items = [(step, st) for step in range(npair) for st in streams]
    ahead = {(step, id(st)): intra(st, step) for step, st in items[:SCORE_LEAD]}
    for i, (step, st) in enumerate(items):
        if step == npair // 2 and st is streams[0]:
            mid_round()
        d = st["d"]
        cur = ahead.pop((step, id(st)))
        if i + SCORE_LEAD < len(items):
            nstep, nst = items[i + SCORE_LEAD]
            ahead[(nstep, id(nst))] = intra(nst, nstep)
        ia, ib = d["a"], d["b_"]
        v = d["v"][cur["rows"], st["vsl"]]
        state = d["s"][st["h"]]
        att = [None, None]
        att[ia] = jnp.where(d["mask_a"], cur["att"][0], 0.0).astype(bf)
        att[ib] = jnp.where(d["mask_b"], cur["att"][1], 0.0).astype(bf)
        lhs = jnp.concatenate([jnp.concatenate([cur["q_rows"][0], att[0]], axis=1),
                               jnp.concatenate([cur["q_rows"][1], att[1]], axis=1)], axis=0)
        rhs = jnp.concatenate([state.astype(bf), v], axis=0)
        o = jnp.dot(lhs, rhs, preferred_element_type=jnp.float32)
        d["o"][cur["rows"], st["vsl"]] = o.astype(d["o"].dtype)
        upd = lax.dot_general(cur["k_pair"], v, (((0,), (0,)), ((), ())),
                              preferred_element_type=jnp.float32)
        c0 = 2 * cur["p"]
        decay = st["decay_t"][:, c0:c0 + 1] * st["decay_t"][:, c0 + 1:c0 + 2]
        d["s"][st["h"]] = decay * state + upd
    after_rounds()


def _out_body(x_ref, of_ref, ob_ref, sz_ref, yc_ref, gn_ref, wo_ref, fg_ref, out_ref):
    acc = jnp.dot(yc_ref[...], wo_ref[GLA_V_W:, :], preferred_element_type=jnp.float32)
    for h in range(GLA_HEADS):
        sl = slice(h * GLA_DV, (h + 1) * GLA_DV)
        o_h = of_ref[:, sl].astype(jnp.float32) + ob_ref[:, sl].astype(jnp.float32)
        ms = jnp.mean(o_h * o_h, axis=-1, keepdims=True)
        y_h = o_h * lax.rsqrt(ms + EPS) * gn_ref[...] * sz_ref[:, sl].astype(jnp.float32)
        acc = acc + jnp.dot(y_h.astype(jnp.bfloat16), wo_ref[sl, :],
                            preferred_element_type=jnp.float32)
    xo = x_ref[...] + acc
    ms = jnp.mean(xo * xo, axis=-1, keepdims=True)
    out_ref[...] = xo * lax.rsqrt(ms + EPS) * fg_ref[...]


def _mix_kernel(qf_ref, kf_ref, vf_ref, lrf_ref, qb_ref, kb_ref, vb_ref, lrb_ref,
                lrf_next_ref, lrb_next_ref, wgf_ref, wgb_ref, trif_ref, trib_ref,
                x_ref, sz_ref, yc_ref, gn_ref, wo_ref, fg_ref, out_ref, s_ref, o_ref, b_ref,
                gs_ref):
    s = pl.program_id(1)
    nblk = pl.num_programs(1) // 2
    slot = s % 2

    @pl.when((pl.program_id(0) == 0) & (s == 0))
    def _():
        _gate_terms_into(lrf_ref, wgf_ref, gs_ref.at[0])
        _gate_terms_into(lrb_ref, wgb_ref, gs_ref.at[1])
        _cumsum_into(gs_ref.at[0], trif_ref, b_ref.at[0, 0])
        _cumsum_into(gs_ref.at[1], trib_ref, b_ref.at[0, 1])

    @pl.when(s < nblk)
    def _():
        def gates_next():
            _gate_terms_into(lrf_next_ref, wgf_ref, gs_ref.at[0])
            _gate_terms_into(lrb_next_ref, wgb_ref, gs_ref.at[1])

        def sums_next():
            _cumsum_into(gs_ref.at[0], trif_ref, b_ref.at[1 - slot, 0])
            _cumsum_into(gs_ref.at[1], trib_ref, b_ref.at[1 - slot, 1])

        _gla_body(qf_ref, kf_ref, vf_ref, b_ref.at[slot, 0], qb_ref, kb_ref, vb_ref,
                  b_ref.at[slot, 1], o_ref.at[0, s], o_ref.at[1, nblk - 1 - s], s_ref, s == 0,
                  gates_next, sums_next)

    @pl.when(s >= nblk)
    def _():
        _out_body(x_ref, o_ref.at[0, s - nblk], o_ref.at[1, s - nblk], sz_ref, yc_ref,
                  gn_ref, wo_ref, fg_ref, out_ref)


def _mix(q, k, v, lr, wg_f, wg_b, x2, sz, yc, gn, wo, fg, *, batch, seq):
    ts = TILE
    nblk = seq // ts
    assert nblk % 2 == 0
    gla_s = lambda s: jnp.minimum(s, nblk - 1)

    def following(b, s):
        nxt = jnp.minimum(b * nblk + gla_s(s) + 1, batch * nblk - 1)
        return nxt // nblk, nxt % nblk

    def fwd_next(b, s):
        b1, s1 = following(b, s)
        return (b1 * nblk + s1, 0)

    def bwd_next(b, s):
        b1, s1 = following(b, s)
        return (b1 * nblk + (nblk - 1 - s1), 0)

    fwd = lambda b, s: (b * nblk + gla_s(s), 0)
    bwd = lambda b, s: (b * nblk + (nblk - 1 - gla_s(s)), 0)
    tile = lambda b, s: (b * nblk + jnp.maximum(s - nblk, 0), 0)
    rows = lambda idx: [pl.BlockSpec((ts, w), idx) for w in (GLA_QK_W, GLA_QK_W, GLA_V_W, LR_W)]
    streamed_specs = (rows(fwd) + rows(bwd)
                      + [pl.BlockSpec((ts, LR_W), fwd_next), pl.BlockSpec((ts, LR_W), bwd_next)]
                      + [pl.BlockSpec((ts, D_MODEL), tile), pl.BlockSpec((ts, GLA_V_W), tile),
                         pl.BlockSpec((ts, CONV_W), tile)])
    out_spec = pl.BlockSpec((ts, D_MODEL), tile)
    streamed = (q, k, v, lr, q, k, v, lr, lr, lr, x2, sz, yc)
    resident = (wg_f, wg_b, _pair_tri(False), _pair_tri(True), gn, wo, fg)

    def call_body(*refs):
        ins = refs[:len(streamed)]
        wgf_ref, wgb_ref, trif_ref, trib_ref, gn_ref, wo_ref, fg_ref = refs[len(streamed):-5]
        out_hbm, s_ref, o_ref, b_ref, gs_ref = refs[-5:]

        def step(*blocks):
            (qf_ref, kf_ref, vf_ref, lrf_ref, qb_ref, kb_ref, vb_ref, lrb_ref,
             lrf_next_ref, lrb_next_ref, x_ref, sz_ref, yc_ref, out_ref) = blocks
            _mix_kernel(qf_ref, kf_ref, vf_ref, lrf_ref, qb_ref, kb_ref, vb_ref, lrb_ref,
                        lrf_next_ref, lrb_next_ref, wgf_ref, wgb_ref, trif_ref, trib_ref,
                        x_ref, sz_ref, yc_ref, gn_ref, wo_ref, fg_ref, out_ref,
                        s_ref, o_ref, b_ref, gs_ref)

        pltpu.emit_pipeline(step, grid=(batch, 2 * nblk), in_specs=streamed_specs,
                            out_specs=[out_spec])(*ins, out_hbm)

    return pl.pallas_call(
        call_body,
        in_specs=([pl.BlockSpec(memory_space=pl.ANY)] * len(streamed)
                  + [pl.BlockSpec(memory_space=pltpu.VMEM)] * len(resident)),
        out_specs=pl.BlockSpec(memory_space=pl.ANY),
        out_shape=jax.ShapeDtypeStruct((batch * seq, D_MODEL), jnp.float32),
        scratch_shapes=[pltpu.VMEM((2, GLA_HEADS, GLA_DK, GLA_DV), jnp.float32),
                        pltpu.VMEM((2, nblk, ts, GLA_V_W), jnp.bfloat16),
                        pltpu.VMEM((2, 2, ts, GLA_QK_W), jnp.float32),
                        pltpu.VMEM((2, 2 * ts, GLA_QK_W), jnp.bfloat16)],
        compiler_params=pltpu.CompilerParams(vmem_limit_bytes=VMEM_LIMIT),
        name="mix",
    )(*streamed, *resident)


def _regroup_kernel(a_ref, b_ref, o_ref):
    j = pl.program_id(0)
    n_before = _OFF_BG // REGROUP_TILE
    n_main = _OFF_LR // REGROUP_TILE
    lr = 2 * GATE_RANK

    def put(rows):
        o_ref[...] = jnp.transpose(rows).astype(o_ref.dtype)

    @pl.when(j < n_before)
    def _():
        put(a_ref[...])

    @pl.when((j >= n_before) & (j < n_main))
    def _():
        put(jnp.concatenate([a_ref[lr:, :], b_ref[...]], axis=0))

    @pl.when(j == n_main)
    def _():
        put(jnp.concatenate([a_ref[:lr, :], jnp.zeros((REGROUP_TILE - lr, a_ref.shape[1]),
                                                      a_ref.dtype)], axis=0))


def _regroup_w_in(w):
    wt = jnp.swapaxes(w, 1, 2)
    _, n, k = wt.shape
    lr = 2 * GATE_RANK
    assert lr == REGROUP_HALO and _OFF_BG % REGROUP_TILE == 0 and _OFF_LR % REGROUP_TILE == 0
    n_main = _OFF_LR // REGROUP_TILE
    sub = REGROUP_TILE // REGROUP_HALO
    first = lambda j: jnp.where(j == n_main, _OFF_BG // REGROUP_TILE, j)
    return pl.pallas_call(
        _regroup_kernel,
        grid=(W_ALL // REGROUP_TILE,),
        in_specs=[pl.BlockSpec((None, REGROUP_TILE, k), lambda j: (0, first(j), 0)),
                  pl.BlockSpec((None, REGROUP_HALO, k),
                               lambda j: (0, jnp.minimum((j + 1) * sub, n // REGROUP_HALO - 1), 0))],
        out_specs=pl.BlockSpec((k, REGROUP_TILE), lambda j: (0, j)),
        out_shape=jax.ShapeDtypeStruct((k, W_ALL), jnp.bfloat16),
        compiler_params=pltpu.CompilerParams(dimension_semantics=("arbitrary",)),
        name="regroup",
    )(wt, wt)


def _gate_weight(w_gk, b_gk, first_row):
    bias = b_gk * LOG2_E
    bias_hi = bias.astype(jnp.bfloat16)
    bias_lo = (bias - bias_hi.astype(jnp.float32)).astype(jnp.bfloat16)
    full = jnp.zeros((LR_W, GLA_QK_W), jnp.bfloat16)
    full = lax.dynamic_update_slice(full, (w_gk * LOG2_E).astype(jnp.bfloat16), (first_row, 0))
    return lax.dynamic_update_slice(full, jnp.stack([bias_hi, bias_lo]), (ONES_COL, 0))


def kernel(x, norm_g, w_in, w_gk_f, b_gk_f, w_gk_b, b_gk_b, gla_norm_g, conv_w, conv_b, w_out, final_g):
    batch, seq, d = x.shape
    depth = w_in.shape[0]
    assert depth == 1 and d == D_MODEL
    assert seq % TILE == 0
    x2 = x.reshape(batch * seq, d)
    q, k, v, sz, yc, lr = _inproj(x2, norm_g[0][None, :], _regroup_w_in(w_in),
                                  conv_w[0], conv_b[0][None, :], seq=seq)
    out = _mix(q, k, v, lr, _gate_weight(w_gk_f[0], b_gk_f[0], 0),
               _gate_weight(w_gk_b[0], b_gk_b[0], GATE_RANK), x2, sz, yc,
               gla_norm_g[0][None, :], w_out[0].astype(jnp.bfloat16), final_g[None, :],
               batch=batch, seq=seq)
    return out.reshape(batch, seq, d)
```

```python
import functools

import jax
import jax.numpy as jnp
import numpy as np
from jax import lax
from jax.experimental import pallas as pl
from jax.experimental.pallas import tpu as pltpu

D_MODEL = 1024
GLA_HEADS = 4
GLA_DK = 128
GLA_DV = 256
GLA_QK_W = GLA_HEADS * GLA_DK
GLA_V_W = GLA_HEADS * GLA_DV
GATE_RANK = 16
GATE_NORM = 16.0
CHUNK = 64
CONV_W = 1024
EPS = 1e-6
LOG2_E = 1.4426950408889634

LANES = 128
HALO = 8
LR_W = LANES
ONES_COL = 2 * GATE_RANK

_OFF_Q = 0
_OFF_K = _OFF_Q + GLA_QK_W
_OFF_V = _OFF_K + GLA_QK_W
_OFF_ZA = _OFF_V + GLA_V_W
_OFF_BG = _OFF_ZA + GLA_V_W
_OFF_CG = _OFF_BG + CONV_W
_OFF_HC = _OFF_CG + CONV_W
_OFF_ZC = _OFF_HC + CONV_W
_OFF_LR = _OFF_ZC + CONV_W
REGROUP_TILE = 4 * LANES
REGROUP_HALO = 2 * GATE_RANK
W_ALL = _OFF_LR + REGROUP_TILE

TILE = 512
VMEM_LIMIT = 58 * 1024 * 1024
SCORE_LEAD = 12


def _silu(z):
    return z * (1.0 / (1.0 + jnp.exp(-z)))


def _inproj_kernel(x_ref, xp_ref, xn_ref, g_ref, w_ref, cw_ref, cb_ref,
                   q_ref, k_ref, v_ref, sz_ref, yc_ref, lr_ref, *, tiles_per_seq):
    tm = x_ref.shape[0]
    pos = pl.program_id(0) % tiles_per_seq

    def normed(x):
        ms = jnp.mean(x * x, axis=-1, keepdims=True)
        return x * lax.rsqrt(ms + EPS) * g_ref[...]

    h = normed(x_ref[...]).astype(jnp.bfloat16)
    halo = jnp.concatenate([normed(xp_ref[...]), normed(xn_ref[...])], axis=0).astype(jnp.bfloat16)
    h_ext = jnp.concatenate([h, halo], axis=0)

    def proj(lo, hi, lhs=h):
        return jnp.dot(lhs, w_ref[:, lo:hi], preferred_element_type=jnp.float32)

    u_ext = proj(_OFF_CG, _OFF_HC, h_ext) * proj(_OFF_HC, _OFF_ZC, h_ext)
    u = u_ext[:tm]
    prev_row = jnp.where(pos == 0, 0.0, u_ext[tm + HALO - 1:tm + HALO])
    next_row = jnp.where(pos == tiles_per_seq - 1, 0.0, u_ext[tm + HALO:tm + HALO + 1])
    t_i = lax.broadcasted_iota(jnp.int32, (tm, 1), 0)
    u_prev = jnp.where(t_i == 0, prev_row, pltpu.roll(u, 1, 0))
    u_next = jnp.where(t_i == tm - 1, next_row, pltpu.roll(u, tm - 1, 0))
    conv = cw_ref[0:1, :] * u_prev + cw_ref[1:2, :] * u + cw_ref[2:3, :] * u_next + cb_ref[...]
    yc_ref[...] = (proj(_OFF_BG, _OFF_CG) * conv * _silu(proj(_OFF_ZC, _OFF_LR))).astype(yc_ref.dtype)

    sz_ref[...] = _silu(proj(_OFF_ZA, _OFF_BG)).astype(sz_ref.dtype)
    q_ref[...] = (proj(_OFF_Q, _OFF_K) * (GLA_DK ** -0.5)).astype(q_ref.dtype)
    k_ref[...] = proj(_OFF_K, _OFF_V).astype(k_ref.dtype)
    v_ref[...] = proj(_OFF_V, _OFF_ZA).astype(v_ref.dtype)
    col = lax.broadcasted_iota(jnp.int32, (1, LR_W), 1)
    ones = jnp.where((col == ONES_COL) | (col == ONES_COL + 1), 1.0, 0.0)
    lr_ref[...] = proj(_OFF_LR, _OFF_LR + LR_W) + ones


def _inproj(x2, norm_g, w_all, conv_w, conv_b, *, seq):
    m = x2.shape[0]
    tm = TILE
    sub = tm // HALO
    nsub = m // HALO
    row = lambda w: pl.BlockSpec((tm, w), lambda i: (i, 0))
    const = lambda shp: pl.BlockSpec(shp, lambda i: (0, 0))
    halo_prev = pl.BlockSpec((HALO, D_MODEL), lambda i: (jnp.maximum(i * sub - 1, 0), 0))
    halo_next = pl.BlockSpec((HALO, D_MODEL), lambda i: (jnp.minimum((i + 1) * sub, nsub - 1), 0))
    bf = jnp.bfloat16
    return pl.pallas_call(
        functools.partial(_inproj_kernel, tiles_per_seq=seq // tm),
        grid=(m // tm,),
        in_specs=[
            row(D_MODEL), halo_prev, halo_next,
            const((1, D_MODEL)),
            pl.BlockSpec((D_MODEL, W_ALL), lambda i: (0, 0), pipeline_mode=pl.Buffered(1)),
            const((3, CONV_W)), const((1, CONV_W)),
        ],
        out_specs=[row(GLA_QK_W), row(GLA_QK_W), row(GLA_V_W), row(GLA_V_W),
                   row(CONV_W), row(LR_W)],
        out_shape=[
            jax.ShapeDtypeStruct((m, GLA_QK_W), bf),
            jax.ShapeDtypeStruct((m, GLA_QK_W), bf),
            jax.ShapeDtypeStruct((m, GLA_V_W), bf),
            jax.ShapeDtypeStruct((m, GLA_V_W), bf),
            jax.ShapeDtypeStruct((m, CONV_W), bf),
            jax.ShapeDtypeStruct((m, LR_W), jnp.float32),
        ],
        compiler_params=pltpu.CompilerParams(
            dimension_semantics=("arbitrary",), vmem_limit_bytes=VMEM_LIMIT),
        name="inproj",
    )(x2, x2, x2, norm_g, w_all, conv_w, conv_b)


def _pair_tri(rev):
    t = np.arange(CHUNK)
    tri = (t[None, :] >= t[:, None]) if rev else (t[None, :] <= t[:, None])
    zero = np.zeros_like(tri)
    keep = np.block([[tri, tri, zero, zero], [zero, zero, tri, tri]])
    return jnp.asarray(keep / GATE_NORM, dtype=jnp.bfloat16)


def _gate_terms_into(lr_ref, wg_ref, gs_ref):
    ts = lr_ref.shape[0]
    x = jnp.dot(lr_ref[...].astype(jnp.bfloat16), wg_ref[...], preferred_element_type=jnp.float32)
    g = jnp.minimum(x, 0.0) - jnp.log2(1.0 + jnp.exp2(-jnp.abs(x)))
    g_top = lax.bitcast_convert_type(
        lax.bitcast_convert_type(g, jnp.uint32) & jnp.uint32(0xFFFF0000), jnp.float32)
    g_hi = g_top.astype(gs_ref.dtype)
    g_lo = (g - g_top).astype(gs_ref.dtype)
    for c in range(ts // CHUNK):
        rows = slice(c * CHUNK, (c + 1) * CHUNK)
        gs_ref[2 * c * CHUNK:(2 * c + 1) * CHUNK, :] = g_hi[rows]
        gs_ref[(2 * c + 1) * CHUNK:(2 * c + 2) * CHUNK, :] = g_lo[rows]


def _cumsum_into(gs_ref, tri_ref, b_ref):
    tri = tri_ref[...]
    for p in range(b_ref.shape[0] // (2 * CHUNK)):
        b_ref[2 * p * CHUNK:(2 * p + 2) * CHUNK, :] = jnp.dot(
            tri, gs_ref[4 * p * CHUNK:(4 * p + 4) * CHUNK, :], preferred_element_type=jnp.float32)


def _gla_body(qf_ref, kf_ref, vf_ref, bf_ref, qb_ref, kb_ref, vb_ref, bb_ref,
              of_ref, ob_ref, s_ref, first_block, mid_round, after_rounds):
    ts = qf_ref.shape[0]
    nc = ts // CHUNK
    npair = nc // 2
    pair = 2 * CHUNK

    @pl.when(first_block)
    def _():
        s_ref[...] = jnp.zeros_like(s_ref)

    ri = lax.broadcasted_iota(jnp.int32, (CHUNK, pair), 0)
    li = lax.broadcasted_iota(jnp.int32, (CHUNK, pair), 1)
    lo_half = li < CHUNK
    dirs = (
        dict(q=qf_ref, k=kf_ref, v=vf_ref, o=of_ref, s=s_ref.at[0], a=0, b_=1,
             mask_a=lo_half & (li <= ri), mask_b=lo_half | (li - CHUNK <= ri),
             ref_row=CHUNK // 2, last_row=CHUNK - 1, order=tuple(range(npair)),
             b=bf_ref),
        dict(q=qb_ref, k=kb_ref, v=vb_ref, o=ob_ref, s=s_ref.at[1], a=1, b_=0,
             mask_a=(~lo_half) & (li - CHUNK > ri), mask_b=(~lo_half) | (li > ri),
             ref_row=CHUNK - 1 - CHUNK // 2, last_row=0, order=tuple(range(npair - 1, -1, -1)),
             b=bb_ref),
    )

    streams = []
    for d in dirs:
        b_last_rows = jnp.concatenate(
            [d["b"][c * CHUNK + d["last_row"]:c * CHUNK + d["last_row"] + 1, :] for c in range(nc)],
            axis=0)
        for h in range(GLA_HEADS):
            ksl = slice(h * GLA_DK, (h + 1) * GLA_DK)
            streams.append(dict(
                d=d, h=h, ksl=ksl, vsl=slice(h * GLA_DV, (h + 1) * GLA_DV),
                decay_t=jnp.transpose(jnp.exp2(b_last_rows[:, ksl])),
            ))

    bf = jnp.bfloat16
    zeros_k = jnp.zeros((CHUNK, GLA_DK), bf)

    def intra(st, step):
        d = st["d"]
        p = d["order"][step]
        rows = slice(p * pair, (p + 1) * pair)
        ia, ib = d["a"], d["b_"]
        b = d["b"][rows, st["ksl"]].reshape(2, CHUNK, GLA_DK)
        b_mid = b[:, d["ref_row"]:d["ref_row"] + 1, :]
        tot = b[:, d["last_row"]:d["last_row"] + 1, :]
        qe = (d["q"][rows, st["ksl"]].astype(jnp.float32).reshape(2, CHUNK, GLA_DK)
              * jnp.exp2(b - b_mid))
        ke = (d["k"][rows, st["ksl"]].astype(jnp.float32).reshape(2, CHUNK, GLA_DK)
              * jnp.exp2(b_mid - b))
        f_q = jnp.exp2(b_mid)
        f_k = jnp.exp2(tot - b_mid)
        q_in_a = (qe[ia] * f_q[ia]).astype(bf)
        q_in_b = (qe[ib] * f_q[ib]).astype(bf)
        q_in_b2 = (qe[ib] * (f_q[ib] * jnp.exp2(tot[ia]))).astype(bf)
        k_out_a = (ke[ia] * f_k[ia]).astype(bf)
        k_out_a2 = (ke[ia] * (f_k[ia] * jnp.exp2(tot[ib]))).astype(bf)
        k_out_b = (ke[ib] * f_k[ib]).astype(bf)
        ke_a = ke[ia].astype(bf)
        ke_b = ke[ib].astype(bf)
        nt = (((1,), (1,)), ((), ()))
        half = lambda x, slot: jnp.concatenate([x, zeros_k] if slot == 0 else [zeros_k, x], axis=0)
        att_a = lax.dot_general(qe[ia].astype(bf), half(ke_a, ia), nt,
                                preferred_element_type=jnp.float32)
        rhs_b = [None, None]
        rhs_b[ia] = jnp.concatenate([k_out_a, zeros_k], axis=1)
        rhs_b[ib] = jnp.concatenate([zeros_k, ke_b], axis=1)
        att_b = lax.dot_general(jnp.concatenate([q_in_b, qe[ib].astype(bf)], axis=1),
                                jnp.concatenate(rhs_b, axis=0), nt,
                                preferred_element_type=jnp.float32)
        q_rows = [None, None]
        q_rows[ia], q_rows[ib] = q_in_a, q_in_b2
        k_rows = [None, None]
        k_rows[ia], k_rows[ib] = k_out_a2, k_out_b
        return dict(p=p, rows=rows, att=(att_a, att_b), q_rows=q_rows,
                    k_pair=jnp.concatenate(k_rows, axis=0))

    items = [(step, st) for step in range(npair) for st in streams]
    ahead = {(step, id(st)): intra(st, step) for step, st in items[:SCORE_LEAD]}
    for i, (step, st) in enumerate(items):
        if step == npair // 2 and st is streams[0]:
            mid_round()
        d = st["d"]
        cur = ahead.pop((step, id(st)))
        if i + SCORE_LEAD < len(items):
            nstep, nst = items[i + SCORE_LEAD]
            ahead[(nstep, id(nst))] = intra(nst, nstep)
        ia, ib = d["a"], d["b_"]
        v = d["v"][cur["rows"], st["vsl"]]
        state = d["s"][st["h"]]
        att = [None, None]
        att[ia] = jnp.where(d["mask_a"], cur["att"][0], 0.0).astype(bf)
        att[ib] = jnp.where(d["mask_b"], cur["att"][1], 0.0).astype(bf)
        lhs = jnp.concatenate([jnp.concatenate([cur["q_rows"][0], att[0]], axis=1),
                               jnp.concatenate([cur["q_rows"][1], att[1]], axis=1)], axis=0)
        rhs = jnp.concatenate([state.astype(bf), v], axis=0)
        o = jnp.dot(lhs, rhs, preferred_element_type=jnp.float32)
        d["o"][cur["rows"], st["vsl"]] = o.astype(d["o"].dtype)
        upd = lax.dot_general(cur["k_pair"], v, (((0,), (0,)), ((), ())),
                              preferred_element_type=jnp.float32)
        c0 = 2 * cur["p"]
        decay = st["decay_t"][:, c0:c0 + 1] * st["decay_t"][:, c0 + 1:c0 + 2]
        d["s"][st["h"]] = decay * state + upd
    after_rounds()


def _out_body(x_ref, of_ref, ob_ref, sz_ref, yc_ref, gn_ref, wo_ref, fg_ref, out_ref):
    acc = jnp.dot(yc_ref[...], wo_ref[GLA_V_W:, :], preferred_element_type=jnp.float32)
    for h in range(GLA_HEADS):
        sl = slice(h * GLA_DV, (h + 1) * GLA_DV)
        o_h = of_ref[:, sl].astype(jnp.float32) + ob_ref[:, sl].astype(jnp.float32)
        ms = jnp.mean(o_h * o_h, axis=-1, keepdims=True)
        y_h = o_h * lax.rsqrt(ms + EPS) * gn_ref[...] * sz_ref[:, sl].astype(jnp.float32)
        acc = acc + jnp.dot(y_h.astype(jnp.bfloat16), wo_ref[sl, :],
                            preferred_element_type=jnp.float32)
    xo = x_ref[...] + acc
    ms = jnp.mean(xo * xo, axis=-1, keepdims=True)
    out_ref[...] = xo * lax.rsqrt(ms + EPS) * fg_ref[...]


def _mix_kernel(qf_ref, kf_ref, vf_ref, lrf_ref, qb_ref, kb_ref, vb_ref, lrb_ref,
                lrf_next_ref, lrb_next_ref, wgf_ref, wgb_ref, trif_ref, trib_ref,
                x_ref, sz_ref, yc_ref, gn_ref, wo_ref, fg_ref, out_ref, s_ref, o_ref, b_ref,
                gs_ref):
    s = pl.program_id(1)
    nblk = pl.num_programs(1) // 2
    slot = s % 2

    @pl.when((pl.program_id(0) == 0) & (s == 0))
    def _():
        _gate_terms_into(lrf_ref, wgf_ref, gs_ref.at[0])
        _gate_terms_into(lrb_ref, wgb_ref, gs_ref.at[1])
        _cumsum_into(gs_ref.at[0], trif_ref, b_ref.at[0, 0])
        _cumsum_into(gs_ref.at[1], trib_ref, b_ref.at[0, 1])

    @pl.when(s < nblk)
    def _():
        def gates_next():
            _gate_terms_into(lrf_next_ref, wgf_ref, gs_ref.at[0])
            _gate_terms_into(lrb_next_ref, wgb_ref, gs_ref.at[1])

        def sums_next():
            _cumsum_into(gs_ref.at[0], trif_ref, b_ref.at[1 - slot, 0])
            _cumsum_into(gs_ref.at[1], trib_ref, b_ref.at[1 - slot, 1])

        _gla_body(qf_ref, kf_ref, vf_ref, b_ref.at[slot, 0], qb_ref, kb_ref, vb_ref,
                  b_ref.at[slot, 1], o_ref.at[0, s], o_ref.at[1, nblk - 1 - s], s_ref, s == 0,
                  gates_next, sums_next)

    @pl.when(s >= nblk)
    def _():
        _out_body(x_ref, o_ref.at[0, s - nblk], o_ref.at[1, s - nblk], sz_ref, yc_ref,
                  gn_ref, wo_ref, fg_ref, out_ref)


def _mix(q, k, v, lr, wg_f, wg_b, x2, sz, yc, gn, wo, fg, *, batch, seq):
    ts = TILE
    nblk = seq // ts
    assert nblk % 2 == 0
    gla_s = lambda s: jnp.minimum(s, nblk - 1)

    def following(b, s):
        nxt = jnp.minimum(b * nblk + gla_s(s) + 1, batch * nblk - 1)
        return nxt // nblk, nxt % nblk

    def fwd_next(b, s):
        b1, s1 = following(b, s)
        return (b1 * nblk + s1, 0)

    def bwd_next(b, s):
        b1, s1 = following(b, s)
        return (b1 * nblk + (nblk - 1 - s1), 0)

    fwd = lambda b, s: (b * nblk + gla_s(s), 0)
    bwd = lambda b, s: (b * nblk + (nblk - 1 - gla_s(s)), 0)
    tile = lambda b, s: (b * nblk + jnp.maximum(s - nblk, 0), 0)
    rows = lambda idx: [pl.BlockSpec((ts, w), idx, pipeline_mode=pl.Buffered(3))
                        for w in (GLA_QK_W, GLA_QK_W, GLA_V_W, LR_W)]
    streamed_specs = (rows(fwd) + rows(bwd)
                      + [pl.BlockSpec((ts, LR_W), fwd_next), pl.BlockSpec((ts, LR_W), bwd_next)]
                      + [pl.BlockSpec((ts, D_MODEL), tile), pl.BlockSpec((ts, GLA_V_W), tile),
                         pl.BlockSpec((ts, CONV_W), tile)])
    out_spec = pl.BlockSpec((ts, D_MODEL), tile)
    streamed = (q, k, v, lr, q, k, v, lr, lr, lr, x2, sz, yc)
    resident = (wg_f, wg_b, _pair_tri(False), _pair_tri(True), gn, wo, fg)

    def call_body(*refs):
        ins = refs[:len(streamed)]
        wgf_ref, wgb_ref, trif_ref, trib_ref, gn_ref, wo_ref, fg_ref = refs[len(streamed):-5]
        out_hbm, s_ref, o_ref, b_ref, gs_ref = refs[-5:]

        def step(*blocks):
            (qf_ref, kf_ref, vf_ref, lrf_ref, qb_ref, kb_ref, vb_ref, lrb_ref,
             lrf_next_ref, lrb_next_ref, x_ref, sz_ref, yc_ref, out_ref) = blocks
            _mix_kernel(qf_ref, kf_ref, vf_ref, lrf_ref, qb_ref, kb_ref, vb_ref, lrb_ref,
                        lrf_next_ref, lrb_next_ref, wgf_ref, wgb_ref, trif_ref, trib_ref,
                        x_ref, sz_ref, yc_ref, gn_ref, wo_ref, fg_ref, out_ref,
                        s_ref, o_ref, b_ref, gs_ref)

        pltpu.emit_pipeline(step, grid=(batch, 2 * nblk), in_specs=streamed_specs,
                            out_specs=[out_spec])(*ins, out_hbm)

    return pl.pallas_call(
        call_body,
        in_specs=([pl.BlockSpec(memory_space=pl.ANY)] * len(streamed)
                  + [pl.BlockSpec(memory_space=pltpu.VMEM)] * len(resident)),
        out_specs=pl.BlockSpec(memory_space=pl.ANY),
        out_shape=jax.ShapeDtypeStruct((batch * seq, D_MODEL), jnp.float32),
        scratch_shapes=[pltpu.VMEM((2, GLA_HEADS, GLA_DK, GLA_DV), jnp.float32),
                        pltpu.VMEM((2, nblk, ts, GLA_V_W), jnp.bfloat16),
                        pltpu.VMEM((2, 2, ts, GLA_QK_W), jnp.float32),
                        pltpu.VMEM((2, 2 * ts, GLA_QK_W), jnp.bfloat16)],
        compiler_params=pltpu.CompilerParams(vmem_limit_bytes=VMEM_LIMIT),
        name="mix",
    )(*streamed, *resident)


def _regroup_kernel(a_ref, b_ref, o_ref):
    j = pl.program_id(0)
    n_before = _OFF_BG // REGROUP_TILE
    n_main = _OFF_LR // REGROUP_TILE
    lr = 2 * GATE_RANK

    def put(rows):
        o_ref[...] = jnp.transpose(rows).astype(o_ref.dtype)

    @pl.when(j < n_before)
    def _():
        put(a_ref[...])

    @pl.when((j >= n_before) & (j < n_main))
    def _():
        put(jnp.concatenate([a_ref[lr:, :], b_ref[...]], axis=0))

    @pl.when(j == n_main)
    def _():
        put(jnp.concatenate([a_ref[:lr, :], jnp.zeros((REGROUP_TILE - lr, a_ref.shape[1]),
                                                      a_ref.dtype)], axis=0))


def _regroup_w_in(w):
    wt = jnp.swapaxes(w, 1, 2)
    _, n, k = wt.shape
    lr = 2 * GATE_RANK
    assert lr == REGROUP_HALO and _OFF_BG % REGROUP_TILE == 0 and _OFF_LR % REGROUP_TILE == 0
    n_main = _OFF_LR // REGROUP_TILE
    sub = REGROUP_TILE // REGROUP_HALO
    first = lambda j: jnp.where(j == n_main, _OFF_BG // REGROUP_TILE, j)
    return pl.pallas_call(
        _regroup_kernel,
        grid=(W_ALL // REGROUP_TILE,),
        in_specs=[pl.BlockSpec((None, REGROUP_TILE, k), lambda j: (0, first(j), 0)),
                  pl.BlockSpec((None, REGROUP_HALO, k),
                               lambda j: (0, jnp.minimum((j + 1) * sub, n // REGROUP_HALO - 1), 0))],
        out_specs=pl.BlockSpec((k, REGROUP_TILE), lambda j: (0, j)),
        out_shape=jax.ShapeDtypeStruct((k, W_ALL), jnp.bfloat16),
        compiler_params=pltpu.CompilerParams(dimension_semantics=("arbitrary",)),
        name="regroup",
    )(wt, wt)


def _gate_weight(w_gk, b_gk, first_row):
    bias = b_gk * LOG2_E
    bias_hi = bias.astype(jnp.bfloat16)
    bias_lo = (bias - bias_hi.astype(jnp.float32)).astype(jnp.bfloat16)
    full = jnp.zeros((LR_W, GLA_QK_W), jnp.bfloat16)
    full = lax.dynamic_update_slice(full, (w_gk * LOG2_E).astype(jnp.bfloat16), (first_row, 0))
    return lax.dynamic_update_slice(full, jnp.stack([bias_hi, bias_lo]), (ONES_COL, 0))


def kernel(x, norm_g, w_in, w_gk_f, b_gk_f, w_gk_b, b_gk_b, gla_norm_g, conv_w, conv_b, w_out, final_g):
    batch, seq, d = x.shape
    depth = w_in.shape[0]
    assert depth == 1 and d == D_MODEL
    assert seq % TILE == 0
    x2 = x.reshape(batch * seq, d)
    q, k, v, sz, yc, lr = _inproj(x2, norm_g[0][None, :], _regroup_w_in(w_in),
                                  conv_w[0], conv_b[0][None, :], seq=seq)
    out = _mix(q, k, v, lr, _gate_weight(w_gk_f[0], b_gk_f[0], 0),
               _gate_weight(w_gk_b[0], b_gk_b[0], GATE_RANK), x2, sz, yc,
               gla_norm_g[0][None, :], w_out[0].astype(jnp.bfloat16), final_g[None, :],
               batch=batch, seq=seq)
    return out.reshape(batch, seq, d)
```

```python
import functools

import jax
import jax.numpy as jnp
import numpy as np
from jax import lax
from jax.experimental import pallas as pl
from jax.experimental.pallas import tpu as pltpu

D_MODEL = 1024
GLA_HEADS = 4
GLA_DK = 128
GLA_DV = 256
GLA_QK_W = GLA_HEADS * GLA_DK
GLA_V_W = GLA_HEADS * GLA_DV
GATE_RANK = 16
GATE_NORM = 16.0
CHUNK = 64
CONV_W = 1024
EPS = 1e-6
LOG2_E = 1.4426950408889634

LANES = 128
HALO = 8
LR_W = LANES
ONES_COL = 2 * GATE_RANK

_OFF_Q = 0
_OFF_K = _OFF_Q + GLA_QK_W
_OFF_V = _OFF_K + GLA_QK_W
_OFF_ZA = _OFF_V + GLA_V_W
_OFF_BG = _OFF_ZA + GLA_V_W
_OFF_CG = _OFF_BG + CONV_W
_OFF_HC = _OFF_CG + CONV_W
_OFF_ZC = _OFF_HC + CONV_W
_OFF_LR = _OFF_ZC + CONV_W
REGROUP_TILE = 4 * LANES
REGROUP_HALO = 2 * GATE_RANK
W_ALL = _OFF_LR + REGROUP_TILE

TILE = 512
VMEM_LIMIT = 58 * 1024 * 1024
SCORE_LEAD = 12


def _silu(z):
    return z * (1.0 / (1.0 + jnp.exp(-z)))


def _inproj_kernel(x_ref, xp_ref, xn_ref, g_ref, w_ref, cw_ref, cb_ref,
                   q_ref, k_ref, v_ref, sz_ref, yc_ref, lr_ref, *, tiles_per_seq):
    tm = x_ref.shape[0]
    pos = pl.program_id(0) % tiles_per_seq

    def normed(x):
        ms = jnp.mean(x * x, axis=-1, keepdims=True)
        return x * lax.rsqrt(ms + EPS) * g_ref[...]

    h = normed(x_ref[...]).astype(jnp.bfloat16)
    halo = jnp.concatenate([normed(xp_ref[...]), normed(xn_ref[...])], axis=0).astype(jnp.bfloat16)
    h_ext = jnp.concatenate([h, halo], axis=0)

    def proj(lo, hi, lhs=h):
        return jnp.dot(lhs, w_ref[:, lo:hi], preferred_element_type=jnp.float32)

    u_ext = proj(_OFF_CG, _OFF_HC, h_ext) * proj(_OFF_HC, _OFF_ZC, h_ext)
    u = u_ext[:tm]
    prev_row = jnp.where(pos == 0, 0.0, u_ext[tm + HALO - 1:tm + HALO])
    next_row = jnp.where(pos == tiles_per_seq - 1, 0.0, u_ext[tm + HALO:tm + HALO + 1])
    t_i = lax.broadcasted_iota(jnp.int32, (tm, 1), 0)
    u_prev = jnp.where(t_i == 0, prev_row, pltpu.roll(u, 1, 0))
    u_next = jnp.where(t_i == tm - 1, next_row, pltpu.roll(u, tm - 1, 0))
    conv = cw_ref[0:1, :] * u_prev + cw_ref[1:2, :] * u + cw_ref[2:3, :] * u_next + cb_ref[...]
    yc_ref[...] = (proj(_OFF_BG, _OFF_CG) * conv * _silu(proj(_OFF_ZC, _OFF_LR))).astype(yc_ref.dtype)

    sz_ref[...] = _silu(proj(_OFF_ZA, _OFF_BG)).astype(sz_ref.dtype)
    q_ref[...] = (proj(_OFF_Q, _OFF_K) * (GLA_DK ** -0.5)).astype(q_ref.dtype)
    k_ref[...] = proj(_OFF_K, _OFF_V).astype(k_ref.dtype)
    v_ref[...] = proj(_OFF_V, _OFF_ZA).astype(v_ref.dtype)
    col = lax.broadcasted_iota(jnp.int32, (1, LR_W), 1)
    ones = jnp.where((col == ONES_COL) | (col == ONES_COL + 1), 1.0, 0.0)
    lr_ref[...] = proj(_OFF_LR, _OFF_LR + LR_W) + ones


def _inproj(x2, norm_g, w_all, conv_w, conv_b, *, seq):
    m = x2.shape[0]
    tm = TILE
    sub = tm // HALO
    nsub = m // HALO
    row = lambda w: pl.BlockSpec((tm, w), lambda i: (i, 0))
    const = lambda shp: pl.BlockSpec(shp, lambda i: (0, 0))
    halo_prev = pl.BlockSpec((HALO, D_MODEL), lambda i: (jnp.maximum(i * sub - 1, 0), 0))
    halo_next = pl.BlockSpec((HALO, D_MODEL), lambda i: (jnp.minimum((i + 1) * sub, nsub - 1), 0))
    bf = jnp.bfloat16
    return pl.pallas_call(
        functools.partial(_inproj_kernel, tiles_per_seq=seq // tm),
        grid=(m // tm,),
        in_specs=[
            row(D_MODEL), halo_prev, halo_next,
            const((1, D_MODEL)),
            pl.BlockSpec((D_MODEL, W_ALL), lambda i: (0, 0), pipeline_mode=pl.Buffered(1)),
            const((3, CONV_W)), const((1, CONV_W)),
        ],
        out_specs=[row(GLA_QK_W), row(GLA_QK_W), row(GLA_V_W), row(GLA_V_W),
                   row(CONV_W), row(LR_W)],
        out_shape=[
            jax.ShapeDtypeStruct((m, GLA_QK_W), bf),
            jax.ShapeDtypeStruct((m, GLA_QK_W), bf),
            jax.ShapeDtypeStruct((m, GLA_V_W), bf),
            jax.ShapeDtypeStruct((m, GLA_V_W), bf),
            jax.ShapeDtypeStruct((m, CONV_W), bf),
            jax.ShapeDtypeStruct((m, LR_W), jnp.float32),
        ],
        compiler_params=pltpu.CompilerParams(
            dimension_semantics=("arbitrary",), vmem_limit_bytes=VMEM_LIMIT),
        name="inproj",
    )(x2, x2, x2, norm_g, w_all, conv_w, conv_b)


def _pair_tri(rev):
    t = np.arange(CHUNK)
    tri = (t[None, :] >= t[:, None]) if rev else (t[None, :] <= t[:, None])
    zero = np.zeros_like(tri)
    keep = np.block([[tri, tri, zero, zero], [zero, zero, tri, tri]])
    return jnp.asarray(keep / GATE_NORM, dtype=jnp.bfloat16)


def _gate_terms_into(lr_ref, wg_ref, gs_ref):
    ts = lr_ref.shape[0]
    x = jnp.dot(lr_ref[...].astype(jnp.bfloat16), wg_ref[...], preferred_element_type=jnp.float32)
    g = jnp.minimum(x, 0.0) - jnp.log2(1.0 + jnp.exp2(-jnp.abs(x)))
    g_top = lax.bitcast_convert_type(
        lax.bitcast_convert_type(g, jnp.uint32) & jnp.uint32(0xFFFF0000), jnp.float32)
    g_hi = g_top.astype(gs_ref.dtype)
    g_lo = (g - g_top).astype(gs_ref.dtype)
    for c in range(ts // CHUNK):
        rows = slice(c * CHUNK, (c + 1) * CHUNK)
        gs_ref[2 * c * CHUNK:(2 * c + 1) * CHUNK, :] = g_hi[rows]
        gs_ref[(2 * c + 1) * CHUNK:(2 * c + 2) * CHUNK, :] = g_lo[rows]


def _cumsum_into(gs_ref, tri_ref, b_ref):
    tri = tri_ref[...]
    for p in range(b_ref.shape[0] // (2 * CHUNK)):
        b_ref[2 * p * CHUNK:(2 * p + 2) * CHUNK, :] = jnp.dot(
            tri, gs_ref[4 * p * CHUNK:(4 * p + 4) * CHUNK, :], preferred_element_type=jnp.float32)


def _gla_body(qf_ref, kf_ref, vf_ref, bf_ref, qb_ref, kb_ref, vb_ref, bb_ref,
              of_ref, ob_ref, s_ref, first_block, mid_round, after_rounds, accumulate):
    ts = qf_ref.shape[0]
    nc = ts // CHUNK
    npair = nc // 2
    pair = 2 * CHUNK

    @pl.when(first_block)
    def _():
        s_ref[...] = jnp.zeros_like(s_ref)

    ri = lax.broadcasted_iota(jnp.int32, (CHUNK, pair), 0)
    li = lax.broadcasted_iota(jnp.int32, (CHUNK, pair), 1)
    lo_half = li < CHUNK
    dirs = (
        dict(q=qf_ref, k=kf_ref, v=vf_ref, o=of_ref, s=s_ref.at[0], a=0, b_=1,
             mask_a=lo_half & (li <= ri), mask_b=lo_half | (li - CHUNK <= ri),
             ref_row=CHUNK // 2, last_row=CHUNK - 1, order=tuple(range(npair)),
             b=bf_ref),
        dict(q=qb_ref, k=kb_ref, v=vb_ref, o=ob_ref, s=s_ref.at[1], a=1, b_=0,
             mask_a=(~lo_half) & (li - CHUNK > ri), mask_b=(~lo_half) | (li > ri),
             ref_row=CHUNK - 1 - CHUNK // 2, last_row=0, order=tuple(range(npair - 1, -1, -1)),
             b=bb_ref),
    )

    streams = []
    for d in dirs:
        b_last_rows = jnp.concatenate(
            [d["b"][c * CHUNK + d["last_row"]:c * CHUNK + d["last_row"] + 1, :] for c in range(nc)],
            axis=0)
        for h in range(GLA_HEADS):
            ksl = slice(h * GLA_DK, (h + 1) * GLA_DK)
            streams.append(dict(
                d=d, h=h, ksl=ksl, vsl=slice(h * GLA_DV, (h + 1) * GLA_DV),
                decay_t=jnp.transpose(jnp.exp2(b_last_rows[:, ksl])),
            ))

    bf = jnp.bfloat16
    zeros_k = jnp.zeros((CHUNK, GLA_DK), bf)

    def intra(st, step):
        d = st["d"]
        p = d["order"][step]
        rows = slice(p * pair, (p + 1) * pair)
        ia, ib = d["a"], d["b_"]
        b = d["b"][rows, st["ksl"]].reshape(2, CHUNK, GLA_DK)
        b_mid = b[:, d["ref_row"]:d["ref_row"] + 1, :]
        tot = b[:, d["last_row"]:d["last_row"] + 1, :]
        qe = (d["q"][rows, st["ksl"]].astype(jnp.float32).reshape(2, CHUNK, GLA_DK)
              * jnp.exp2(b - b_mid))
        ke = (d["k"][rows, st["ksl"]].astype(jnp.float32).reshape(2, CHUNK, GLA_DK)
              * jnp.exp2(b_mid - b))
        f_q = jnp.exp2(b_mid)
        f_k = jnp.exp2(tot - b_mid)
        q_in_a = (qe[ia] * f_q[ia]).astype(bf)
        q_in_b = (qe[ib] * f_q[ib]).astype(bf)
        q_in_b2 = (qe[ib] * (f_q[ib] * jnp.exp2(tot[ia]))).astype(bf)
        k_out_a = (ke[ia] * f_k[ia]).astype(bf)
        k_out_a2 = (ke[ia] * (f_k[ia] * jnp.exp2(tot[ib]))).astype(bf)
        k_out_b = (ke[ib] * f_k[ib]).astype(bf)
        ke_a = ke[ia].astype(bf)
        ke_b = ke[ib].astype(bf)
        nt = (((1,), (1,)), ((), ()))
        half = lambda x, slot: jnp.concatenate([x, zeros_k] if slot == 0 else [zeros_k, x], axis=0)
        att_a = lax.dot_general(qe[ia].astype(bf), half(ke_a, ia), nt,
                                preferred_element_type=jnp.float32)
        rhs_b = [None, None]
        rhs_b[ia] = jnp.concatenate([k_out_a, zeros_k], axis=1)
        rhs_b[ib] = jnp.concatenate([zeros_k, ke_b], axis=1)
        att_b = lax.dot_general(jnp.concatenate([q_in_b, qe[ib].astype(bf)], axis=1),
                                jnp.concatenate(rhs_b, axis=0), nt,
                                preferred_element_type=jnp.float32)
        q_rows = [None, None]
        q_rows[ia], q_rows[ib] = q_in_a, q_in_b2
        k_rows = [None, None]
        k_rows[ia], k_rows[ib] = k_out_a2, k_out_b
        return dict(p=p, rows=rows, att=(att_a, att_b), q_rows=q_rows,
                    k_pair=jnp.concatenate(k_rows, axis=0))

    items = [(step, st) for step in range(npair) for st in streams]
    ahead = {(step, id(st)): intra(st, step) for step, st in items[:SCORE_LEAD]}
    for i, (step, st) in enumerate(items):
        if step == npair // 2 and st is streams[0]:
            mid_round()
        d = st["d"]
        cur = ahead.pop((step, id(st)))
        if i + SCORE_LEAD < len(items):
            nstep, nst = items[i + SCORE_LEAD]
            ahead[(nstep, id(nst))] = intra(nst, nstep)
        ia, ib = d["a"], d["b_"]
        v = d["v"][cur["rows"], st["vsl"]]
        state = d["s"][st["h"]]
        att = [None, None]
        att[ia] = jnp.where(d["mask_a"], cur["att"][0], 0.0).astype(bf)
        att[ib] = jnp.where(d["mask_b"], cur["att"][1], 0.0).astype(bf)
        lhs = jnp.concatenate([jnp.concatenate([cur["q_rows"][0], att[0]], axis=1),
                               jnp.concatenate([cur["q_rows"][1], att[1]], axis=1)], axis=0)
        rhs = jnp.concatenate([state.astype(bf), v], axis=0)
        o = jnp.dot(lhs, rhs, preferred_element_type=jnp.float32)
        if accumulate:
            o = o + d["o"][cur["rows"], st["vsl"]].astype(jnp.float32)
        d["o"][cur["rows"], st["vsl"]] = o.astype(d["o"].dtype)
        upd = lax.dot_general(cur["k_pair"], v, (((0,), (0,)), ((), ())),
                              preferred_element_type=jnp.float32)
        c0 = 2 * cur["p"]
        decay = st["decay_t"][:, c0:c0 + 1] * st["decay_t"][:, c0 + 1:c0 + 2]
        d["s"][st["h"]] = decay * state + upd
    after_rounds()


def _out_body(x_ref, o_ref, sz_ref, yc_ref, gn_ref, wo_ref, fg_ref, out_ref):
    acc = jnp.dot(yc_ref[...], wo_ref[GLA_V_W:, :], preferred_element_type=jnp.float32)
    for h in range(GLA_HEADS):
        sl = slice(h * GLA_DV, (h + 1) * GLA_DV)
        o_h = o_ref[:, sl].astype(jnp.float32)
        ms = jnp.mean(o_h * o_h, axis=-1, keepdims=True)
        y_h = o_h * lax.rsqrt(ms + EPS) * gn_ref[...] * sz_ref[:, sl].astype(jnp.float32)
        acc = acc + jnp.dot(y_h.astype(jnp.bfloat16), wo_ref[sl, :],
                            preferred_element_type=jnp.float32)
    xo = x_ref[...] + acc
    ms = jnp.mean(xo * xo, axis=-1, keepdims=True)
    out_ref[...] = xo * lax.rsqrt(ms + EPS) * fg_ref[...]


def _mix_kernel(qf_ref, kf_ref, vf_ref, lrf_ref, qb_ref, kb_ref, vb_ref, lrb_ref,
                lrf_next_ref, lrb_next_ref, wgf_ref, wgb_ref, trif_ref, trib_ref,
                x_ref, sz_ref, yc_ref, gn_ref, wo_ref, fg_ref, out_ref, s_ref, o_ref, b_ref,
                gs_ref):
    b = pl.program_id(0)
    s = pl.program_id(1)
    nseq = pl.num_programs(0) - 1
    nblk = pl.num_programs(1)
    slot = s % 2
    cur = b % 2

    @pl.when((b == 0) & (s == 0))
    def _():
        _gate_terms_into(lrf_ref, wgf_ref, gs_ref.at[0])
        _gate_terms_into(lrb_ref, wgb_ref, gs_ref.at[1])
        _cumsum_into(gs_ref.at[0], trif_ref, b_ref.at[0, 0])
        _cumsum_into(gs_ref.at[1], trib_ref, b_ref.at[0, 1])

    @pl.when(b >= 1)
    def _():
        _out_body(x_ref, o_ref.at[1 - cur, s], sz_ref, yc_ref, gn_ref, wo_ref, fg_ref, out_ref)

    def gla(accumulate):
        def gates_next():
            _gate_terms_into(lrf_next_ref, wgf_ref, gs_ref.at[0])
            _gate_terms_into(lrb_next_ref, wgb_ref, gs_ref.at[1])

        def sums_next():
            _cumsum_into(gs_ref.at[0], trif_ref, b_ref.at[1 - slot, 0])
            _cumsum_into(gs_ref.at[1], trib_ref, b_ref.at[1 - slot, 1])

        _gla_body(qf_ref, kf_ref, vf_ref, b_ref.at[slot, 0], qb_ref, kb_ref, vb_ref,
                  b_ref.at[slot, 1], o_ref.at[cur, s], o_ref.at[cur, nblk - 1 - s], s_ref, s == 0,
                  gates_next, sums_next, accumulate)

    @pl.when((b < nseq) & (s < nblk // 2))
    def _():
        gla(False)

    @pl.when((b < nseq) & (s >= nblk // 2))
    def _():
        gla(True)


def _mix(q, k, v, lr, wg_f, wg_b, x2, sz, yc, gn, wo, fg, *, batch, seq):
    ts = TILE
    nblk = seq // ts
    assert nblk % 2 == 0
    gla_b = lambda b: jnp.minimum(b, batch - 1)
    gla_s = lambda b, s: jnp.where(b < batch, s, nblk - 1)

    def following(b, s):
        nxt = jnp.minimum(gla_b(b) * nblk + gla_s(b, s) + 1, batch * nblk - 1)
        return nxt // nblk, nxt % nblk

    def fwd_next(b, s):
        b1, s1 = following(b, s)
        return (b1 * nblk + s1, 0)

    def bwd_next(b, s):
        b1, s1 = following(b, s)
        return (b1 * nblk + (nblk - 1 - s1), 0)

    fwd = lambda b, s: (gla_b(b) * nblk + gla_s(b, s), 0)
    bwd = lambda b, s: (gla_b(b) * nblk + (nblk - 1 - gla_s(b, s)), 0)
    tile = lambda b, s: (jnp.maximum(b - 1, 0) * nblk + jnp.where(b >= 1, s, 0), 0)
    rows = lambda idx: [pl.BlockSpec((ts, w), idx) for w in (GLA_QK_W, GLA_QK_W, GLA_V_W, LR_W)]
    const = lambda shp, **kw: pl.BlockSpec(shp, lambda b, s: (0, 0), **kw)
    return pl.pallas_call(
        _mix_kernel,
        grid=(batch + 1, nblk),
        in_specs=(rows(fwd) + rows(bwd)
                  + [pl.BlockSpec((ts, LR_W), fwd_next), pl.BlockSpec((ts, LR_W), bwd_next)]
                  + [const((LR_W, GLA_QK_W))] * 2
                  + [const((2 * CHUNK, 4 * CHUNK))] * 2
                  + [pl.BlockSpec((ts, D_MODEL), tile), pl.BlockSpec((ts, GLA_V_W), tile),
                     pl.BlockSpec((ts, CONV_W), tile), const((1, GLA_DV)),
                     const((GLA_V_W + CONV_W, D_MODEL), pipeline_mode=pl.Buffered(1)),
                     const((1, D_MODEL))]),
        out_specs=pl.BlockSpec((ts, D_MODEL), tile),
        out_shape=jax.ShapeDtypeStruct((batch * seq, D_MODEL), jnp.float32),
        scratch_shapes=[pltpu.VMEM((2, GLA_HEADS, GLA_DK, GLA_DV), jnp.float32),
                        pltpu.VMEM((2, nblk, ts, GLA_V_W), jnp.bfloat16),
                        pltpu.VMEM((2, 2, ts, GLA_QK_W), jnp.float32),
                        pltpu.VMEM((2, 2 * ts, GLA_QK_W), jnp.bfloat16)],
        compiler_params=pltpu.CompilerParams(
            dimension_semantics=("arbitrary", "arbitrary"), vmem_limit_bytes=VMEM_LIMIT),
        name="mix",
    )(q, k, v, lr, q, k, v, lr, lr, lr, wg_f, wg_b, _pair_tri(False), _pair_tri(True),
      x2, sz, yc, gn, wo, fg)


def _regroup_kernel(a_ref, b_ref, o_ref):
    j = pl.program_id(0)
    n_before = _OFF_BG // REGROUP_TILE
    n_main = _OFF_LR // REGROUP_TILE
    lr = 2 * GATE_RANK

    def put(rows):
        o_ref[...] = jnp.transpose(rows).astype(o_ref.dtype)

    @pl.when(j < n_before)
    def _():
        put(a_ref[...])

    @pl.when((j >= n_before) & (j < n_main))
    def _():
        put(jnp.concatenate([a_ref[lr:, :], b_ref[...]], axis=0))

    @pl.when(j == n_main)
    def _():
        put(jnp.concatenate([a_ref[:lr, :], jnp.zeros((REGROUP_TILE - lr, a_ref.shape[1]),
                                                      a_ref.dtype)], axis=0))


def _regroup_w_in(w):
    wt = jnp.swapaxes(w, 1, 2)
    _, n, k = wt.shape
    lr = 2 * GATE_RANK
    assert lr == REGROUP_HALO and _OFF_BG % REGROUP_TILE == 0 and _OFF_LR % REGROUP_TILE == 0
    n_main = _OFF_LR // REGROUP_TILE
    sub = REGROUP_TILE // REGROUP_HALO
    first = lambda j: jnp.where(j == n_main, _OFF_BG // REGROUP_TILE, j)
    return pl.pallas_call(
        _regroup_kernel,
        grid=(W_ALL // REGROUP_TILE,),
        in_specs=[pl.BlockSpec((None, REGROUP_TILE, k), lambda j: (0, first(j), 0)),
                  pl.BlockSpec((None, REGROUP_HALO, k),
                               lambda j: (0, jnp.minimum((j + 1) * sub, n // REGROUP_HALO - 1), 0))],
        out_specs=pl.BlockSpec((k, REGROUP_TILE), lambda j: (0, j)),
        out_shape=jax.ShapeDtypeStruct((k, W_ALL), jnp.bfloat16),
        compiler_params=pltpu.CompilerParams(dimension_semantics=("arbitrary",)),
        name="regroup",
    )(wt, wt)


def _gate_weight(w_gk, b_gk, first_row):
    bias = b_gk * LOG2_E
    bias_hi = bias.astype(jnp.bfloat16)
    bias_lo = (bias - bias_hi.astype(jnp.float32)).astype(jnp.bfloat16)
    full = jnp.zeros((LR_W, GLA_QK_W), jnp.bfloat16)
    full = lax.dynamic_update_slice(full, (w_gk * LOG2_E).astype(jnp.bfloat16), (first_row, 0))
    return lax.dynamic_update_slice(full, jnp.stack([bias_hi, bias_lo]), (ONES_COL, 0))


def kernel(x, norm_g, w_in, w_gk_f, b_gk_f, w_gk_b, b_gk_b, gla_norm_g, conv_w, conv_b, w_out, final_g):
    batch, seq, d = x.shape
    depth = w_in.shape[0]
    assert depth == 1 and d == D_MODEL
    assert seq % TILE == 0
    x2 = x.reshape(batch * seq, d)
    q, k, v, sz, yc, lr = _inproj(x2, norm_g[0][None, :], _regroup_w_in(w_in),
                                  conv_w[0], conv_b[0][None, :], seq=seq)
    out = _mix(q, k, v, lr, _gate_weight(w_gk_f[0], b_gk_f[0], 0),
               _gate_weight(w_gk_b[0], b_gk_b[0], GATE_RANK), x2, sz, yc,
               gla_norm_g[0][None, :], w_out[0].astype(jnp.bfloat16), final_g[None, :],
               batch=batch, seq=seq)
    return out.reshape(batch, seq, d)
```

```python
import functools

import jax
import jax.numpy as jnp
import numpy as np
from jax import lax
from jax.experimental import pallas as pl
from jax.experimental.pallas import tpu as pltpu

D_MODEL = 1024
GLA_HEADS = 4
GLA_DK = 128
GLA_DV = 256
GLA_QK_W = GLA_HEADS * GLA_DK
GLA_V_W = GLA_HEADS * GLA_DV
GATE_RANK = 16
GATE_NORM = 16.0
CHUNK = 64
CONV_W = 1024
EPS = 1e-6
LOG2_E = 1.4426950408889634

LANES = 128
HALO = 8
LR_W = LANES
ONES_COL = 2 * GATE_RANK

_OFF_Q = 0
_OFF_K = _OFF_Q + GLA_QK_W
_OFF_V = _OFF_K + GLA_QK_W
_OFF_ZA = _OFF_V + GLA_V_W
_OFF_BG = _OFF_ZA + GLA_V_W
_OFF_CG = _OFF_BG + CONV_W
_OFF_HC = _OFF_CG + CONV_W
_OFF_ZC = _OFF_HC + CONV_W
_OFF_LR = _OFF_ZC + CONV_W
REGROUP_TILE = 4 * LANES
REGROUP_HALO = 2 * GATE_RANK
W_ALL = _OFF_LR + REGROUP_TILE

TILE = 512
VMEM_LIMIT = 58 * 1024 * 1024
SCORE_LEAD = 12


def _silu(z):
    return z * (1.0 / (1.0 + jnp.exp(-z)))


def _inproj_kernel(x_ref, xp_ref, xn_ref, g_ref, w_ref, cw_ref, cb_ref,
                   q_ref, k_ref, v_ref, sz_ref, yc_ref, lr_ref, *, tiles_per_seq):
    tm = x_ref.shape[0]
    pos = pl.program_id(0) % tiles_per_seq

    def normed(x):
        ms = jnp.mean(x * x, axis=-1, keepdims=True)
        return x * lax.rsqrt(ms + EPS) * g_ref[...]

    h = normed(x_ref[...]).astype(jnp.bfloat16)
    halo = jnp.concatenate([normed(xp_ref[...]), normed(xn_ref[...])], axis=0).astype(jnp.bfloat16)
    h_ext = jnp.concatenate([h, halo], axis=0)

    def proj(lo, hi, lhs=h):
        return jnp.dot(lhs, w_ref[:, lo:hi], preferred_element_type=jnp.float32)

    u_ext = proj(_OFF_CG, _OFF_HC, h_ext) * proj(_OFF_HC, _OFF_ZC, h_ext)
    u = u_ext[:tm]
    prev_row = jnp.where(pos == 0, 0.0, u_ext[tm + HALO - 1:tm + HALO])
    next_row = jnp.where(pos == tiles_per_seq - 1, 0.0, u_ext[tm + HALO:tm + HALO + 1])
    t_i = lax.broadcasted_iota(jnp.int32, (tm, 1), 0)
    u_prev = jnp.where(t_i == 0, prev_row, pltpu.roll(u, 1, 0))
    u_next = jnp.where(t_i == tm - 1, next_row, pltpu.roll(u, tm - 1, 0))
    conv = cw_ref[0:1, :] * u_prev + cw_ref[1:2, :] * u + cw_ref[2:3, :] * u_next + cb_ref[...]
    yc_ref[...] = (proj(_OFF_BG, _OFF_CG) * conv * _silu(proj(_OFF_ZC, _OFF_LR))).astype(yc_ref.dtype)

    sz_ref[...] = _silu(proj(_OFF_ZA, _OFF_BG)).astype(sz_ref.dtype)
    q_ref[...] = (proj(_OFF_Q, _OFF_K) * (GLA_DK ** -0.5)).astype(q_ref.dtype)
    k_ref[...] = proj(_OFF_K, _OFF_V).astype(k_ref.dtype)
    v_ref[...] = proj(_OFF_V, _OFF_ZA).astype(v_ref.dtype)
    col = lax.broadcasted_iota(jnp.int32, (1, LR_W), 1)
    ones = jnp.where((col == ONES_COL) | (col == ONES_COL + 1), 1.0, 0.0)
    lr_ref[...] = proj(_OFF_LR, _OFF_LR + LR_W) + ones


def _inproj(x2, norm_g, w_all, conv_w, conv_b, *, seq):
    m = x2.shape[0]
    tm = TILE
    sub = tm // HALO
    nsub = m // HALO
    row = lambda w: pl.BlockSpec((tm, w), lambda i: (i, 0))
    const = lambda shp: pl.BlockSpec(shp, lambda i: (0, 0))
    halo_prev = pl.BlockSpec((HALO, D_MODEL), lambda i: (jnp.maximum(i * sub - 1, 0), 0))
    halo_next = pl.BlockSpec((HALO, D_MODEL), lambda i: (jnp.minimum((i + 1) * sub, nsub - 1), 0))
    bf = jnp.bfloat16
    return pl.pallas_call(
        functools.partial(_inproj_kernel, tiles_per_seq=seq // tm),
        grid=(m // tm,),
        in_specs=[
            row(D_MODEL), halo_prev, halo_next,
            const((1, D_MODEL)),
            pl.BlockSpec((D_MODEL, W_ALL), lambda i: (0, 0), pipeline_mode=pl.Buffered(1)),
            const((3, CONV_W)), const((1, CONV_W)),
        ],
        out_specs=[row(GLA_QK_W), row(GLA_QK_W), row(GLA_V_W), row(GLA_V_W),
                   row(CONV_W), row(LR_W)],
        out_shape=[
            jax.ShapeDtypeStruct((m, GLA_QK_W), bf),
            jax.ShapeDtypeStruct((m, GLA_QK_W), bf),
            jax.ShapeDtypeStruct((m, GLA_V_W), bf),
            jax.ShapeDtypeStruct((m, GLA_V_W), bf),
            jax.ShapeDtypeStruct((m, CONV_W), bf),
            jax.ShapeDtypeStruct((m, LR_W), jnp.float32),
        ],
        compiler_params=pltpu.CompilerParams(
            dimension_semantics=("arbitrary",), vmem_limit_bytes=VMEM_LIMIT),
        name="inproj",
    )(x2, x2, x2, norm_g, w_all, conv_w, conv_b)


def _pair_tri(rev):
    t = np.arange(CHUNK)
    tri = (t[None, :] >= t[:, None]) if rev else (t[None, :] <= t[:, None])
    zero = np.zeros_like(tri)
    keep = np.block([[tri, tri, zero, zero], [zero, zero, tri, tri]])
    return jnp.asarray(keep / GATE_NORM, dtype=jnp.bfloat16)


def _gate_terms_into(lr_ref, wg_ref, gs_ref):
    ts = lr_ref.shape[0]
    x = jnp.dot(lr_ref[...].astype(jnp.bfloat16), wg_ref[...], preferred_element_type=jnp.float32)
    g = jnp.minimum(x, 0.0) - jnp.log2(1.0 + jnp.exp2(-jnp.abs(x)))
    g_top = lax.bitcast_convert_type(
        lax.bitcast_convert_type(g, jnp.uint32) & jnp.uint32(0xFFFF0000), jnp.float32)
    g_hi = g_top.astype(gs_ref.dtype)
    g_lo = (g - g_top).astype(gs_ref.dtype)
    for c in range(ts // CHUNK):
        rows = slice(c * CHUNK, (c + 1) * CHUNK)
        gs_ref[2 * c * CHUNK:(2 * c + 1) * CHUNK, :] = g_hi[rows]
        gs_ref[(2 * c + 1) * CHUNK:(2 * c + 2) * CHUNK, :] = g_lo[rows]


def _cumsum_into(gs_ref, tri_ref, b_ref):
    tri = tri_ref[...]
    for p in range(b_ref.shape[0] // (2 * CHUNK)):
        b_ref[2 * p * CHUNK:(2 * p + 2) * CHUNK, :] = jnp.dot(
            tri, gs_ref[4 * p * CHUNK:(4 * p + 4) * CHUNK, :], preferred_element_type=jnp.float32)


def _gla_body(qf_ref, kf_ref, vf_ref, bf_ref, qb_ref, kb_ref, vb_ref, bb_ref,
              of_ref, ob_ref, s_ref, first_block, mid_round, after_rounds, accumulate):
    ts = qf_ref.shape[0]
    nc = ts // CHUNK
    npair = nc // 2
    pair = 2 * CHUNK

    @pl.when(first_block)
    def _():
        s_ref[...] = jnp.zeros_like(s_ref)

    ri = lax.broadcasted_iota(jnp.int32, (CHUNK, pair), 0)
    li = lax.broadcasted_iota(jnp.int32, (CHUNK, pair), 1)
    lo_half = li < CHUNK
    dirs = (
        dict(q=qf_ref, k=kf_ref, v=vf_ref, o=of_ref, s=s_ref.at[0], a=0, b_=1,
             mask_a=lo_half & (li <= ri), mask_b=lo_half | (li - CHUNK <= ri),
             ref_row=CHUNK // 2, last_row=CHUNK - 1, order=tuple(range(npair)),
             b=bf_ref),
        dict(q=qb_ref, k=kb_ref, v=vb_ref, o=ob_ref, s=s_ref.at[1], a=1, b_=0,
             mask_a=(~lo_half) & (li - CHUNK > ri), mask_b=(~lo_half) | (li > ri),
             ref_row=CHUNK - 1 - CHUNK // 2, last_row=0, order=tuple(range(npair - 1, -1, -1)),
             b=bb_ref),
    )

    streams = []
    for d in dirs:
        b_last_rows = jnp.concatenate(
            [d["b"][c * CHUNK + d["last_row"]:c * CHUNK + d["last_row"] + 1, :] for c in range(nc)],
            axis=0)
        for h in range(GLA_HEADS):
            ksl = slice(h * GLA_DK, (h + 1) * GLA_DK)
            streams.append(dict(
                d=d, h=h, ksl=ksl, vsl=slice(h * GLA_DV, (h + 1) * GLA_DV),
                decay_t=jnp.transpose(jnp.exp2(b_last_rows[:, ksl])),
            ))

    bf = jnp.bfloat16
    zeros_k = jnp.zeros((CHUNK, GLA_DK), bf)

    def intra(st, step):
        d = st["d"]
        p = d["order"][step]
        rows = slice(p * pair, (p + 1) * pair)
        ia, ib = d["a"], d["b_"]
        b = d["b"][rows, st["ksl"]].reshape(2, CHUNK, GLA_DK)
        b_mid = b[:, d["ref_row"]:d["ref_row"] + 1, :]
        tot = b[:, d["last_row"]:d["last_row"] + 1, :]
        qe = (d["q"][rows, st["ksl"]].astype(jnp.float32).reshape(2, CHUNK, GLA_DK)
              * jnp.exp2(b - b_mid))
        ke = (d["k"][rows, st["ksl"]].astype(jnp.float32).reshape(2, CHUNK, GLA_DK)
              * jnp.exp2(b_mid - b))
        f_q = jnp.exp2(b_mid)
        f_k = jnp.exp2(tot - b_mid)
        q_in_a = (qe[ia] * f_q[ia]).astype(bf)
        q_in_b = (qe[ib] * f_q[ib]).astype(bf)
        q_in_b2 = (qe[ib] * (f_q[ib] * jnp.exp2(tot[ia]))).astype(bf)
        k_out_a = (ke[ia] * f_k[ia]).astype(bf)
        k_out_a2 = (ke[ia] * (f_k[ia] * jnp.exp2(tot[ib]))).astype(bf)
        k_out_b = (ke[ib] * f_k[ib]).astype(bf)
        ke_a = ke[ia].astype(bf)
        ke_b = ke[ib].astype(bf)
        nt = (((1,), (1,)), ((), ()))
        half = lambda x, slot: jnp.concatenate([x, zeros_k] if slot == 0 else [zeros_k, x], axis=0)
        att_a = lax.dot_general(qe[ia].astype(bf), half(ke_a, ia), nt,
                                preferred_element_type=jnp.float32)
        rhs_b = [None, None]
        rhs_b[ia] = jnp.concatenate([k_out_a, zeros_k], axis=1)
        rhs_b[ib] = jnp.concatenate([zeros_k, ke_b], axis=1)
        att_b = lax.dot_general(jnp.concatenate([q_in_b, qe[ib].astype(bf)], axis=1),
                                jnp.concatenate(rhs_b, axis=0), nt,
                                preferred_element_type=jnp.float32)
        q_rows = [None, None]
        q_rows[ia], q_rows[ib] = q_in_a, q_in_b2
        k_rows = [None, None]
        k_rows[ia], k_rows[ib] = k_out_a2, k_out_b
        return dict(p=p, rows=rows, att=(att_a, att_b), q_rows=q_rows,
                    k_pair=jnp.concatenate(k_rows, axis=0))

    items = [(step, st) for step in range(npair) for st in streams]
    ahead = {(step, id(st)): intra(st, step) for step, st in items[:SCORE_LEAD]}
    for i, (step, st) in enumerate(items):
        if step == npair // 2 and st is streams[0]:
            mid_round()
        d = st["d"]
        cur = ahead.pop((step, id(st)))
        if i + SCORE_LEAD < len(items):
            nstep, nst = items[i + SCORE_LEAD]
            ahead[(nstep, id(nst))] = intra(nst, nstep)
        ia, ib = d["a"], d["b_"]
        v = d["v"][cur["rows"], st["vsl"]]
        state = d["s"][st["h"]]
        att = [None, None]
        att[ia] = jnp.where(d["mask_a"], cur["att"][0], 0.0).astype(bf)
        att[ib] = jnp.where(d["mask_b"], cur["att"][1], 0.0).astype(bf)
        lhs = jnp.concatenate([jnp.concatenate([cur["q_rows"][0], att[0]], axis=1),
                               jnp.concatenate([cur["q_rows"][1], att[1]], axis=1)], axis=0)
        rhs = jnp.concatenate([state.astype(bf), v], axis=0)
        o = jnp.dot(lhs, rhs, preferred_element_type=jnp.float32)
        if accumulate:
            o = o + d["o"][cur["rows"], st["vsl"]].astype(jnp.float32)
        d["o"][cur["rows"], st["vsl"]] = o.astype(d["o"].dtype)
        upd = lax.dot_general(cur["k_pair"], v, (((0,), (0,)), ((), ())),
                              preferred_element_type=jnp.float32)
        c0 = 2 * cur["p"]
        decay = st["decay_t"][:, c0:c0 + 1] * st["decay_t"][:, c0 + 1:c0 + 2]
        d["s"][st["h"]] = decay * state + upd
    after_rounds()


def _out_body(x_ref, o_ref, sz_ref, yc_ref, gn_ref, wo_ref, fg_ref, out_ref):
    acc = jnp.dot(yc_ref[...], wo_ref[GLA_V_W:, :], preferred_element_type=jnp.float32)
    for h in range(GLA_HEADS):
        sl = slice(h * GLA_DV, (h + 1) * GLA_DV)
        o_h = o_ref[:, sl].astype(jnp.float32)
        ms = jnp.mean(o_h * o_h, axis=-1, keepdims=True)
        y_h = o_h * lax.rsqrt(ms + EPS) * gn_ref[...] * sz_ref[:, sl].astype(jnp.float32)
        acc = acc + jnp.dot(y_h.astype(jnp.bfloat16), wo_ref[sl, :],
                            preferred_element_type=jnp.float32)
    xo = x_ref[...] + acc
    ms = jnp.mean(xo * xo, axis=-1, keepdims=True)
    out_ref[...] = xo * lax.rsqrt(ms + EPS) * fg_ref[...]


def _mix_kernel(qf_ref, kf_ref, vf_ref, lrf_ref, qb_ref, kb_ref, vb_ref, lrb_ref,
                lrf_next_ref, lrb_next_ref, wgf_ref, wgb_ref, trif_ref, trib_ref,
                x_ref, sz_ref, yc_ref, gn_ref, wo_ref, fg_ref, out_ref, s_ref, o_ref, b_ref,
                gs_ref):
    b = pl.program_id(0)
    s = pl.program_id(1)
    nseq = pl.num_programs(0) - 1
    nblk = pl.num_programs(1)
    slot = s % 2
    cur = b % 2

    @pl.when((b == 0) & (s == 0))
    def _():
        _gate_terms_into(lrf_ref, wgf_ref, gs_ref.at[0])
        _gate_terms_into(lrb_ref, wgb_ref, gs_ref.at[1])
        _cumsum_into(gs_ref.at[0], trif_ref, b_ref.at[0, 0])
        _cumsum_into(gs_ref.at[1], trib_ref, b_ref.at[0, 1])

    def out_stage():
        _out_body(x_ref, o_ref.at[1 - cur, s], sz_ref, yc_ref, gn_ref, wo_ref, fg_ref, out_ref)

    pl.when(b == nseq)(out_stage)

    def gla(accumulate, with_out):
        if with_out:
            out_stage()

        def gates_next():
            _gate_terms_into(lrf_next_ref, wgf_ref, gs_ref.at[0])
            _gate_terms_into(lrb_next_ref, wgb_ref, gs_ref.at[1])

        def sums_next():
            _cumsum_into(gs_ref.at[0], trif_ref, b_ref.at[1 - slot, 0])
            _cumsum_into(gs_ref.at[1], trib_ref, b_ref.at[1 - slot, 1])

        _gla_body(qf_ref, kf_ref, vf_ref, b_ref.at[slot, 0], qb_ref, kb_ref, vb_ref,
                  b_ref.at[slot, 1], o_ref.at[cur, s], o_ref.at[cur, nblk - 1 - s], s_ref, s == 0,
                  gates_next, sums_next, accumulate)

    for accumulate in (False, True):
        in_half = (s >= nblk // 2) if accumulate else (s < nblk // 2)
        pl.when((b == 0) & in_half)(functools.partial(gla, accumulate, False))
        pl.when((b >= 1) & (b < nseq) & in_half)(functools.partial(gla, accumulate, True))


def _mix(q, k, v, lr, wg_f, wg_b, x2, sz, yc, gn, wo, fg, *, batch, seq):
    ts = TILE
    nblk = seq // ts
    assert nblk % 2 == 0
    gla_b = lambda b: jnp.minimum(b, batch - 1)
    gla_s = lambda b, s: jnp.where(b < batch, s, nblk - 1)

    def following(b, s):
        nxt = jnp.minimum(gla_b(b) * nblk + gla_s(b, s) + 1, batch * nblk - 1)
        return nxt // nblk, nxt % nblk

    def fwd_next(b, s):
        b1, s1 = following(b, s)
        return (b1 * nblk + s1, 0)

    def bwd_next(b, s):
        b1, s1 = following(b, s)
        return (b1 * nblk + (nblk - 1 - s1), 0)

    fwd = lambda b, s: (gla_b(b) * nblk + gla_s(b, s), 0)
    bwd = lambda b, s: (gla_b(b) * nblk + (nblk - 1 - gla_s(b, s)), 0)
    tile = lambda b, s: (jnp.maximum(b - 1, 0) * nblk + jnp.where(b >= 1, s, 0), 0)
    rows = lambda idx: [pl.BlockSpec((ts, w), idx) for w in (GLA_QK_W, GLA_QK_W, GLA_V_W, LR_W)]
    const = lambda shp, **kw: pl.BlockSpec(shp, lambda b, s: (0, 0), **kw)
    return pl.pallas_call(
        _mix_kernel,
        grid=(batch + 1, nblk),
        in_specs=(rows(fwd) + rows(bwd)
                  + [pl.BlockSpec((ts, LR_W), fwd_next), pl.BlockSpec((ts, LR_W), bwd_next)]
                  + [const((LR_W, GLA_QK_W))] * 2
                  + [const((2 * CHUNK, 4 * CHUNK))] * 2
                  + [pl.BlockSpec((ts, D_MODEL), tile), pl.BlockSpec((ts, GLA_V_W), tile),
                     pl.BlockSpec((ts, CONV_W), tile), const((1, GLA_DV)),
                     const((GLA_V_W + CONV_W, D_MODEL), pipeline_mode=pl.Buffered(1)),
                     const((1, D_MODEL))]),
        out_specs=pl.BlockSpec((ts, D_MODEL), tile),
        out_shape=jax.ShapeDtypeStruct((batch * seq, D_MODEL), jnp.float32),
        scratch_shapes=[pltpu.VMEM((2, GLA_HEADS, GLA_DK, GLA_DV), jnp.float32),
                        pltpu.VMEM((2, nblk, ts, GLA_V_W), jnp.bfloat16),
                        pltpu.VMEM((2, 2, ts, GLA_QK_W), jnp.float32),
                        pltpu.VMEM((2, 2 * ts, GLA_QK_W), jnp.bfloat16)],
        compiler_params=pltpu.CompilerParams(
            dimension_semantics=("arbitrary", "arbitrary"), vmem_limit_bytes=VMEM_LIMIT),
        name="mix",
    )(q, k, v, lr, q, k, v, lr, lr, lr, wg_f, wg_b, _pair_tri(False), _pair_tri(True),
      x2, sz, yc, gn, wo, fg)


def _regroup_kernel(a_ref, b_ref, o_ref):
    j = pl.program_id(0)
    n_before = _OFF_BG // REGROUP_TILE
    n_main = _OFF_LR // REGROUP_TILE
    lr = 2 * GATE_RANK

    def put(rows):
        o_ref[...] = jnp.transpose(rows).astype(o_ref.dtype)

    @pl.when(j < n_before)
    def _():
        put(a_ref[...])

    @pl.when((j >= n_before) & (j < n_main))
    def _():
        put(jnp.concatenate([a_ref[lr:, :], b_ref[...]], axis=0))

    @pl.when(j == n_main)
    def _():
        put(jnp.concatenate([a_ref[:lr, :], jnp.zeros((REGROUP_TILE - lr, a_ref.shape[1]),
                                                      a_ref.dtype)], axis=0))


def _regroup_w_in(w):
    wt = jnp.swapaxes(w, 1, 2)
    _, n, k = wt.shape
    lr = 2 * GATE_RANK
    assert lr == REGROUP_HALO and _OFF_BG % REGROUP_TILE == 0 and _OFF_LR % REGROUP_TILE == 0
    n_main = _OFF_LR // REGROUP_TILE
    sub = REGROUP_TILE // REGROUP_HALO
    first = lambda j: jnp.where(j == n_main, _OFF_BG // REGROUP_TILE, j)
    return pl.pallas_call(
        _regroup_kernel,
        grid=(W_ALL // REGROUP_TILE,),
        in_specs=[pl.BlockSpec((None, REGROUP_TILE, k), lambda j: (0, first(j), 0)),
                  pl.BlockSpec((None, REGROUP_HALO, k),
                               lambda j: (0, jnp.minimum((j + 1) * sub, n // REGROUP_HALO - 1), 0))],
        out_specs=pl.BlockSpec((k, REGROUP_TILE), lambda j: (0, j)),
        out_shape=jax.ShapeDtypeStruct((k, W_ALL), jnp.bfloat16),
        compiler_params=pltpu.CompilerParams(dimension_semantics=("arbitrary",)),
        name="regroup",
    )(wt, wt)


def _gate_weight(w_gk, b_gk, first_row):
    bias = b_gk * LOG2_E
    bias_hi = bias.astype(jnp.bfloat16)
    bias_lo = (bias - bias_hi.astype(jnp.float32)).astype(jnp.bfloat16)
    full = jnp.zeros((LR_W, GLA_QK_W), jnp.bfloat16)
    full = lax.dynamic_update_slice(full, (w_gk * LOG2_E).astype(jnp.bfloat16), (first_row, 0))
    return lax.dynamic_update_slice(full, jnp.stack([bias_hi, bias_lo]), (ONES_COL, 0))


def kernel(x, norm_g, w_in, w_gk_f, b_gk_f, w_gk_b, b_gk_b, gla_norm_g, conv_w, conv_b, w_out, final_g):
    batch, seq, d = x.shape
    depth = w_in.shape[0]
    assert depth == 1 and d == D_MODEL
    assert seq % TILE == 0
    x2 = x.reshape(batch * seq, d)
    q, k, v, sz, yc, lr = _inproj(x2, norm_g[0][None, :], _regroup_w_in(w_in),
                                  conv_w[0], conv_b[0][None, :], seq=seq)
    out = _mix(q, k, v, lr, _gate_weight(w_gk_f[0], b_gk_f[0], 0),
               _gate_weight(w_gk_b[0], b_gk_b[0], GATE_RANK), x2, sz, yc,
               gla_norm_g[0][None, :], w_out[0].astype(jnp.bfloat16), final_g[None, :],
               batch=batch, seq=seq)
    return out.reshape(batch, seq, d)
```

```python
import functools

import jax
import jax.numpy as jnp
import numpy as np
from jax import lax
from jax.experimental import pallas as pl
from jax.experimental.pallas import tpu as pltpu

D_MODEL = 1024
GLA_HEADS = 4
GLA_DK = 128
GLA_DV = 256
GLA_QK_W = GLA_HEADS * GLA_DK
GLA_V_W = GLA_HEADS * GLA_DV
GATE_RANK = 16
GATE_NORM = 16.0
CHUNK = 64
CONV_W = 1024
EPS = 1e-6
LOG2_E = 1.4426950408889634

LANES = 128
HALO = 8
LR_W = LANES
ONES_COL = 2 * GATE_RANK

_OFF_Q = 0
_OFF_K = _OFF_Q + GLA_QK_W
_OFF_V = _OFF_K + GLA_QK_W
_OFF_ZA = _OFF_V + GLA_V_W
_OFF_BG = _OFF_ZA + GLA_V_W
_OFF_CG = _OFF_BG + CONV_W
_OFF_HC = _OFF_CG + CONV_W
_OFF_ZC = _OFF_HC + CONV_W
_OFF_LR = _OFF_ZC + CONV_W
REGROUP_TILE = 4 * LANES
REGROUP_HALO = 2 * GATE_RANK
W_ALL = _OFF_LR + REGROUP_TILE

TILE = 512
VMEM_LIMIT = 58 * 1024 * 1024
SCORE_LEAD = 12


def _silu(z):
    return z * (1.0 / (1.0 + jnp.exp(-z)))


def _inproj_kernel(x_ref, xp_ref, xn_ref, g_ref, w_ref, cw_ref, cb_ref,
                   q_ref, k_ref, v_ref, sz_ref, yc_ref, lr_ref, *, tiles_per_seq):
    tm = x_ref.shape[0]
    pos = pl.program_id(0) % tiles_per_seq

    def normed(x):
        ms = jnp.mean(x * x, axis=-1, keepdims=True)
        return x * lax.rsqrt(ms + EPS) * g_ref[...]

    h = normed(x_ref[...]).astype(jnp.bfloat16)
    halo = jnp.concatenate([normed(xp_ref[...]), normed(xn_ref[...])], axis=0).astype(jnp.bfloat16)
    h_ext = jnp.concatenate([h, halo], axis=0)

    def proj(lo, hi, lhs=h):
        return jnp.dot(lhs, w_ref[:, lo:hi], preferred_element_type=jnp.float32)

    u_ext = proj(_OFF_CG, _OFF_HC, h_ext) * proj(_OFF_HC, _OFF_ZC, h_ext)
    u = u_ext[:tm]
    prev_row = jnp.where(pos == 0, 0.0, u_ext[tm + HALO - 1:tm + HALO])
    next_row = jnp.where(pos == tiles_per_seq - 1, 0.0, u_ext[tm + HALO:tm + HALO + 1])
    t_i = lax.broadcasted_iota(jnp.int32, (tm, 1), 0)
    u_prev = jnp.where(t_i == 0, prev_row, pltpu.roll(u, 1, 0))
    u_next = jnp.where(t_i == tm - 1, next_row, pltpu.roll(u, tm - 1, 0))
    conv = cw_ref[0:1, :] * u_prev + cw_ref[1:2, :] * u + cw_ref[2:3, :] * u_next + cb_ref[...]
    yc_ref[...] = (proj(_OFF_BG, _OFF_CG) * conv * _silu(proj(_OFF_ZC, _OFF_LR))).astype(yc_ref.dtype)

    sz_ref[...] = _silu(proj(_OFF_ZA, _OFF_BG)).astype(sz_ref.dtype)
    q_ref[...] = (proj(_OFF_Q, _OFF_K) * (GLA_DK ** -0.5)).astype(q_ref.dtype)
    k_ref[...] = proj(_OFF_K, _OFF_V).astype(k_ref.dtype)
    v_ref[...] = proj(_OFF_V, _OFF_ZA).astype(v_ref.dtype)
    col = lax.broadcasted_iota(jnp.int32, (1, LR_W), 1)
    ones = jnp.where((col == ONES_COL) | (col == ONES_COL + 1), 1.0, 0.0)
    lr_ref[...] = proj(_OFF_LR, _OFF_LR + LR_W) + ones


def _inproj(x2, norm_g, w_all, conv_w, conv_b, *, seq):
    m = x2.shape[0]
    tm = TILE
    sub = tm // HALO
    nsub = m // HALO
    row = lambda w: pl.BlockSpec((tm, w), lambda i: (i, 0))
    const = lambda shp: pl.BlockSpec(shp, lambda i: (0, 0))
    halo_prev = pl.BlockSpec((HALO, D_MODEL), lambda i: (jnp.maximum(i * sub - 1, 0), 0))
    halo_next = pl.BlockSpec((HALO, D_MODEL), lambda i: (jnp.minimum((i + 1) * sub, nsub - 1), 0))
    bf = jnp.bfloat16
    return pl.pallas_call(
        functools.partial(_inproj_kernel, tiles_per_seq=seq // tm),
        grid=(m // tm,),
        in_specs=[
            row(D_MODEL), halo_prev, halo_next,
            const((1, D_MODEL)),
            pl.BlockSpec((D_MODEL, W_ALL), lambda i: (0, 0), pipeline_mode=pl.Buffered(1)),
            const((3, CONV_W)), const((1, CONV_W)),
        ],
        out_specs=[row(GLA_QK_W), row(GLA_QK_W), row(GLA_V_W), row(GLA_V_W),
                   row(CONV_W), row(LR_W)],
        out_shape=[
            jax.ShapeDtypeStruct((m, GLA_QK_W), bf),
            jax.ShapeDtypeStruct((m, GLA_QK_W), bf),
            jax.ShapeDtypeStruct((m, GLA_V_W), bf),
            jax.ShapeDtypeStruct((m, GLA_V_W), bf),
            jax.ShapeDtypeStruct((m, CONV_W), bf),
            jax.ShapeDtypeStruct((m, LR_W), jnp.float32),
        ],
        compiler_params=pltpu.CompilerParams(
            dimension_semantics=("arbitrary",), vmem_limit_bytes=VMEM_LIMIT),
        name="inproj",
    )(x2, x2, x2, norm_g, w_all, conv_w, conv_b)


def _pair_tri(rev):
    t = np.arange(CHUNK)
    tri = (t[None, :] >= t[:, None]) if rev else (t[None, :] <= t[:, None])
    zero = np.zeros_like(tri)
    keep = np.block([[tri, tri, zero, zero], [zero, zero, tri, tri]])
    return jnp.asarray(keep / GATE_NORM, dtype=jnp.bfloat16)


def _gate_terms_into(lr_ref, wg_ref, gs_ref):
    ts = lr_ref.shape[0]
    x = jnp.dot(lr_ref[...].astype(jnp.bfloat16), wg_ref[...], preferred_element_type=jnp.float32)
    g = jnp.minimum(x, 0.0) - jnp.log2(1.0 + jnp.exp2(-jnp.abs(x)))
    g_top = lax.bitcast_convert_type(
        lax.bitcast_convert_type(g, jnp.uint32) & jnp.uint32(0xFFFF0000), jnp.float32)
    g_hi = g_top.astype(gs_ref.dtype)
    g_lo = (g - g_top).astype(gs_ref.dtype)
    for c in range(ts // CHUNK):
        rows = slice(c * CHUNK, (c + 1) * CHUNK)
        gs_ref[2 * c * CHUNK:(2 * c + 1) * CHUNK, :] = g_hi[rows]
        gs_ref[(2 * c + 1) * CHUNK:(2 * c + 2) * CHUNK, :] = g_lo[rows]


def _cumsum_into(gs_ref, tri_ref, b_ref):
    tri = tri_ref[...]
    for p in range(b_ref.shape[0] // (2 * CHUNK)):
        b_ref[2 * p * CHUNK:(2 * p + 2) * CHUNK, :] = jnp.dot(
            tri, gs_ref[4 * p * CHUNK:(4 * p + 4) * CHUNK, :], preferred_element_type=jnp.float32)


def _gla_body(qf_ref, kf_ref, vf_ref, bf_ref, qb_ref, kb_ref, vb_ref, bb_ref,
              of_ref, ob_ref, s_ref, first_block, mid_round, after_rounds, accumulate):
    ts = qf_ref.shape[0]
    nc = ts // CHUNK
    npair = nc // 2
    pair = 2 * CHUNK

    @pl.when(first_block)
    def _():
        s_ref[...] = jnp.zeros_like(s_ref)

    ri = lax.broadcasted_iota(jnp.int32, (CHUNK, pair), 0)
    li = lax.broadcasted_iota(jnp.int32, (CHUNK, pair), 1)
    lo_half = li < CHUNK
    dirs = (
        dict(q=qf_ref, k=kf_ref, v=vf_ref, o=of_ref, s=s_ref.at[0], a=0, b_=1,
             mask_a=lo_half & (li <= ri), mask_b=lo_half | (li - CHUNK <= ri),
             ref_row=CHUNK // 2, last_row=CHUNK - 1, order=tuple(range(npair)),
             b=bf_ref),
        dict(q=qb_ref, k=kb_ref, v=vb_ref, o=ob_ref, s=s_ref.at[1], a=1, b_=0,
             mask_a=(~lo_half) & (li - CHUNK > ri), mask_b=(~lo_half) | (li > ri),
             ref_row=CHUNK - 1 - CHUNK // 2, last_row=0, order=tuple(range(npair - 1, -1, -1)),
             b=bb_ref),
    )

    streams = []
    for d in dirs:
        b_last_rows = jnp.concatenate(
            [d["b"][c * CHUNK + d["last_row"]:c * CHUNK + d["last_row"] + 1, :] for c in range(nc)],
            axis=0)
        for h in range(GLA_HEADS):
            ksl = slice(h * GLA_DK, (h + 1) * GLA_DK)
            streams.append(dict(
                d=d, h=h, ksl=ksl, vsl=slice(h * GLA_DV, (h + 1) * GLA_DV),
                decay_t=jnp.transpose(jnp.exp2(b_last_rows[:, ksl])),
            ))

    bf = jnp.bfloat16
    zeros_k = jnp.zeros((CHUNK, GLA_DK), bf)

    def intra(st, step):
        d = st["d"]
        p = d["order"][step]
        rows = slice(p * pair, (p + 1) * pair)
        ia, ib = d["a"], d["b_"]
        b = d["b"][rows, st["ksl"]].reshape(2, CHUNK, GLA_DK)
        b_mid = b[:, d["ref_row"]:d["ref_row"] + 1, :]
        tot = b[:, d["last_row"]:d["last_row"] + 1, :]
        qe = (d["q"][rows, st["ksl"]].astype(jnp.float32).reshape(2, CHUNK, GLA_DK)
              * jnp.exp2(b - b_mid))
        ke = (d["k"][rows, st["ksl"]].astype(jnp.float32).reshape(2, CHUNK, GLA_DK)
              * jnp.exp2(b_mid - b))
        f_q = jnp.exp2(b_mid)
        f_k = jnp.exp2(tot - b_mid)
        q_in_a = (qe[ia] * f_q[ia]).astype(bf)
        q_in_b = (qe[ib] * f_q[ib]).astype(bf)
        q_in_b2 = (qe[ib] * (f_q[ib] * jnp.exp2(tot[ia]))).astype(bf)
        k_out_a = (ke[ia] * f_k[ia]).astype(bf)
        k_out_a2 = (ke[ia] * (f_k[ia] * jnp.exp2(tot[ib]))).astype(bf)
        k_out_b = (ke[ib] * f_k[ib]).astype(bf)
        ke_a = ke[ia].astype(bf)
        ke_b = ke[ib].astype(bf)
        nt = (((1,), (1,)), ((), ()))
        half = lambda x, slot: jnp.concatenate([x, zeros_k] if slot == 0 else [zeros_k, x], axis=0)
        att_a = lax.dot_general(qe[ia].astype(bf), half(ke_a, ia), nt,
                                preferred_element_type=jnp.float32)
        rhs_b = [None, None]
        rhs_b[ia] = jnp.concatenate([k_out_a, zeros_k], axis=1)
        rhs_b[ib] = jnp.concatenate([zeros_k, ke_b], axis=1)
        att_b = lax.dot_general(jnp.concatenate([q_in_b, qe[ib].astype(bf)], axis=1),
                                jnp.concatenate(rhs_b, axis=0), nt,
                                preferred_element_type=jnp.float32)
        q_rows = [None, None]
        q_rows[ia], q_rows[ib] = q_in_a, q_in_b2
        k_rows = [None, None]
        k_rows[ia], k_rows[ib] = k_out_a2, k_out_b
        return dict(p=p, rows=rows, att=(att_a, att_b), q_rows=q_rows,
                    k_pair=jnp.concatenate(k_rows, axis=0))

    items = [(step, st) for step in range(npair) for st in streams]
    ahead = {(step, id(st)): intra(st, step) for step, st in items[:SCORE_LEAD]}
    for i, (step, st) in enumerate(items):
        if step == npair // 2 and st is streams[0]:
            mid_round()
        d = st["d"]
        cur = ahead.pop((step, id(st)))
        if i + SCORE_LEAD < len(items):
            nstep, nst = items[i + SCORE_LEAD]
            ahead[(nstep, id(nst))] = intra(nst, nstep)
        ia, ib = d["a"], d["b_"]
        v = d["v"][cur["rows"], st["vsl"]]
        state = d["s"][st["h"]]
        att = [None, None]
        att[ia] = jnp.where(d["mask_a"], cur["att"][0], 0.0).astype(bf)
        att[ib] = jnp.where(d["mask_b"], cur["att"][1], 0.0).astype(bf)
        lhs = jnp.concatenate([jnp.concatenate([cur["q_rows"][0], att[0]], axis=1),
                               jnp.concatenate([cur["q_rows"][1], att[1]], axis=1)], axis=0)
        rhs = jnp.concatenate([state.astype(bf), v], axis=0)
        o = jnp.dot(lhs, rhs, preferred_element_type=jnp.float32)
        if accumulate:
            o = o + d["o"][cur["rows"], st["vsl"]].astype(jnp.float32)
        d["o"][cur["rows"], st["vsl"]] = o.astype(d["o"].dtype)
        upd = lax.dot_general(cur["k_pair"], v, (((0,), (0,)), ((), ())),
                              preferred_element_type=jnp.float32)
        c0 = 2 * cur["p"]
        decay = st["decay_t"][:, c0:c0 + 1] * st["decay_t"][:, c0 + 1:c0 + 2]
        d["s"][st["h"]] = decay * state + upd
    after_rounds()


def _out_body(x_ref, o_ref, sz_ref, yc_ref, gn_ref, wo_ref, fg_ref, out_ref):
    acc = jnp.dot(yc_ref[...], wo_ref[GLA_V_W:, :], preferred_element_type=jnp.float32)
    for h in range(GLA_HEADS):
        sl = slice(h * GLA_DV, (h + 1) * GLA_DV)
        o_h = o_ref[:, sl].astype(jnp.float32)
        ms = jnp.mean(o_h * o_h, axis=-1, keepdims=True)
        y_h = o_h * lax.rsqrt(ms + EPS) * gn_ref[...] * sz_ref[:, sl].astype(jnp.float32)
        acc = acc + jnp.dot(y_h.astype(jnp.bfloat16), wo_ref[sl, :],
                            preferred_element_type=jnp.float32)
    xo = x_ref[...] + acc
    ms = jnp.mean(xo * xo, axis=-1, keepdims=True)
    out_ref[...] = xo * lax.rsqrt(ms + EPS) * fg_ref[...]


def _mix_kernel(qf_ref, kf_ref, vf_ref, lrf_ref, qb_ref, kb_ref, vb_ref, lrb_ref,
                lrf_next_ref, lrb_next_ref, wgf_ref, wgb_ref, trif_ref, trib_ref,
                x_ref, sz_ref, yc_ref, gn_ref, wo_ref, fg_ref, out_ref, s_ref, o_ref, b_ref,
                gs_ref):
    b = pl.program_id(0)
    s = pl.program_id(1)
    nseq = pl.num_programs(0) - 1
    nblk = pl.num_programs(1)
    slot = s % 2
    cur = b % 2

    @pl.when((b == 0) & (s == 0))
    def _():
        _gate_terms_into(lrf_ref, wgf_ref, gs_ref.at[0])
        _gate_terms_into(lrb_ref, wgb_ref, gs_ref.at[1])
        _cumsum_into(gs_ref.at[0], trif_ref, b_ref.at[0, 0])
        _cumsum_into(gs_ref.at[1], trib_ref, b_ref.at[0, 1])

    @pl.when(b >= 1)
    def _():
        _out_body(x_ref, o_ref.at[1 - cur, s], sz_ref, yc_ref, gn_ref, wo_ref, fg_ref, out_ref)

    def gla(accumulate):
        def gates_next():
            _gate_terms_into(lrf_next_ref, wgf_ref, gs_ref.at[0])
            _gate_terms_into(lrb_next_ref, wgb_ref, gs_ref.at[1])

        def sums_next():
            _cumsum_into(gs_ref.at[0], trif_ref, b_ref.at[1 - slot, 0])
            _cumsum_into(gs_ref.at[1], trib_ref, b_ref.at[1 - slot, 1])

        _gla_body(qf_ref, kf_ref, vf_ref, b_ref.at[slot, 0], qb_ref, kb_ref, vb_ref,
                  b_ref.at[slot, 1], o_ref.at[cur, s], o_ref.at[cur, nblk - 1 - s], s_ref, s == 0,
                  gates_next, sums_next, accumulate)

    @pl.when((b < nseq) & (s < nblk // 2))
    def _():
        o_ref[cur, s] = jnp.zeros(o_ref.shape[2:], o_ref.dtype)
        o_ref[cur, nblk - 1 - s] = jnp.zeros(o_ref.shape[2:], o_ref.dtype)

    @pl.when(b < nseq)
    def _():
        gla(True)


def _mix(q, k, v, lr, wg_f, wg_b, x2, sz, yc, gn, wo, fg, *, batch, seq):
    ts = TILE
    nblk = seq // ts
    assert nblk % 2 == 0
    gla_b = lambda b: jnp.minimum(b, batch - 1)
    gla_s = lambda b, s: jnp.where(b < batch, s, nblk - 1)

    def following(b, s):
        nxt = jnp.minimum(gla_b(b) * nblk + gla_s(b, s) + 1, batch * nblk - 1)
        return nxt // nblk, nxt % nblk

    def fwd_next(b, s):
        b1, s1 = following(b, s)
        return (b1 * nblk + s1, 0)

    def bwd_next(b, s):
        b1, s1 = following(b, s)
        return (b1 * nblk + (nblk - 1 - s1), 0)

    fwd = lambda b, s: (gla_b(b) * nblk + gla_s(b, s), 0)
    bwd = lambda b, s: (gla_b(b) * nblk + (nblk - 1 - gla_s(b, s)), 0)
    tile = lambda b, s: (jnp.maximum(b - 1, 0) * nblk + jnp.where(b >= 1, s, 0), 0)
    rows = lambda idx: [pl.BlockSpec((ts, w), idx) for w in (GLA_QK_W, GLA_QK_W, GLA_V_W, LR_W)]
    const = lambda shp, **kw: pl.BlockSpec(shp, lambda b, s: (0, 0), **kw)
    return pl.pallas_call(
        _mix_kernel,
        grid=(batch + 1, nblk),
        in_specs=(rows(fwd) + rows(bwd)
                  + [pl.BlockSpec((ts, LR_W), fwd_next), pl.BlockSpec((ts, LR_W), bwd_next)]
                  + [const((LR_W, GLA_QK_W))] * 2
                  + [const((2 * CHUNK, 4 * CHUNK))] * 2
                  + [pl.BlockSpec((ts, D_MODEL), tile), pl.BlockSpec((ts, GLA_V_W), tile),
                     pl.BlockSpec((ts, CONV_W), tile), const((1, GLA_DV)),
                     const((GLA_V_W + CONV_W, D_MODEL), pipeline_mode=pl.Buffered(1)),
                     const((1, D_MODEL))]),
        out_specs=pl.BlockSpec((ts, D_MODEL), tile),
        out_shape=jax.ShapeDtypeStruct((batch * seq, D_MODEL), jnp.float32),
        scratch_shapes=[pltpu.VMEM((2, GLA_HEADS, GLA_DK, GLA_DV), jnp.float32),
                        pltpu.VMEM((2, nblk, ts, GLA_V_W), jnp.bfloat16),
                        pltpu.VMEM((2, 2, ts, GLA_QK_W), jnp.float32),
                        pltpu.VMEM((2, 2 * ts, GLA_QK_W), jnp.bfloat16)],
        compiler_params=pltpu.CompilerParams(
            dimension_semantics=("arbitrary", "arbitrary"), vmem_limit_bytes=VMEM_LIMIT),
        name="mix",
    )(q, k, v, lr, q, k, v, lr, lr, lr, wg_f, wg_b, _pair_tri(False), _pair_tri(True),
      x2, sz, yc, gn, wo, fg)


def _regroup_kernel(a_ref, b_ref, o_ref):
    j = pl.program_id(0)
    n_before = _OFF_BG // REGROUP_TILE
    n_main = _OFF_LR // REGROUP_TILE
    lr = 2 * GATE_RANK

    def put(rows):
        o_ref[...] = jnp.transpose(rows).astype(o_ref.dtype)

    @pl.when(j < n_before)
    def _():
        put(a_ref[...])

    @pl.when((j >= n_before) & (j < n_main))
    def _():
        put(jnp.concatenate([a_ref[lr:, :], b_ref[...]], axis=0))

    @pl.when(j == n_main)
    def _():
        put(jnp.concatenate([a_ref[:lr, :], jnp.zeros((REGROUP_TILE - lr, a_ref.shape[1]),
                                                      a_ref.dtype)], axis=0))


def _regroup_w_in(w):
    wt = jnp.swapaxes(w, 1, 2)
    _, n, k = wt.shape
    lr = 2 * GATE_RANK
    assert lr == REGROUP_HALO and _OFF_BG % REGROUP_TILE == 0 and _OFF_LR % REGROUP_TILE == 0
    n_main = _OFF_LR // REGROUP_TILE
    sub = REGROUP_TILE // REGROUP_HALO
    first = lambda j: jnp.where(j == n_main, _OFF_BG // REGROUP_TILE, j)
    return pl.pallas_call(
        _regroup_kernel,
        grid=(W_ALL // REGROUP_TILE,),
        in_specs=[pl.BlockSpec((None, REGROUP_TILE, k), lambda j: (0, first(j), 0)),
                  pl.BlockSpec((None, REGROUP_HALO, k),
                               lambda j: (0, jnp.minimum((j + 1) * sub, n // REGROUP_HALO - 1), 0))],
        out_specs=pl.BlockSpec((k, REGROUP_TILE), lambda j: (0, j)),
        out_shape=jax.ShapeDtypeStruct((k, W_ALL), jnp.bfloat16),
        compiler_params=pltpu.CompilerParams(dimension_semantics=("arbitrary",)),
        name="regroup",
    )(wt, wt)


def _gate_weight(w_gk, b_gk, first_row):
    bias = b_gk * LOG2_E
    bias_hi = bias.astype(jnp.bfloat16)
    bias_lo = (bias - bias_hi.astype(jnp.float32)).astype(jnp.bfloat16)
    full = jnp.zeros((LR_W, GLA_QK_W), jnp.bfloat16)
    full = lax.dynamic_update_slice(full, (w_gk * LOG2_E).astype(jnp.bfloat16), (first_row, 0))
    return lax.dynamic_update_slice(full, jnp.stack([bias_hi, bias_lo]), (ONES_COL, 0))


def kernel(x, norm_g, w_in, w_gk_f, b_gk_f, w_gk_b, b_gk_b, gla_norm_g, conv_w, conv_b, w_out, final_g):
    batch, seq, d = x.shape
    depth = w_in.shape[0]
    assert depth == 1 and d == D_MODEL
    assert seq % TILE == 0
    x2 = x.reshape(batch * seq, d)
    q, k, v, sz, yc, lr = _inproj(x2, norm_g[0][None, :], _regroup_w_in(w_in),
                                  conv_w[0], conv_b[0][None, :], seq=seq)
    out = _mix(q, k, v, lr, _gate_weight(w_gk_f[0], b_gk_f[0], 0),
               _gate_weight(w_gk_b[0], b_gk_b[0], GATE_RANK), x2, sz, yc,
               gla_norm_g[0][None, :], w_out[0].astype(jnp.bfloat16), final_g[None, :],
               batch=batch, seq=seq)
    return out.reshape(batch, seq, d)
```

```python
import functools

import jax
import jax.numpy as jnp
import numpy as np
from jax import lax
from jax.experimental import pallas as pl
from jax.experimental.pallas import tpu as pltpu

D_MODEL = 1024
GLA_HEADS = 4
GLA_DK = 128
GLA_DV = 256
GLA_QK_W = GLA_HEADS * GLA_DK
GLA_V_W = GLA_HEADS * GLA_DV
GATE_RANK = 16
GATE_NORM = 16.0
CHUNK = 64
CONV_W = 1024
EPS = 1e-6
LOG2_E = 1.4426950408889634

LANES = 128
HALO = 8
LR_W = LANES
ONES_COL = 2 * GATE_RANK

_OFF_Q = 0
_OFF_K = _OFF_Q + GLA_QK_W
_OFF_V = _OFF_K + GLA_QK_W
_OFF_ZA = _OFF_V + GLA_V_W
_OFF_BG = _OFF_ZA + GLA_V_W
_OFF_CG = _OFF_BG + CONV_W
_OFF_HC = _OFF_CG + CONV_W
_OFF_ZC = _OFF_HC + CONV_W
_OFF_LR = _OFF_ZC + CONV_W
REGROUP_TILE = 4 * LANES
REGROUP_HALO = 2 * GATE_RANK
W_ALL = _OFF_LR + REGROUP_TILE

TILE = 512
VMEM_LIMIT = 58 * 1024 * 1024
SCORE_LEAD = 12


def _silu(z):
    return z * (1.0 / (1.0 + jnp.exp(-z)))


def _inproj_kernel(x_ref, xp_ref, xn_ref, g_ref, w_ref, cw_ref, cb_ref,
                   q_ref, k_ref, v_ref, sz_ref, yc_ref, lr_ref, *, tiles_per_seq):
    tm = x_ref.shape[0]
    pos = pl.program_id(0) % tiles_per_seq

    def normed(x):
        ms = jnp.mean(x * x, axis=-1, keepdims=True)
        return x * lax.rsqrt(ms + EPS) * g_ref[...]

    h = normed(x_ref[...]).astype(jnp.bfloat16)
    halo = jnp.concatenate([normed(xp_ref[...]), normed(xn_ref[...])], axis=0).astype(jnp.bfloat16)
    h_ext = jnp.concatenate([h, halo], axis=0)

    def proj(lo, hi, lhs=h):
        return jnp.dot(lhs, w_ref[:, lo:hi], preferred_element_type=jnp.float32)

    u_ext = proj(_OFF_CG, _OFF_HC, h_ext) * proj(_OFF_HC, _OFF_ZC, h_ext)
    u = u_ext[:tm]
    prev_row = jnp.where(pos == 0, 0.0, u_ext[tm + HALO - 1:tm + HALO])
    next_row = jnp.where(pos == tiles_per_seq - 1, 0.0, u_ext[tm + HALO:tm + HALO + 1])
    t_i = lax.broadcasted_iota(jnp.int32, (tm, 1), 0)
    u_prev = jnp.where(t_i == 0, prev_row, pltpu.roll(u, 1, 0))
    u_next = jnp.where(t_i == tm - 1, next_row, pltpu.roll(u, tm - 1, 0))
    conv = cw_ref[0:1, :] * u_prev + cw_ref[1:2, :] * u + cw_ref[2:3, :] * u_next + cb_ref[...]
    yc_ref[...] = (proj(_OFF_BG, _OFF_CG) * conv * _silu(proj(_OFF_ZC, _OFF_LR))).astype(yc_ref.dtype)

    sz_ref[...] = _silu(proj(_OFF_ZA, _OFF_BG)).astype(sz_ref.dtype)
    q_ref[...] = (proj(_OFF_Q, _OFF_K) * (GLA_DK ** -0.5)).astype(q_ref.dtype)
    k_ref[...] = proj(_OFF_K, _OFF_V).astype(k_ref.dtype)
    v_ref[...] = proj(_OFF_V, _OFF_ZA).astype(v_ref.dtype)
    col = lax.broadcasted_iota(jnp.int32, (1, LR_W), 1)
    ones = jnp.where((col == ONES_COL) | (col == ONES_COL + 1), 1.0, 0.0)
    lr_ref[...] = proj(_OFF_LR, _OFF_LR + LR_W) + ones


def _inproj(x2, norm_g, w_all, conv_w, conv_b, *, seq):
    m = x2.shape[0]
    tm = TILE
    sub = tm // HALO
    nsub = m // HALO
    row = lambda w: pl.BlockSpec((tm, w), lambda i: (i, 0))
    const = lambda shp: pl.BlockSpec(shp, lambda i: (0, 0))
    halo_prev = pl.BlockSpec((HALO, D_MODEL), lambda i: (jnp.maximum(i * sub - 1, 0), 0))
    halo_next = pl.BlockSpec((HALO, D_MODEL), lambda i: (jnp.minimum((i + 1) * sub, nsub - 1), 0))
    bf = jnp.bfloat16
    return pl.pallas_call(
        functools.partial(_inproj_kernel, tiles_per_seq=seq // tm),
        grid=(m // tm,),
        in_specs=[
            row(D_MODEL), halo_prev, halo_next,
            const((1, D_MODEL)),
            pl.BlockSpec((D_MODEL, W_ALL), lambda i: (0, 0), pipeline_mode=pl.Buffered(1)),
            const((3, CONV_W)), const((1, CONV_W)),
        ],
        out_specs=[row(GLA_QK_W), row(GLA_QK_W), row(GLA_V_W), row(GLA_V_W),
                   row(CONV_W), row(LR_W)],
        out_shape=[
            jax.ShapeDtypeStruct((m, GLA_QK_W), bf),
            jax.ShapeDtypeStruct((m, GLA_QK_W), bf),
            jax.ShapeDtypeStruct((m, GLA_V_W), bf),
            jax.ShapeDtypeStruct((m, GLA_V_W), bf),
            jax.ShapeDtypeStruct((m, CONV_W), bf),
            jax.ShapeDtypeStruct((m, LR_W), jnp.float32),
        ],
        compiler_params=pltpu.CompilerParams(
            dimension_semantics=("arbitrary",), vmem_limit_bytes=VMEM_LIMIT),
        name="inproj",
    )(x2, x2, x2, norm_g, w_all, conv_w, conv_b)


def _pair_tri(rev):
    t = np.arange(CHUNK)
    tri = (t[None, :] >= t[:, None]) if rev else (t[None, :] <= t[:, None])
    zero = np.zeros_like(tri)
    keep = np.block([[tri, tri, zero, zero], [zero, zero, tri, tri]])
    return jnp.asarray(keep / GATE_NORM, dtype=jnp.bfloat16)


def _gate_terms_into(lr_ref, wg_ref, gs_ref):
    ts = lr_ref.shape[0]
    x = jnp.dot(lr_ref[...].astype(jnp.bfloat16), wg_ref[...], preferred_element_type=jnp.float32)
    g = jnp.minimum(x, 0.0) - jnp.log2(1.0 + jnp.exp2(-jnp.abs(x)))
    g_top = lax.bitcast_convert_type(
        lax.bitcast_convert_type(g, jnp.uint32) & jnp.uint32(0xFFFF0000), jnp.float32)
    g_hi = g_top.astype(gs_ref.dtype)
    g_lo = (g - g_top).astype(gs_ref.dtype)
    for c in range(ts // CHUNK):
        rows = slice(c * CHUNK, (c + 1) * CHUNK)
        gs_ref[2 * c * CHUNK:(2 * c + 1) * CHUNK, :] = g_hi[rows]
        gs_ref[(2 * c + 1) * CHUNK:(2 * c + 2) * CHUNK, :] = g_lo[rows]


def _cumsum_into(gs_ref, tri_ref, b_ref):
    tri = tri_ref[...]
    for p in range(b_ref.shape[0] // (2 * CHUNK)):
        b_ref[2 * p * CHUNK:(2 * p + 2) * CHUNK, :] = jnp.dot(
            tri, gs_ref[4 * p * CHUNK:(4 * p + 4) * CHUNK, :], preferred_element_type=jnp.float32)


def _gla_body(qf_ref, kf_ref, vf_ref, bf_ref, qb_ref, kb_ref, vb_ref, bb_ref,
              of_ref, ob_ref, s_ref, first_block, mid_round, after_rounds, accumulate):
    ts = qf_ref.shape[0]
    nc = ts // CHUNK
    npair = nc // 2
    pair = 2 * CHUNK

    @pl.when(first_block)
    def _():
        s_ref[...] = jnp.zeros_like(s_ref)

    ri = lax.broadcasted_iota(jnp.int32, (CHUNK, pair), 0)
    li = lax.broadcasted_iota(jnp.int32, (CHUNK, pair), 1)
    lo_half = li < CHUNK
    dirs = (
        dict(q=qf_ref, k=kf_ref, v=vf_ref, o=of_ref, s=s_ref.at[0], a=0, b_=1,
             mask_a=lo_half & (li <= ri), mask_b=lo_half | (li - CHUNK <= ri),
             ref_row=CHUNK // 2, last_row=CHUNK - 1, order=tuple(range(npair)),
             b=bf_ref),
        dict(q=qb_ref, k=kb_ref, v=vb_ref, o=ob_ref, s=s_ref.at[1], a=1, b_=0,
             mask_a=(~lo_half) & (li - CHUNK > ri), mask_b=(~lo_half) | (li > ri),
             ref_row=CHUNK - 1 - CHUNK // 2, last_row=0, order=tuple(range(npair - 1, -1, -1)),
             b=bb_ref),
    )

    streams = []
    for d in dirs:
        b_last_rows = jnp.concatenate(
            [d["b"][c * CHUNK + d["last_row"]:c * CHUNK + d["last_row"] + 1, :] for c in range(nc)],
            axis=0)
        for h in range(GLA_HEADS):
            ksl = slice(h * GLA_DK, (h + 1) * GLA_DK)
            streams.append(dict(
                d=d, h=h, ksl=ksl, vsl=slice(h * GLA_DV, (h + 1) * GLA_DV),
                decay_t=jnp.transpose(jnp.exp2(b_last_rows[:, ksl])),
            ))

    bf = jnp.bfloat16
    zeros_k = jnp.zeros((CHUNK, GLA_DK), bf)

    def intra(st, step):
        d = st["d"]
        p = d["order"][step]
        rows = slice(p * pair, (p + 1) * pair)
        ia, ib = d["a"], d["b_"]
        b = d["b"][rows, st["ksl"]].reshape(2, CHUNK, GLA_DK)
        b_mid = b[:, d["ref_row"]:d["ref_row"] + 1, :]
        tot = b[:, d["last_row"]:d["last_row"] + 1, :]
        qe = (d["q"][rows, st["ksl"]].astype(jnp.float32).reshape(2, CHUNK, GLA_DK)
              * jnp.exp2(b - b_mid))
        ke = (d["k"][rows, st["ksl"]].astype(jnp.float32).reshape(2, CHUNK, GLA_DK)
              * jnp.exp2(b_mid - b))
        f_q = jnp.exp2(b_mid)
        f_k = jnp.exp2(tot - b_mid)
        q_in_a = (qe[ia] * f_q[ia]).astype(bf)
        q_in_b = (qe[ib] * f_q[ib]).astype(bf)
        q_in_b2 = (qe[ib] * (f_q[ib] * jnp.exp2(tot[ia]))).astype(bf)
        k_out_a = (ke[ia] * f_k[ia]).astype(bf)
        k_out_a2 = (ke[ia] * (f_k[ia] * jnp.exp2(tot[ib]))).astype(bf)
        k_out_b = (ke[ib] * f_k[ib]).astype(bf)
        ke_a = ke[ia].astype(bf)
        ke_b = ke[ib].astype(bf)
        nt = (((1,), (1,)), ((), ()))
        half = lambda x, slot: jnp.concatenate([x, zeros_k] if slot == 0 else [zeros_k, x], axis=0)
        att_a = lax.dot_general(qe[ia].astype(bf), half(ke_a, ia), nt,
                                preferred_element_type=jnp.float32)
        rhs_b = [None, None]
        rhs_b[ia] = jnp.concatenate([k_out_a, zeros_k], axis=1)
        rhs_b[ib] = jnp.concatenate([zeros_k, ke_b], axis=1)
        att_b = lax.dot_general(jnp.concatenate([q_in_b, qe[ib].astype(bf)], axis=1),
                                jnp.concatenate(rhs_b, axis=0), nt,
                                preferred_element_type=jnp.float32)
        q_rows = [None, None]
        q_rows[ia], q_rows[ib] = q_in_a, q_in_b2
        k_rows = [None, None]
        k_rows[ia], k_rows[ib] = k_out_a2, k_out_b
        return dict(p=p, rows=rows, att=(att_a, att_b), q_rows=q_rows,
                    k_pair=jnp.concatenate(k_rows, axis=0))

    items = [(step, st) for step in range(npair) for st in streams]
    ahead = {(step, id(st)): intra(st, step) for step, st in items[:SCORE_LEAD]}
    for i, (step, st) in enumerate(items):
        if step == npair // 2 and st is streams[0]:
            mid_round()
        d = st["d"]
        cur = ahead.pop((step, id(st)))
        if i + SCORE_LEAD < len(items):
            nstep, nst = items[i + SCORE_LEAD]
            ahead[(nstep, id(nst))] = intra(nst, nstep)
        ia, ib = d["a"], d["b_"]
        v = d["v"][cur["rows"], st["vsl"]]
        state = d["s"][st["h"]]
        att = [None, None]
        att[ia] = jnp.where(d["mask_a"], cur["att"][0], 0.0).astype(bf)
        att[ib] = jnp.where(d["mask_b"], cur["att"][1], 0.0).astype(bf)
        lhs = jnp.concatenate([jnp.concatenate([cur["q_rows"][0], att[0]], axis=1),
                               jnp.concatenate([cur["q_rows"][1], att[1]], axis=1)], axis=0)
        rhs = jnp.concatenate([state.astype(bf), v], axis=0)
        o = jnp.dot(lhs, rhs, preferred_element_type=jnp.float32)
        if accumulate:
            o = o + d["o"][cur["rows"], st["vsl"]].astype(jnp.float32)
        d["o"][cur["rows"], st["vsl"]] = o.astype(d["o"].dtype)
        upd = lax.dot_general(cur["k_pair"], v, (((0,), (0,)), ((), ())),
                              preferred_element_type=jnp.float32)
        c0 = 2 * cur["p"]
        decay = st["decay_t"][:, c0:c0 + 1] * st["decay_t"][:, c0 + 1:c0 + 2]
        d["s"][st["h"]] = decay * state + upd
    after_rounds()


def _out_body(x_ref, o_ref, sz_ref, yc_ref, gn_ref, wo_ref, fg_ref, out_ref):
    acc = jnp.dot(yc_ref[...], wo_ref[GLA_V_W:, :], preferred_element_type=jnp.float32)
    for h in range(GLA_HEADS):
        sl = slice(h * GLA_DV, (h + 1) * GLA_DV)
        o_h = o_ref[:, sl].astype(jnp.float32)
        ms = jnp.mean(o_h * o_h, axis=-1, keepdims=True)
        y_h = o_h * lax.rsqrt(ms + EPS) * gn_ref[...] * sz_ref[:, sl].astype(jnp.float32)
        acc = acc + jnp.dot(y_h.astype(jnp.bfloat16), wo_ref[sl, :],
                            preferred_element_type=jnp.float32)
    xo = x_ref[...] + acc
    ms = jnp.mean(xo * xo, axis=-1, keepdims=True)
    out_ref[...] = xo * lax.rsqrt(ms + EPS) * fg_ref[...]


def _mix_kernel(qf_ref, kf_ref, vf_ref, lrf_ref, qb_ref, kb_ref, vb_ref, lrb_ref,
                lrf_next_ref, lrb_next_ref, wgf_ref, wgb_ref, trif_ref, trib_ref,
                x_ref, sz_ref, yc_ref, gn_ref, wo_ref, fg_ref, out_ref, b_ref, gs_ref, s_ref,
                o_ref):
    b = pl.program_id(0)
    s = pl.program_id(1)
    nseq = pl.num_programs(0) - 1
    nblk = pl.num_programs(1)
    slot = s % 2
    cur = b % 2

    @pl.when((b == 0) & (s == 0))
    def _():
        _gate_terms_into(lrf_ref, wgf_ref, gs_ref.at[0])
        _gate_terms_into(lrb_ref, wgb_ref, gs_ref.at[1])
        _cumsum_into(gs_ref.at[0], trif_ref, b_ref.at[0, 0])
        _cumsum_into(gs_ref.at[1], trib_ref, b_ref.at[0, 1])

    @pl.when(b >= 1)
    def _():
        _out_body(x_ref, o_ref.at[1 - cur, s], sz_ref, yc_ref, gn_ref, wo_ref, fg_ref, out_ref)

    def gla(accumulate):
        def gates_next():
            _gate_terms_into(lrf_next_ref, wgf_ref, gs_ref.at[0])
            _gate_terms_into(lrb_next_ref, wgb_ref, gs_ref.at[1])

        def sums_next():
            _cumsum_into(gs_ref.at[0], trif_ref, b_ref.at[1 - slot, 0])
            _cumsum_into(gs_ref.at[1], trib_ref, b_ref.at[1 - slot, 1])

        _gla_body(qf_ref, kf_ref, vf_ref, b_ref.at[slot, 0], qb_ref, kb_ref, vb_ref,
                  b_ref.at[slot, 1], o_ref.at[cur, s], o_ref.at[cur, nblk - 1 - s], s_ref, s == 0,
                  gates_next, sums_next, accumulate)

    @pl.when((b < nseq) & (s < nblk // 2))
    def _():
        gla(False)

    @pl.when((b < nseq) & (s >= nblk // 2))
    def _():
        gla(True)


def _mix(q, k, v, lr, wg_f, wg_b, x2, sz, yc, gn, wo, fg, *, batch, seq):
    ts = TILE
    nblk = seq // ts
    assert nblk % 2 == 0
    gla_b = lambda b: jnp.minimum(b, batch - 1)
    gla_s = lambda b, s: jnp.where(b < batch, s, nblk - 1)

    def following(b, s):
        nxt = jnp.minimum(gla_b(b) * nblk + gla_s(b, s) + 1, batch * nblk - 1)
        return nxt // nblk, nxt % nblk

    def fwd_next(b, s):
        b1, s1 = following(b, s)
        return (b1 * nblk + s1, 0)

    def bwd_next(b, s):
        b1, s1 = following(b, s)
        return (b1 * nblk + (nblk - 1 - s1), 0)

    fwd = lambda b, s: (gla_b(b) * nblk + gla_s(b, s), 0)
    bwd = lambda b, s: (gla_b(b) * nblk + (nblk - 1 - gla_s(b, s)), 0)
    tile = lambda b, s: (jnp.maximum(b - 1, 0) * nblk + jnp.where(b >= 1, s, 0), 0)
    rows = lambda idx: [pl.BlockSpec((ts, w), idx) for w in (GLA_QK_W, GLA_QK_W, GLA_V_W, LR_W)]
    const = lambda shp, **kw: pl.BlockSpec(shp, lambda b, s: (0, 0), **kw)
    return pl.pallas_call(
        _mix_kernel,
        grid=(batch + 1, nblk),
        in_specs=(rows(fwd) + rows(bwd)
                  + [pl.BlockSpec((ts, LR_W), fwd_next), pl.BlockSpec((ts, LR_W), bwd_next)]
                  + [const((LR_W, GLA_QK_W))] * 2
                  + [const((2 * CHUNK, 4 * CHUNK))] * 2
                  + [pl.BlockSpec((ts, D_MODEL), tile), pl.BlockSpec((ts, GLA_V_W), tile),
                     pl.BlockSpec((ts, CONV_W), tile), const((1, GLA_DV)),
                     const((GLA_V_W + CONV_W, D_MODEL), pipeline_mode=pl.Buffered(1)),
                     const((1, D_MODEL))]),
        out_specs=pl.BlockSpec((ts, D_MODEL), tile),
        out_shape=jax.ShapeDtypeStruct((batch * seq, D_MODEL), jnp.float32),
        scratch_shapes=[pltpu.VMEM((2, 2, ts, GLA_QK_W), jnp.float32),
                        pltpu.VMEM((2, 2 * ts, GLA_QK_W), jnp.bfloat16),
                        pltpu.VMEM((2, GLA_HEADS, GLA_DK, GLA_DV), jnp.float32),
                        pltpu.VMEM((2, nblk, ts, GLA_V_W), jnp.bfloat16)],
        compiler_params=pltpu.CompilerParams(
            dimension_semantics=("arbitrary", "arbitrary"), vmem_limit_bytes=VMEM_LIMIT),
        name="mix",
    )(q, k, v, lr, q, k, v, lr, lr, lr, wg_f, wg_b, _pair_tri(False), _pair_tri(True),
      x2, sz, yc, gn, wo, fg)


def _regroup_kernel(a_ref, b_ref, o_ref):
    j = pl.program_id(0)
    n_before = _OFF_BG // REGROUP_TILE
    n_main = _OFF_LR // REGROUP_TILE
    lr = 2 * GATE_RANK

    def put(rows):
        o_ref[...] = jnp.transpose(rows).astype(o_ref.dtype)

    @pl.when(j < n_before)
    def _():
        put(a_ref[...])

    @pl.when((j >= n_before) & (j < n_main))
    def _():
        put(jnp.concatenate([a_ref[lr:, :], b_ref[...]], axis=0))

    @pl.when(j == n_main)
    def _():
        put(jnp.concatenate([a_ref[:lr, :], jnp.zeros((REGROUP_TILE - lr, a_ref.shape[1]),
                                                      a_ref.dtype)], axis=0))


def _regroup_w_in(w):
    wt = jnp.swapaxes(w, 1, 2)
    _, n, k = wt.shape
    lr = 2 * GATE_RANK
    assert lr == REGROUP_HALO and _OFF_BG % REGROUP_TILE == 0 and _OFF_LR % REGROUP_TILE == 0
    n_main = _OFF_LR // REGROUP_TILE
    sub = REGROUP_TILE // REGROUP_HALO
    first = lambda j: jnp.where(j == n_main, _OFF_BG // REGROUP_TILE, j)
    return pl.pallas_call(
        _regroup_kernel,
        grid=(W_ALL // REGROUP_TILE,),
        in_specs=[pl.BlockSpec((None, REGROUP_TILE, k), lambda j: (0, first(j), 0)),
                  pl.BlockSpec((None, REGROUP_HALO, k),
                               lambda j: (0, jnp.minimum((j + 1) * sub, n // REGROUP_HALO - 1), 0))],
        out_specs=pl.BlockSpec((k, REGROUP_TILE), lambda j: (0, j)),
        out_shape=jax.ShapeDtypeStruct((k, W_ALL), jnp.bfloat16),
        compiler_params=pltpu.CompilerParams(dimension_semantics=("arbitrary",)),
        name="regroup",
    )(wt, wt)


def _gate_weight(w_gk, b_gk, first_row):
    bias = b_gk * LOG2_E
    bias_hi = bias.astype(jnp.bfloat16)
    bias_lo = (bias - bias_hi.astype(jnp.float32)).astype(jnp.bfloat16)
    full = jnp.zeros((LR_W, GLA_QK_W), jnp.bfloat16)
    full = lax.dynamic_update_slice(full, (w_gk * LOG2_E).astype(jnp.bfloat16), (first_row, 0))
    return lax.dynamic_update_slice(full, jnp.stack([bias_hi, bias_lo]), (ONES_COL, 0))


def kernel(x, norm_g, w_in, w_gk_f, b_gk_f, w_gk_b, b_gk_b, gla_norm_g, conv_w, conv_b, w_out, final_g):
    batch, seq, d = x.shape
    depth = w_in.shape[0]
    assert depth == 1 and d == D_MODEL
    assert seq % TILE == 0
    x2 = x.reshape(batch * seq, d)
    q, k, v, sz, yc, lr = _inproj(x2, norm_g[0][None, :], _regroup_w_in(w_in),
                                  conv_w[0], conv_b[0][None, :], seq=seq)
    out = _mix(q, k, v, lr, _gate_weight(w_gk_f[0], b_gk_f[0], 0),
               _gate_weight(w_gk_b[0], b_gk_b[0], GATE_RANK), x2, sz, yc,
               gla_norm_g[0][None, :], w_out[0].astype(jnp.bfloat16), final_g[None, :],
               batch=batch, seq=seq)
    return out.reshape(batch, seq, d)
```
